```python
import math
import jax
import jax.numpy as jnp
from jax import lax
import numpy as np

D_MODEL = 2048
BATCH = 2
SEQ = 8192
DEPTH = 1

HEAD_DIM = 64
RWKV_WIDTH = D_MODEL // 2
RWKV_HEADS = RWKV_WIDTH // HEAD_DIM
SB_WIDTH = D_MODEL // 2
SB_HEADS = SB_WIDTH // HEAD_DIM
DECAY_LORA = 64
ICLR_LORA = 64
GATE_LORA = 160
D_FF = 4 * D_MODEL
Q_BLOCK = 128
N_MOD = 6
NORM_EPS = 1e-6
GN_EPS = 64e-5
RWKV_COLS = 3 * RWKV_WIDTH + DECAY_LORA + ICLR_LORA + GATE_LORA
SB_COLS = 3 * SB_WIDTH
GATE_COLS = 2 * D_MODEL
IN_COLS = RWKV_COLS + SB_COLS + GATE_COLS

kernel_name = "hybrid_rwkv7_stickbreaking_block"


def _rmsnorm(x, g):
    xf = x.astype(jnp.float32)
    y = xf * lax.rsqrt(jnp.mean(xf * xf, axis=-1, keepdims=True) + NORM_EPS)
    return (y * g).astype(x.dtype)


def _token_shift(u):
    return jnp.pad(u, ((0, 0), (1, 0), (0, 0)))[:, :-1]


def _l2norm_heads(t):
    tf = t.astype(jnp.float32)
    return tf * lax.rsqrt(jnp.maximum(jnp.sum(tf * tf, axis=-1, keepdims=True), 1e-24))


def _group_norm(y):
    mean = jnp.mean(y, axis=-1, keepdims=True)
    var = jnp.mean(jnp.square(y - mean), axis=-1, keepdims=True)
    return (y - mean) * lax.rsqrt(var + GN_EPS)


def _rwkv7_scan(r, w, k, v, kk, a):
    B, S, H, N = r.shape

    def step(state, inp):
        r_t, w_t, k_t, v_t, kk_t, a_t = inp
        s_kk = jnp.einsum('bhij,bhj->bhi', state, kk_t)
        state = (state * w_t[:, :, None, :]
                 - s_kk[..., None] * (kk_t * a_t)[:, :, None, :]
                 + v_t[..., None] * k_t[:, :, None, :])
        return state, jnp.einsum('bhij,bhj->bhi', state, r_t)

    xs = tuple(jnp.moveaxis(t.astype(jnp.float32), 1, 0) for t in (r, w, k, v, kk, a))
    s0 = jnp.zeros((B, H, N, N), jnp.float32)
    _, ys = lax.scan(step, s0, xs)
    return jnp.moveaxis(ys, 0, 1)


def _stick_breaking(q, k, v):
    B, H, S, N = q.shape
    nb = S // Q_BLOCK
    qb = q.reshape(B, H, nb, Q_BLOCK, N).transpose(2, 0, 1, 3, 4)
    key_pos = jnp.arange(S)
    scale = 1.0 / math.sqrt(N)

    def block(args):
        q_blk, i = args
        z = jnp.einsum('bhqd,bhkd->bhqk', q_blk, k,
                       preferred_element_type=jnp.float32) * scale
        q_pos = i * Q_BLOCK + jnp.arange(Q_BLOCK)
        causal = key_pos[None, :] < q_pos[:, None]
        log_keep = jnp.where(causal, jax.nn.log_sigmoid(-z), 0.0)
        later = lax.cumsum(log_keep, axis=3, reverse=True) - log_keep
        att = jnp.where(causal, jnp.exp(jax.nn.log_sigmoid(z) + later), 0.0)
        return jnp.einsum('bhqk,bhkd->bhqd', att.astype(v.dtype), v)

    out = lax.map(block, (qb, jnp.arange(nb)))
    return out.transpose(1, 2, 0, 3, 4).reshape(B, H, S, N)


def setup_inputs(seed: int = 0) -> dict:
    key = jax.random.key(seed)
    ks = jax.random.split(key, 22)
    L, D, C = DEPTH, D_MODEL, RWKV_WIDTH

    def nrm(k, shape, s):
        return jax.random.normal(k, shape, jnp.float32) * s

    return {
        "x": nrm(ks[0], (BATCH, SEQ, D), 1.0),
        "c": nrm(ks[1], (BATCH, D), 1.0),
        "w_ada": nrm(ks[2], (L, D, N_MOD * D), 0.5 * D ** -0.5),
        "b_ada": nrm(ks[3], (L, N_MOD * D), 0.05),
        "norm_g": 1.0 + nrm(ks[4], (L, 4, D), 0.05),
        "w_in": nrm(ks[5], (L, D, IN_COLS), D ** -0.5),
        "mu_shift": jax.random.uniform(ks[6], (L, RWKV_COLS), jnp.float32),
        "w0": -0.6 + nrm(ks[7], (L, C), 0.5),
        "w2": nrm(ks[8], (L, DECAY_LORA, C), 0.5 * DECAY_LORA ** -0.5),
        "a0": nrm(ks[9], (L, C), 0.5),
        "a2": nrm(ks[10], (L, ICLR_LORA, C), 0.5 * ICLR_LORA ** -0.5),
        "g2": nrm(ks[11], (L, GATE_LORA, C), GATE_LORA ** -0.5),
        "k_k": 0.85 + nrm(ks[12], (L, C), 0.05),
        "k_a": 1.0 + nrm(ks[13], (L, C), 0.05),
        "r_k": nrm(ks[14], (L, RWKV_HEADS, HEAD_DIM), 0.1),
        "ln_x_w": 1.0 + nrm(ks[15], (L, C), 0.05),
        "ln_x_b": nrm(ks[16], (L, C), 0.02),
        "w_up_rwkv": nrm(ks[17], (L, C, D), C ** -0.5),
        "w_up_sb": nrm(ks[18], (L, SB_WIDTH, D), SB_WIDTH ** -0.5),
        "w_out": nrm(ks[19], (L, D, D), D ** -0.5),
        "w_mlp_in": nrm(ks[20], (L, D, D_FF), D ** -0.5),
        "w_mlp_out": nrm(ks[21], (L, D_FF, D), D_FF ** -0.5),
    }


def reference(x, c, w_ada, b_ada, norm_g, w_in, mu_shift, w0, w2, a0, a2, g2, k_k, k_a,
              r_k, ln_x_w, ln_x_b, w_up_rwkv, w_up_sb, w_out, w_mlp_in, w_mlp_out):
    B, S, D = x.shape
    H, N = RWKV_HEADS, HEAD_DIM
    C = RWKV_WIDTH
    rwkv_splits = (C, 2 * C, 3 * C, 3 * C + DECAY_LORA, 3 * C + DECAY_LORA + ICLR_LORA)

    def heads(t):
        return t.reshape(B, S, H, N)

    for l in range(DEPTH):
        mod = jax.nn.silu(c) @ w_ada[l] + b_ada[l]
        shift_m, scale_m, gate_m, shift_f, scale_f, gate_f = jnp.split(mod[:, None, :], N_MOD, axis=-1)

        h = _rmsnorm(x, norm_g[l, 0]) * (1.0 + scale_m) + shift_m
        proj = h @ w_in[l]
        p_rwkv, p_sb, p_gate = jnp.split(proj, (RWKV_COLS, RWKV_COLS + SB_COLS), axis=-1)

        p_rwkv = p_rwkv + (_token_shift(p_rwkv) - p_rwkv) * mu_shift[l]
        r, k, v, dw, da, dg = jnp.split(p_rwkv, rwkv_splits, axis=-1)
        w_log = -jax.nn.softplus(-(w0[l] + jnp.tanh(dw) @ w2[l])) - 0.5
        decay = jnp.exp(-jnp.exp(w_log.astype(jnp.float32)))
        a = jax.nn.sigmoid(a0[l] + da @ a2[l])
        g = jax.nn.sigmoid(dg) @ g2[l]
        kk = _l2norm_heads(heads(k * k_k[l]))
        k = k * (1.0 + (a - 1.0) * k_a[l])
        r_h, k_h, v_h = heads(r), heads(k), heads(v)
        y = _rwkv7_scan(r_h, heads(decay), k_h, v_h, kk, heads(a))
        y = _group_norm(y).reshape(B, S, C) * ln_x_w[l] + ln_x_b[l]
        bonus = jnp.sum(r_h * k_h * r_k[l], axis=-1, keepdims=True) * v_h
        y_a = ((y + bonus.reshape(B, S, C)) * g).astype(x.dtype)

        q_s, k_s, v_s = [t.reshape(B, S, SB_HEADS, HEAD_DIM).transpose(0, 2, 1, 3)
                         for t in jnp.split(p_sb, 3, axis=-1)]
        y_b = _stick_breaking(q_s, k_s, v_s).transpose(0, 2, 1, 3).reshape(B, S, SB_WIDTH)

        gate_a, gate_b = jnp.split(p_gate, 2, axis=-1)
        merged = (jax.nn.sigmoid(gate_a) * (y_a @ w_up_rwkv[l])
                  + jax.nn.sigmoid(gate_b) * (y_b @ w_up_sb[l]))
        mix = merged @ w_out[l]
        x = x + gate_m * _rmsnorm(mix, norm_g[l, 1])

        hf = _rmsnorm(x, norm_g[l, 2]) * (1.0 + scale_f) + shift_f
        f = jnp.square(jax.nn.relu(hf @ w_mlp_in[l])) @ w_mlp_out[l]
        x = x + gate_f * _rmsnorm(f, norm_g[l, 3])
    return x
```

```python
import functools
import math

import jax
import jax.numpy as jnp
from jax import lax
from jax.experimental import pallas as pl
from jax.experimental.pallas import tpu as pltpu

F32 = jnp.float32
BF16 = jnp.bfloat16

HEAD_DIM = 64
LANES = 128
DECAY_LORA = 64
ICLR_LORA = 64
GATE_LORA = 160
LORA_PAD = 512
NORM_EPS = 1e-6
GN_EPS = 64e-5
CHUNK = 128
SB_BLOCK = 128
SB_UNDERFLOW = 104.0
VMEM_LIMIT = 56 * 1024 * 1024


def _cparams(sem, vmem=VMEM_LIMIT):
    return pltpu.CompilerParams(dimension_semantics=sem, vmem_limit_bytes=vmem)


def _dot(a, b):
    return jnp.dot(a, b, preferred_element_type=F32)


def _dot_nt(a, b):
    return lax.dot_general(a, b, (((1,), (1,)), ((), ())), preferred_element_type=F32)


def _dot_tn(a, b):
    return lax.dot_general(a, b, (((0,), (0,)), ((), ())), preferred_element_type=F32)


def _split2(x):
    hi = x.astype(BF16)
    lo = (x - hi.astype(F32)).astype(BF16)
    return hi, lo


def _split3(x):
    hi = x.astype(BF16)
    r1 = x - hi.astype(F32)
    mid = r1.astype(BF16)
    lo = (r1 - mid.astype(F32)).astype(BF16)
    return hi, mid, lo


def _dot_exact01(x, m01):
    hi, mid, lo = _split3(x)
    return _dot(hi, m01) + _dot(mid, m01) + _dot(lo, m01)


def _exact01_dot(m01, x):
    hi, mid, lo = _split3(x)
    return _dot(m01, hi) + _dot(m01, mid) + _dot(m01, lo)


def _head_sum(x, bd):
    return _dot_exact01(x, bd)


def _rms(x, g):
    return x * lax.rsqrt(jnp.mean(x * x, axis=-1, keepdims=True) + NORM_EPS) * g


def _ada_kernel(c_ref, w_ref, b_ref, o_ref):
    c = c_ref[...]
    s = c * jax.nn.sigmoid(c)
    o_ref[...] = _dot(s.astype(BF16), w_ref[...].astype(BF16)) + b_ref[...]


def _ada(c, w_ada, b_ada, tn=1536):
    B, D = c.shape
    N = w_ada.shape[1]
    rows = 8
    c_pad = jnp.zeros((rows, D), F32).at[:B].set(c)
    out = pl.pallas_call(
        _ada_kernel,
        out_shape=jax.ShapeDtypeStruct((rows, N), F32),
        grid=(N // tn,),
        in_specs=[pl.BlockSpec((rows, D), lambda j: (0, 0)),
                  pl.BlockSpec((D, tn), lambda j: (0, j)),
                  pl.BlockSpec((1, tn), lambda j: (0, j))],
        out_specs=pl.BlockSpec((rows, tn), lambda j: (0, j)),
        compiler_params=_cparams(("parallel",)),
        name="ada",
    )(c_pad, w_ada, b_ada.reshape(1, N))
    return out[:B]


def _inproj_kernel(x_ref, mod_ref, g_ref, w_ref, o_ref, h_ref, *, d):
    @pl.when(pl.program_id(1) == 0)
    def _():
        shift = mod_ref[:, 0:d]
        scale = mod_ref[:, d:2 * d]
        h = _rms(x_ref[...], g_ref[...]) * (1.0 + scale) + shift
        h_ref[...] = h.astype(BF16)

    o_ref[...] = _dot(h_ref[...], w_ref[...])


def _inproj(x2, mod3, g0, w_bf, seq, tm=1024, tn=512):
    T, D = x2.shape
    N = w_bf.shape[1]
    return pl.pallas_call(
        functools.partial(_inproj_kernel, d=D),
        out_shape=jax.ShapeDtypeStruct((T, N), F32),
        grid=(T // tm, N // tn),
        in_specs=[pl.BlockSpec((tm, D), lambda i, j: (i, 0)),
                  pl.BlockSpec((None, 1, mod3.shape[2]), lambda i, j: ((i * tm) // seq, 0, 0)),
                  pl.BlockSpec((1, D), lambda i, j: (0, 0)),
                  pl.BlockSpec((D, tn), lambda i, j: (0, j))],
        out_specs=pl.BlockSpec((tm, tn), lambda i, j: (i, j)),
        scratch_shapes=[pltpu.VMEM((tm, D), BF16)],
        compiler_params=_cparams(("parallel", "arbitrary")),
        name="inproj",
    )(x2, mod3, g0, w_bf)


def _prep_kernel(r_ref, k_ref, v_ref, l_ref, rp_ref, kp_ref, vp_ref, lp_ref,
                 mur_ref, muk_ref, muv_ref, mul_ref, w0_ref, a0_ref, kk_ref, ka_ref,
                 w2_ref, a2_ref, g2_ref, bd_ref,
                 ro_ref, ko_ref, vo_ref, kko_ref, bo_ref, lwo_ref, go_ref, *, tm, seq):
    i = pl.program_id(0)
    first = (i * tm) % seq == 0
    row = lax.broadcasted_iota(jnp.int32, (tm, 1), 0)

    def mixed(cur_ref, prev_ref, mu_ref):
        cur = cur_ref[...]
        prev_row = jnp.where(first, 0.0, prev_ref[7:8, :])
        shifted = jnp.where(row == 0, prev_row, pltpu.roll(cur, 1, 0))
        return cur + (shifted - cur) * mu_ref[...]

    r = mixed(r_ref, rp_ref, mur_ref)
    k = mixed(k_ref, kp_ref, muk_ref)
    v = mixed(v_ref, vp_ref, muv_ref)
    lo = mixed(l_ref, lp_ref, mul_ref)

    dw = _dot(jnp.tanh(lo).astype(BF16), w2_ref[...])
    da = _dot(lo.astype(BF16), a2_ref[...])
    g = _dot(jax.nn.sigmoid(lo).astype(BF16), g2_ref[...])

    w_log = -jax.nn.softplus(-(w0_ref[...] + dw)) - 0.5
    lw = -jnp.exp(w_log)
    a = jax.nn.sigmoid(a0_ref[...] + da)
    kx = k * kk_ref[...]
    bd = bd_ref[...]
    n_blk = kx.shape[1] // LANES
    sq = kx * kx
    ssum = jnp.concatenate(
        [_head_sum(sq[:, c * LANES:(c + 1) * LANES], bd) for c in range(n_blk)], axis=1)
    kk = kx * lax.rsqrt(jnp.maximum(ssum, 1e-24))
    k2 = k * (1.0 + (a - 1.0) * ka_ref[...])

    ro_ref[...] = r
    ko_ref[...] = k2
    vo_ref[...] = v
    kko_ref[...] = kk
    bo_ref[...] = kk * a
    lwo_ref[...] = lw
    go_ref[...] = g


def _prep(P, seq, consts, tm=256):
    T = P.shape[0]
    C = consts["w0"].shape[1]
    cb = C // 1024
    assert cb == 1
    col_r, col_k, col_v, col_l = consts["col_r"], consts["col_k"], consts["col_v"], consts["col_l"]

    def cur(col, w):
        return pl.BlockSpec((tm, w), lambda i: (i, col))

    def prev(col, w):
        return pl.BlockSpec((8, w), lambda i: (jnp.maximum(i * (tm // 8) - 1, 0), col))

    def const(shape):
        return pl.BlockSpec(shape, lambda i: (0, 0))

    out_sd = jax.ShapeDtypeStruct((T, C), F32)
    outs = pl.pallas_call(
        functools.partial(_prep_kernel, tm=tm, seq=seq),
        out_shape=[out_sd] * 7,
        grid=(T // tm,),
        in_specs=[cur(col_r, C), cur(col_k, C), cur(col_v, C), cur(col_l, LORA_PAD),
                  prev(col_r, C), prev(col_k, C), prev(col_v, C), prev(col_l, LORA_PAD),
                  const((1, C)), const((1, C)), const((1, C)), const((1, LORA_PAD)),
                  const((1, C)), const((1, C)), const((1, C)), const((1, C)),
                  const((LORA_PAD, C)), const((LORA_PAD, C)), const((LORA_PAD, C)),
                  const((LANES, LANES))],
        out_specs=[pl.BlockSpec((tm, C), lambda i: (i, 0))] * 7,
        compiler_params=_cparams(("parallel",)),
        name="rwkv_prep",
    )(P, P, P, P, P, P, P, P,
      consts["mu_r"], consts["mu_k"], consts["mu_v"], consts["mu_l"],
      consts["w0"], consts["a0"], consts["k_k"], consts["k_a"],
      consts["w2p"], consts["a2p"], consts["g2p"], consts["bd"])
    return outs


def _scan_kernel(r_ref, k_ref, v_ref, kk_ref, b_ref, lw_ref, g_ref,
                 rk_ref, lnw_ref, lnb_ref, bd_ref, o_ref, h_ref, *, nc):
    L = CHUNK
    half = L // 2

    @pl.when(pl.program_id(2) == 0)
    def _():
        h_ref[...] = jnp.zeros_like(h_ref)

    ri = lax.broadcasted_iota(jnp.int32, (L, L), 0)
    ci = lax.broadcasted_iota(jnp.int32, (L, L), 1)
    lower_incl = ri >= ci
    lower_strict = ri > ci
    eye = ri == ci
    same_head = (ri < HEAD_DIM) == (ci < HEAD_DIM)
    tri01 = lower_incl.astype(BF16)
    eye_f = eye.astype(F32)
    lane = lax.broadcasted_iota(jnp.int32, (1, LANES), 1)
    head0 = lane < HEAD_DIM
    bd = bd_ref[...]

    def sel(x0, x1):
        return jnp.where(head0, x0, x1)

    for c in range(nc):
        rows = pl.ds(c * L, L)
        r = r_ref[rows, :]
        k = k_ref[rows, :]
        v = v_ref[rows, :]
        kk = kk_ref[rows, :]
        b = b_ref[rows, :]
        lw = lw_ref[rows, :]

        cum = _exact01_dot(tri01, lw)
        mid = cum[half - 1:half, :]
        end = cum[L - 1:L, :]
        r_mid = r * jnp.exp(cum - mid)
        kk_mid = kk * jnp.exp(cum - lw - mid)
        e_back = jnp.exp(mid - cum)
        k_mid = k * e_back
        b_mid = b * e_back
        kk_dec = kk * jnp.exp(cum - lw)
        r_dec = r * jnp.exp(cum)
        e_end = jnp.exp(end - cum)
        k_end = (k * e_end).astype(BF16)
        b_end = (b * e_end).astype(BF16)
        p_end = jnp.exp(end)

        rhs = jnp.concatenate([b_mid, k_mid], axis=0).astype(BF16)
        v_bf = v.astype(BF16)
        t_inv, a_kk, a_rk, a_rb = [], [], [], []
        for h in range(2):
            hm = head0 if h == 0 else jnp.logical_not(head0)
            kk_h = jnp.where(hm, kk_mid, 0.0).astype(BF16)
            r_h = jnp.where(hm, r_mid, 0.0).astype(BF16)
            ak = _dot_nt(kk_h, rhs)
            ar = _dot_nt(r_h, rhs)
            a_kb = jnp.where(lower_strict, ak[:, :L], 0.0)
            a_kk.append(jnp.where(lower_strict, ak[:, L:], 0.0).astype(BF16))
            a_rb.append(jnp.where(lower_incl, ar[:, :L], 0.0).astype(BF16))
            a_rk.append(jnp.where(lower_incl, ar[:, L:], 0.0).astype(BF16))
            t = eye_f - a_kb
            p = a_kb
            for _ in range(int(math.log2(L)) - 1):
                p_bf = p.astype(BF16)
                p = _dot(p_bf, p_bf)
                t = t + _dot(t.astype(BF16), p.astype(BF16))
            t_inv.append(t.astype(BF16))

        av = sel(_dot(a_kk[0], v_bf), _dot(a_kk[1], v_bf))
        arkv = sel(_dot(a_rk[0], v_bf), _dot(a_rk[1], v_bf))
        kd_bf = kk_dec.astype(BF16)
        av_bf = av.astype(BF16)
        w1 = sel(_dot(t_inv[0], kd_bf), _dot(t_inv[1], kd_bf))
        w2 = sel(_dot(t_inv[0], av_bf), _dot(t_inv[1], av_bf))
        w1_bf = w1.astype(BF16)
        w2_bf = w2.astype(BF16)
        rq = r_dec - sel(_dot(a_rb[0], w1_bf), _dot(a_rb[1], w1_bf))
        y0 = arkv - sel(_dot(a_rb[0], w2_bf), _dot(a_rb[1], w2_bf))
        gmat = eye_f * p_end - jnp.where(same_head, _dot_tn(b_end, w1_bf), 0.0)
        cmat = jnp.where(same_head, _dot_tn(k_end, v_bf) - _dot_tn(b_end, w2_bf), 0.0)

        h_bf = h_ref[...].astype(BF16)
        y = _dot(rq.astype(BF16), h_bf) + y0
        h_ref[...] = _dot(gmat.astype(BF16), h_bf) + cmat

        inv_n = 1.0 / HEAD_DIM
        mean = _head_sum(y, bd) * inv_n
        yc = y - mean
        var = _head_sum(yc * yc, bd) * inv_n
        yn = yc * lax.rsqrt(var + GN_EPS) * lnw_ref[...] + lnb_ref[...]
        bonus = _head_sum(r * k * rk_ref[...], bd) * v
        o_ref[rows, :] = (yn + bonus) * g_ref[rows, :]


def _scan(arrs, rk, lnw, lnb, bd, batch, seq, nc=2):
    r, k, v, kk, b, lw, g = [a.reshape(batch, seq, a.shape[1]) for a in arrs]
    C = r.shape[2]
    blk = pl.BlockSpec((None, CHUNK * nc, LANES), lambda bi, p, c: (bi, c, p))
    par = pl.BlockSpec((1, LANES), lambda bi, p, c: (0, p))
    out = pl.pallas_call(
        functools.partial(_scan_kernel, nc=nc),
        out_shape=jax.ShapeDtypeStruct((batch, seq, C), F32),
        grid=(batch, C // LANES, seq // (CHUNK * nc)),
        in_specs=[blk] * 7 + [par, par, par,
                              pl.BlockSpec((LANES, LANES), lambda bi, p, c: (0, 0))],
        out_specs=blk,
        scratch_shapes=[pltpu.VMEM((LANES, LANES), F32)],
        compiler_params=_cparams(("parallel", "parallel", "arbitrary")),
        name="rwkv_scan",
    )(r, k, v, kk, b, lw, g, rk, lnw, lnb, bd)
    return out.reshape(batch * seq, C)


def _sb_kernel(q_ref, k_ref, v_ref, o_ref):
    tq = tk = SB_BLOCK
    i = pl.program_id(2)
    lane = lax.broadcasted_iota(jnp.int32, (1, LANES), 1)
    head0 = lane < HEAD_DIM
    q = q_ref[...] * (1.0 / math.sqrt(HEAD_DIM))
    qh = [jnp.where(head0, q, 0.0).astype(BF16), jnp.where(head0, 0.0, q).astype(BF16)]
    ri = lax.broadcasted_iota(jnp.int32, (tq, tk), 0)
    ci = lax.broadcasted_iota(jnp.int32, (tq, tk), 1)
    after01 = (ri > ci).astype(BF16)

    def body(carry):
        j, _, c0, c1, acc = carry
        start = pl.multiple_of(j * tk, tk)
        kb = k_ref[pl.ds(start, tk), :].astype(BF16)
        vb = v_ref[pl.ds(start, tk), :].astype(BF16)
        causal = (j * tk + ci) < (i * tq + ri)
        new_c = []
        for h, c_h in enumerate((c0, c1)):
            z = _dot_nt(qh[h], kb)
            sp = jnp.maximum(z, 0.0) + jnp.log1p(jnp.exp(-jnp.abs(z)))
            log_keep = jnp.where(causal, -sp, 0.0)
            hi, lo = _split2(log_keep)
            later = c_h + _dot(hi, after01) + _dot(lo, after01)
            att = jnp.where(causal, jnp.exp(z - sp + later), 0.0)
            pv = _dot(att.astype(BF16), vb)
            acc = acc + jnp.where(head0 if h == 0 else jnp.logical_not(head0), pv, 0.0)
            new_c.append(later[:, 0:1] + log_keep[:, 0:1])
        alive = (jnp.max(jnp.maximum(new_c[0], new_c[1])) > -SB_UNDERFLOW).astype(jnp.int32)
        return j - 1, alive, new_c[0], new_c[1], acc

    def cond(carry):
        j, alive = carry[0], carry[1]
        return jnp.logical_and(j >= 0, alive > 0)

    zero_c = jnp.zeros((tq, 1), F32)
    init = (i, jnp.int32(1), zero_c, zero_c, jnp.zeros((tq, LANES), F32))
    o_ref[...] = lax.while_loop(cond, body, init)[4]


def _stick_breaking(P3, col_q, col_k, col_v, width):
    B, S, _ = P3.shape
    n_pair = width // LANES
    out = pl.pallas_call(
        _sb_kernel,
        out_shape=jax.ShapeDtypeStruct((B, S, width), F32),
        grid=(B, n_pair, S // SB_BLOCK),
        in_specs=[pl.BlockSpec((None, SB_BLOCK, LANES), lambda b, p, i: (b, i, col_q + p)),
                  pl.BlockSpec((None, S, LANES), lambda b, p, i: (b, 0, col_k + p)),
                  pl.BlockSpec((None, S, LANES), lambda b, p, i: (b, 0, col_v + p))],
        out_specs=pl.BlockSpec((None, SB_BLOCK, LANES), lambda b, p, i: (b, i, p)),
        compiler_params=_cparams(("parallel", "parallel", "arbitrary")),
        name="stick_breaking",
    )(P3, P3, P3)
    return out.reshape(B * S, width)


def _merge_kernel(ya_ref, yb_ref, ga_ref, gb_ref, x_ref, mod_ref, g1_ref,
                  wua_ref, wub_ref, wo_ref, o_ref, *, d):
    ua = _dot(ya_ref[...].astype(BF16), wua_ref[...])
    ub = _dot(yb_ref[...].astype(BF16), wub_ref[...])
    merged = jax.nn.sigmoid(ga_ref[...]) * ua + jax.nn.sigmoid(gb_ref[...]) * ub
    mix = _dot(merged.astype(BF16), wo_ref[...])
    gate_m = mod_ref[:, 2 * d:3 * d]
    o_ref[...] = x_ref[...] + gate_m * _rms(mix, g1_ref[...])


def _merge(ya, yb, P, x2, mod3, g1, wua, wub, wo, seq, tm=256):
    T, D = x2.shape
    C = ya.shape[1]

    def const(shape):
        return pl.BlockSpec(shape, lambda i: (0, 0), pipeline_mode=pl.Buffered(1))

    return pl.pallas_call(
        functools.partial(_merge_kernel, d=D),
        out_shape=jax.ShapeDtypeStruct((T, D), F32),
        grid=(T // tm,),
        in_specs=[pl.BlockSpec((tm, C), lambda i: (i, 0)),
                  pl.BlockSpec((tm, C), lambda i: (i, 0)),
                  pl.BlockSpec((tm, D), lambda i: (i, 0)),
                  pl.BlockSpec((tm, D), lambda i: (i, 1)),
                  pl.BlockSpec((tm, D), lambda i: (i, 0)),
                  pl.BlockSpec((None, 1, mod3.shape[2]), lambda i: ((i * tm) // seq, 0, 0)),
                  pl.BlockSpec((1, D), lambda i: (0, 0)),
                  const((C, D)), const((C, D)), const((D, D))],
        out_specs=pl.BlockSpec((tm, D), lambda i: (i, 0)),
        compiler_params=_cparams(("parallel",)),
        name="merge_out",
    )(ya, yb, P, P, x2, mod3, g1, wua, wub, wo)


def _mlp_kernel(x_ref, mod_ref, g2_ref, g3_ref, w1_ref, w2_ref, o_ref, h_ref, acc_ref, *, d):
    j = pl.program_id(1)

    @pl.when(j == 0)
    def _():
        shift = mod_ref[:, 3 * d:4 * d]
        scale = mod_ref[:, 4 * d:5 * d]
        h = _rms(x_ref[...], g2_ref[...]) * (1.0 + scale) + shift
        h_ref[...] = h.astype(BF16)
        acc_ref[...] = jnp.zeros_like(acc_ref)

    u = jnp.maximum(_dot(h_ref[...], w1_ref[...]), 0.0)
    acc_ref[...] += _dot((u * u).astype(BF16), w2_ref[...])

    @pl.when(j == pl.num_programs(1) - 1)
    def _():
        gate_f = mod_ref[:, 5 * d:6 * d]
        o_ref[...] = x_ref[...] + gate_f * _rms(acc_ref[...], g3_ref[...])


def _mlp(x2, mod3, g2, g3, w1, w2, seq, tm=512, tf=512):
    T, D = x2.shape
    F = w1.shape[1]
    return pl.pallas_call(
        functools.partial(_mlp_kernel, d=D),
        out_shape=jax.ShapeDtypeStruct((T, D), F32),
        grid=(T // tm, F // tf),
        in_specs=[pl.BlockSpec((tm, D), lambda i, j: (i, 0)),
                  pl.BlockSpec((None, 1, mod3.shape[2]), lambda i, j: ((i * tm) // seq, 0, 0)),
                  pl.BlockSpec((1, D), lambda i, j: (0, 0)),
                  pl.BlockSpec((1, D), lambda i, j: (0, 0)),
                  pl.BlockSpec((D, tf), lambda i, j: (0, j)),
                  pl.BlockSpec((tf, D), lambda i, j: (j, 0))],
        out_specs=pl.BlockSpec((tm, D), lambda i, j: (i, 0)),
        scratch_shapes=[pltpu.VMEM((tm, D), BF16), pltpu.VMEM((tm, D), F32)],
        compiler_params=_cparams(("parallel", "arbitrary")),
        name="mlp",
    )(x2, mod3, g2, g3, w1, w2)


def _pad_rows(w, offset, total):
    return jnp.zeros((total, w.shape[1]), w.dtype).at[offset:offset + w.shape[0]].set(w)


def kernel(x, c, w_ada, b_ada, norm_g, w_in, mu_shift, w0, w2, a0, a2, g2, k_k, k_a,
           r_k, ln_x_w, ln_x_b, w_up_rwkv, w_up_sb, w_out, w_mlp_in, w_mlp_out):
    B, S, D = x.shape
    depth = w_in.shape[0]
    C = w0.shape[1]
    W = w_up_sb.shape[1]
    rwkv_cols = 3 * C + DECAY_LORA + ICLR_LORA + GATE_LORA
    sb_cols = 3 * W
    n_lora = DECAY_LORA + ICLR_LORA + GATE_LORA
    assert C == 1024 and W == 1024 and D == 2 * C and S % (2 * CHUNK) == 0

    hi = lax.broadcasted_iota(jnp.int32, (LANES, LANES), 0) // HEAD_DIM
    hj = lax.broadcasted_iota(jnp.int32, (LANES, LANES), 1) // HEAD_DIM
    bd = (hi == hj).astype(BF16)

    x2 = x.reshape(B * S, D)
    for l in range(depth):
        mod = _ada(c, w_ada[l], b_ada[l])
        mod3 = mod.reshape(B, 1, mod.shape[1])

        wl = w_in[l]
        g_off = rwkv_cols + sb_cols
        lora_w = jnp.pad(wl[:, 3 * C:rwkv_cols], ((0, 0), (0, LORA_PAD - n_lora)))
        w_perm = jnp.concatenate(
            [wl[:, g_off:], wl[:, :3 * C], wl[:, rwkv_cols:g_off], lora_w], axis=1).astype(BF16)
        P = _inproj(x2, mod3, norm_g[l, 0].reshape(1, D), w_perm, S)

        mu = mu_shift[l]
        consts = dict(
            col_r=2 * D // C, col_k=2 * D // C + 1, col_v=2 * D // C + 2,
            col_l=(2 * D + 3 * C + 3 * W) // LORA_PAD,
            mu_r=mu[None, :C], mu_k=mu[None, C:2 * C], mu_v=mu[None, 2 * C:3 * C],
            mu_l=jnp.pad(mu[None, 3 * C:], ((0, 0), (0, LORA_PAD - n_lora))),
            w0=w0[l][None], a0=a0[l][None], k_k=k_k[l][None], k_a=k_a[l][None],
            w2p=_pad_rows(w2[l], 0, LORA_PAD).astype(BF16),
            a2p=_pad_rows(a2[l], DECAY_LORA, LORA_PAD).astype(BF16),
            g2p=_pad_rows(g2[l], DECAY_LORA + ICLR_LORA, LORA_PAD).astype(BF16),
            bd=bd)
        prep = _prep(P, S, consts)
        ya = _scan(prep, r_k[l].reshape(1, C), ln_x_w[l][None], ln_x_b[l][None], bd, B, S)

        col_sb = (2 * D + 3 * C) // LANES
        yb = _stick_breaking(P.reshape(B, S, P.shape[1]), col_sb, col_sb + W // LANES,
                             col_sb + 2 * W // LANES, W)

        x2 = _merge(ya, yb, P, x2, mod3, norm_g[l, 1].reshape(1, D),
                    w_up_rwkv[l].astype(BF16), w_up_sb[l].astype(BF16), w_out[l].astype(BF16), S)
        x2 = _mlp(x2, mod3, norm_g[l, 2].reshape(1, D), norm_g[l, 3].reshape(1, D),
                  w_mlp_in[l].astype(BF16), w_mlp_out[l].astype(BF16), S)
    return x2.reshape(B, S, D)
```

```python
import functools
import math

import jax
import jax.numpy as jnp
from jax import lax
from jax.experimental import pallas as pl
from jax.experimental.pallas import tpu as pltpu

F32 = jnp.float32
BF16 = jnp.bfloat16

HEAD_DIM = 64
LANES = 128
DECAY_LORA = 64
ICLR_LORA = 64
GATE_LORA = 160
LORA_PAD = 512
NORM_EPS = 1e-6
GN_EPS = 64e-5
CHUNK = 128
SB_BLOCK = 128
SB_BLOCKS_PER_STEP = 3
SB_UNDERFLOW = 104.0
VMEM_LIMIT = 56 * 1024 * 1024


def _cparams(sem, vmem=VMEM_LIMIT):
    return pltpu.CompilerParams(dimension_semantics=sem, vmem_limit_bytes=vmem)


def _dot(a, b):
    return jnp.dot(a, b, preferred_element_type=F32)


def _dot_nt(a, b):
    return lax.dot_general(a, b, (((1,), (1,)), ((), ())), preferred_element_type=F32)


def _dot_tn(a, b):
    return lax.dot_general(a, b, (((0,), (0,)), ((), ())), preferred_element_type=F32)


def _split2(x):
    hi = x.astype(BF16)
    lo = (x - hi.astype(F32)).astype(BF16)
    return hi, lo


def _split3(x):
    hi = x.astype(BF16)
    r1 = x - hi.astype(F32)
    mid = r1.astype(BF16)
    lo = (r1 - mid.astype(F32)).astype(BF16)
    return hi, mid, lo


def _dot_exact01(x, m01):
    n = x.shape[0]
    res = _dot(jnp.concatenate(_split3(x), axis=0), m01)
    return res[:n] + res[n:2 * n] + res[2 * n:]


def _exact01_dot(m01, x):
    n = x.shape[1]
    res = _dot(m01, jnp.concatenate(_split3(x), axis=1))
    return res[:, :n] + res[:, n:2 * n] + res[:, 2 * n:]


def _head_sum(x, bd):
    return _dot_exact01(x, bd)


def _rms(x, g):
    return x * lax.rsqrt(jnp.mean(x * x, axis=-1, keepdims=True) + NORM_EPS) * g


def _ada_kernel(c_ref, w_ref, b_ref, o_ref):
    c = c_ref[...]
    s = c * jax.nn.sigmoid(c)
    o_ref[...] = _dot(s.astype(BF16), w_ref[...].astype(BF16)) + b_ref[...]


def _ada(c, w_ada, b_ada, tn=1536):
    B, D = c.shape
    N = w_ada.shape[1]
    rows = 8
    c_pad = jnp.zeros((rows, D), F32).at[:B].set(c)
    out = pl.pallas_call(
        _ada_kernel,
        out_shape=jax.ShapeDtypeStruct((rows, N), F32),
        grid=(N // tn,),
        in_specs=[pl.BlockSpec((rows, D), lambda j: (0, 0)),
                  pl.BlockSpec((D, tn), lambda j: (0, j)),
                  pl.BlockSpec((1, tn), lambda j: (0, j))],
        out_specs=pl.BlockSpec((rows, tn), lambda j: (0, j)),
        compiler_params=_cparams(("parallel",)),
        name="ada",
    )(c_pad, w_ada, b_ada.reshape(1, N))
    return out[:B]


def _inproj_kernel(x_ref, mod_ref, g_ref, w_ref, o_ref, h_ref, *, d):
    @pl.when(pl.program_id(1) == 0)
    def _():
        shift = mod_ref[:, 0:d]
        scale = mod_ref[:, d:2 * d]
        h = _rms(x_ref[...], g_ref[...]) * (1.0 + scale) + shift
        h_ref[...] = h.astype(BF16)

    o_ref[...] = _dot(h_ref[...], w_ref[...])


def _inproj(x2, mod3, g0, w_bf, seq, tm=1024, tn=512):
    T, D = x2.shape
    N = w_bf.shape[1]
    return pl.pallas_call(
        functools.partial(_inproj_kernel, d=D),
        out_shape=jax.ShapeDtypeStruct((T, N), F32),
        grid=(T // tm, N // tn),
        in_specs=[pl.BlockSpec((tm, D), lambda i, j: (i, 0)),
                  pl.BlockSpec((None, 1, mod3.shape[2]), lambda i, j: ((i * tm) // seq, 0, 0)),
                  pl.BlockSpec((1, D), lambda i, j: (0, 0)),
                  pl.BlockSpec((D, tn), lambda i, j: (0, j))],
        out_specs=pl.BlockSpec((tm, tn), lambda i, j: (i, j)),
        scratch_shapes=[pltpu.VMEM((tm, D), BF16)],
        compiler_params=_cparams(("parallel", "arbitrary")),
        name="inproj",
    )(x2, mod3, g0, w_bf)


def _prep_kernel(r_ref, k_ref, v_ref, l_ref, rp_ref, kp_ref, vp_ref, lp_ref,
                 mur_ref, muk_ref, muv_ref, mul_ref, w0_ref, a0_ref, kk_ref, ka_ref,
                 w2_ref, a2_ref, g2_ref, bd_ref,
                 ro_ref, ko_ref, vo_ref, kko_ref, bo_ref, lwo_ref, go_ref, *, tm, seq):
    i = pl.program_id(0)
    first = (i * tm) % seq == 0
    row = lax.broadcasted_iota(jnp.int32, (tm, 1), 0)

    def mixed(cur_ref, prev_ref, mu_ref):
        cur = cur_ref[...]
        prev_row = jnp.where(first, 0.0, prev_ref[7:8, :])
        shifted = jnp.where(row == 0, prev_row, pltpu.roll(cur, 1, 0))
        return cur + (shifted - cur) * mu_ref[...]

    r = mixed(r_ref, rp_ref, mur_ref)
    k = mixed(k_ref, kp_ref, muk_ref)
    v = mixed(v_ref, vp_ref, muv_ref)
    lo = mixed(l_ref, lp_ref, mul_ref)

    dw = _dot(jnp.tanh(lo).astype(BF16), w2_ref[...])
    da = _dot(lo.astype(BF16), a2_ref[...])
    g = _dot(jax.nn.sigmoid(lo).astype(BF16), g2_ref[...])

    w_log = -jax.nn.softplus(-(w0_ref[...] + dw)) - 0.5
    lw = -jnp.exp(w_log)
    a = jax.nn.sigmoid(a0_ref[...] + da)
    kx = k * kk_ref[...]
    bd = bd_ref[...]
    n_blk = kx.shape[1] // LANES
    sq = kx * kx
    ssum = jnp.concatenate(
        [_head_sum(sq[:, c * LANES:(c + 1) * LANES], bd) for c in range(n_blk)], axis=1)
    kk = kx * lax.rsqrt(jnp.maximum(ssum, 1e-24))
    k2 = k * (1.0 + (a - 1.0) * ka_ref[...])

    ro_ref[...] = r
    ko_ref[...] = k2
    vo_ref[...] = v
    kko_ref[...] = kk
    bo_ref[...] = kk * a
    lwo_ref[...] = lw
    go_ref[...] = g


def _prep(P, seq, consts, tm=256):
    T = P.shape[0]
    C = consts["w0"].shape[1]
    cb = C // 1024
    assert cb == 1
    col_r, col_k, col_v, col_l = consts["col_r"], consts["col_k"], consts["col_v"], consts["col_l"]

    def cur(col, w):
        return pl.BlockSpec((tm, w), lambda i: (i, col))

    def prev(col, w):
        return pl.BlockSpec((8, w), lambda i: (jnp.maximum(i * (tm // 8) - 1, 0), col))

    def const(shape):
        return pl.BlockSpec(shape, lambda i: (0, 0))

    out_sd = jax.ShapeDtypeStruct((T, C), F32)
    outs = pl.pallas_call(
        functools.partial(_prep_kernel, tm=tm, seq=seq),
        out_shape=[out_sd] * 7,
        grid=(T // tm,),
        in_specs=[cur(col_r, C), cur(col_k, C), cur(col_v, C), cur(col_l, LORA_PAD),
                  prev(col_r, C), prev(col_k, C), prev(col_v, C), prev(col_l, LORA_PAD),
                  const((1, C)), const((1, C)), const((1, C)), const((1, LORA_PAD)),
                  const((1, C)), const((1, C)), const((1, C)), const((1, C)),
                  const((LORA_PAD, C)), const((LORA_PAD, C)), const((LORA_PAD, C)),
                  const((LANES, LANES))],
        out_specs=[pl.BlockSpec((tm, C), lambda i: (i, 0))] * 7,
        compiler_params=_cparams(("parallel",)),
        name="rwkv_prep",
    )(P, P, P, P, P, P, P, P,
      consts["mu_r"], consts["mu_k"], consts["mu_v"], consts["mu_l"],
      consts["w0"], consts["a0"], consts["k_k"], consts["k_a"],
      consts["w2p"], consts["a2p"], consts["g2p"], consts["bd"])
    return outs


def _scan_kernel(r_ref, k_ref, v_ref, kk_ref, b_ref, lw_ref, g_ref,
                 rk_ref, lnw_ref, lnb_ref, bd_ref, o_ref, h_ref, *, nc, npair):
    L = CHUNK
    half = L // 2

    @pl.when(pl.program_id(2) == 0)
    def _():
        h_ref[...] = jnp.zeros_like(h_ref)

    ri = lax.broadcasted_iota(jnp.int32, (L, L), 0)
    ci = lax.broadcasted_iota(jnp.int32, (L, L), 1)
    lower_incl = ri >= ci
    lower_strict = ri > ci
    eye = ri == ci
    same_head = (ri < HEAD_DIM) == (ci < HEAD_DIM)
    tri01 = lower_incl.astype(BF16)
    eye_f = eye.astype(F32)
    lane = lax.broadcasted_iota(jnp.int32, (1, LANES), 1)
    head0 = lane < HEAD_DIM
    bd = bd_ref[...]

    lane2 = lax.broadcasted_iota(jnp.int32, (1, 2 * LANES), 1)
    head0_2 = (lane2 % LANES) < HEAD_DIM
    tri_stack = jnp.concatenate([lower_strict, lower_incl], axis=0)
    chunks = range(nc * npair)
    pairs = [(c, h) for c in chunks for h in range(2)]

    def sel(x0, x1):
        return jnp.where(head0 if x0.shape[1] == LANES else head0_2, x0, x1)

    def lanes_of(u):
        return slice((u // nc) * LANES, (u // nc + 1) * LANES)

    def ld(ref, u):
        return ref[pl.ds((u % nc) * L, L), lanes_of(u)]

    r = [ld(r_ref, c) for c in chunks]
    k = [ld(k_ref, c) for c in chunks]
    v = [ld(v_ref, c) for c in chunks]
    kk = [ld(kk_ref, c) for c in chunks]
    b = [ld(b_ref, c) for c in chunks]
    lw = [ld(lw_ref, c) for c in chunks]

    cum = [_exact01_dot(tri01, lw[c]) for c in chunks]
    lhs_mid, rhs_mid, kd_bf, r_dec, k_end, b_end, p_end, v_bf = [], [], [], [], [], [], [], []
    for c in chunks:
        mid = cum[c][half - 1:half, :]
        end = cum[c][L - 1:L, :]
        r_mid = r[c] * jnp.exp(cum[c] - mid)
        kk_mid = kk[c] * jnp.exp(cum[c] - lw[c] - mid)
        e_back = jnp.exp(mid - cum[c])
        lhs_mid.append(jnp.concatenate([kk_mid, r_mid], axis=0))
        rhs_mid.append(jnp.concatenate([b[c] * e_back, k[c] * e_back], axis=0).astype(BF16))
        kd_bf.append((kk[c] * jnp.exp(cum[c] - lw[c])).astype(BF16))
        r_dec.append(r[c] * jnp.exp(cum[c]))
        e_end = jnp.exp(end - cum[c])
        k_end.append((k[c] * e_end).astype(BF16))
        b_end.append((b[c] * e_end).astype(BF16))
        p_end.append(jnp.exp(end))
        v_bf.append(v[c].astype(BF16))

    aa = {}
    for (c, h) in pairs:
        hm = head0 if h == 0 else jnp.logical_not(head0)
        aa[c, h] = _dot_nt(jnp.where(hm, lhs_mid[c], 0.0).astype(BF16), rhs_mid[c])
    a_kb = {key: jnp.where(lower_strict, aa[key][:L, :L], 0.0) for key in pairs}
    a_rb = {key: jnp.where(lower_incl, aa[key][L:, :L], 0.0).astype(BF16) for key in pairs}
    a_xk = {key: jnp.where(tri_stack, aa[key][:, L:], 0.0).astype(BF16) for key in pairs}
    avr = [_dot(jnp.concatenate([a_xk[c, 0], a_xk[c, 1]], axis=0), v_bf[c]) for c in chunks]

    n_lvl = int(math.log2(L)) - 1
    p = {}
    for key in pairs:
        a_bf = a_kb[key].astype(BF16)
        p[key] = _dot(a_bf, a_bf)
    t = {key: eye_f - a_kb[key] for key in pairs}
    for lvl in range(n_lvl):
        for key in pairs:
            p_bf = p[key].astype(BF16)
            if lvl + 1 < n_lvl:
                tp = _dot(jnp.concatenate([t[key], p[key]], axis=0).astype(BF16), p_bf)
                t[key] = t[key] + tp[:L]
                p[key] = tp[L:]
            else:
                t[key] = t[key] + _dot(t[key].astype(BF16), p_bf)

    rq, y0, gmat, cmat = [], [], [], []
    for c in chunks:
        av = sel(avr[c][:L], avr[c][2 * L:3 * L])
        arkv = sel(avr[c][L:2 * L], avr[c][3 * L:])
        x = jnp.concatenate([kd_bf[c], av.astype(BF16)], axis=1)
        tx = _dot(jnp.concatenate([t[c, 0], t[c, 1]], axis=0).astype(BF16), x)
        w_bf = sel(tx[:L], tx[L:]).astype(BF16)
        ax = _dot(jnp.concatenate([a_rb[c, 0], a_rb[c, 1]], axis=0), w_bf)
        aw = sel(ax[:L], ax[L:])
        bw = _dot_tn(b_end[c], w_bf)
        kv = _dot_tn(k_end[c], v_bf[c])
        rq.append(r_dec[c] - aw[:, :LANES])
        y0.append(arkv - aw[:, LANES:])
        gmat.append(eye_f * p_end[c] - jnp.where(same_head, bw[:, :LANES], 0.0))
        cmat.append(jnp.where(same_head, kv - bw[:, LANES:], 0.0))

    bonus = [_head_sum(r[u] * k[u] * rk_ref[:, lanes_of(u)], bd) * v[u] for u in chunks]

    inv_n = 1.0 / HEAD_DIM
    hstate = [h_ref[q] for q in range(npair)]
    ys = [None] * len(chunks)
    for c in range(nc):
        for q in range(npair):
            u = q * nc + c
            res = _dot(jnp.concatenate([rq[u], gmat[u]], axis=0).astype(BF16),
                       hstate[q].astype(BF16))
            ys[u] = res[:L] + y0[u]
            hstate[q] = res[L:] + cmat[u]
    for q in range(npair):
        h_ref[q] = hstate[q]
    y = jnp.concatenate(ys, axis=0)
    mean = _head_sum(y, bd) * inv_n
    yc = y - mean
    var = _head_sum(yc * yc, bd) * inv_n
    yn = yc * lax.rsqrt(var + GN_EPS)
    for u in chunks:
        rows = pl.ds((u % nc) * L, L)
        o_ref[rows, lanes_of(u)] = (
            (yn[u * L:(u + 1) * L] * lnw_ref[:, lanes_of(u)] + lnb_ref[:, lanes_of(u)] + bonus[u])
            * g_ref[rows, lanes_of(u)])


def _scan(arrs, rk, lnw, lnb, bd, batch, seq, nc=4, npair=2):
    r, k, v, kk, b, lw, g = [a.reshape(batch, seq, a.shape[1]) for a in arrs]
    C = r.shape[2]
    width = LANES * npair
    blk = pl.BlockSpec((None, CHUNK * nc, width), lambda bi, p, c: (bi, c, p))
    par = pl.BlockSpec((1, width), lambda bi, p, c: (0, p))
    out = pl.pallas_call(
        functools.partial(_scan_kernel, nc=nc, npair=npair),
        out_shape=jax.ShapeDtypeStruct((batch, seq, C), F32),
        grid=(batch, C // width, seq // (CHUNK * nc)),
        in_specs=[blk] * 7 + [par, par, par,
                              pl.BlockSpec((LANES, LANES), lambda bi, p, c: (0, 0))],
        out_specs=blk,
        scratch_shapes=[pltpu.VMEM((npair, LANES, LANES), F32)],
        compiler_params=_cparams(("parallel", "parallel", "arbitrary")),
        name="rwkv_scan",
    )(r, k, v, kk, b, lw, g, rk, lnw, lnb, bd)
    return out.reshape(batch * seq, C)


def _sb_kernel(q_ref, k_ref, v_ref, o_ref):
    tq = tk = SB_BLOCK
    nu = SB_BLOCKS_PER_STEP
    i = pl.program_id(2)
    lane = lax.broadcasted_iota(jnp.int32, (1, LANES), 1)
    head0 = lane < HEAD_DIM
    q = q_ref[...] * (1.0 / math.sqrt(HEAD_DIM))
    q2 = jnp.concatenate([jnp.where(head0, q, 0.0), jnp.where(head0, 0.0, q)], axis=0).astype(BF16)
    ri = lax.broadcasted_iota(jnp.int32, (tk, tk), 0)
    ci = lax.broadcasted_iota(jnp.int32, (tk, tk), 1)
    after01 = (ri > ci).astype(BF16)
    r2 = lax.broadcasted_iota(jnp.int32, (2 * tq, tk), 0) % tq
    c2 = lax.broadcasted_iota(jnp.int32, (2 * tq, tk), 1)
    key_minus_query = c2 - r2

    def body(carry):
        jt, _, carry2, acc2 = carry
        z, sp, log_keep, mask, vb = [], [], [], [], []
        for u in range(nu):
            jb = jt - u
            start = pl.multiple_of(jnp.maximum(jb, 0) * tk, tk)
            kb = k_ref[pl.ds(start, tk), :].astype(BF16)
            vb.append(v_ref[pl.ds(start, tk), :].astype(BF16))
            zu = _dot_nt(q2, kb)
            spu = jnp.maximum(zu, 0.0) + jnp.log1p(jnp.exp(-jnp.abs(zu)))
            mu = key_minus_query < jnp.where(jb >= 0, i * tq - jb * tk, -tq)
            z.append(zu)
            sp.append(spu)
            mask.append(mu)
            log_keep.append(jnp.where(mu, -spu, 0.0))
        n = 2 * tq
        parts = []
        for u in range(nu):
            parts.extend(_split2(log_keep[u]))
        loc = _dot(jnp.concatenate(parts, axis=0), after01)
        offset = carry2
        for u in range(nu):
            local = loc[2 * u * n:(2 * u + 1) * n] + loc[(2 * u + 1) * n:(2 * u + 2) * n]
            later = local + offset
            att = jnp.where(mask[u], jnp.exp(z[u] - sp[u] + later), 0.0)
            acc2 = acc2 + _dot(att.astype(BF16), vb[u])
            offset = offset + (local[:, 0:1] + log_keep[u][:, 0:1])
        alive = (jnp.max(offset) > -SB_UNDERFLOW).astype(jnp.int32)
        return jt - nu, alive, offset, acc2

    def cond(carry):
        jt, alive = carry[0], carry[1]
        return jnp.logical_and(jt >= 0, alive > 0)

    init = (i, jnp.int32(1), jnp.zeros((2 * tq, 1), F32), jnp.zeros((2 * tq, LANES), F32))
    acc2 = lax.while_loop(cond, body, init)[3]
    o_ref[...] = jnp.where(head0, acc2[:tq], acc2[tq:])


def _stick_breaking(P3, col_q, col_k, col_v, width):
    B, S, _ = P3.shape
    n_pair = width // LANES
    out = pl.pallas_call(
        _sb_kernel,
        out_shape=jax.ShapeDtypeStruct((B, S, width), F32),
        grid=(B, n_pair, S // SB_BLOCK),
        in_specs=[pl.BlockSpec((None, SB_BLOCK, LANES), lambda b, p, i: (b, i, col_q + p)),
                  pl.BlockSpec((None, S, LANES), lambda b, p, i: (b, 0, col_k + p)),
                  pl.BlockSpec((None, S, LANES), lambda b, p, i: (b, 0, col_v + p))],
        out_specs=pl.BlockSpec((None, SB_BLOCK, LANES), lambda b, p, i: (b, i, p)),
        compiler_params=_cparams(("parallel", "parallel", "arbitrary")),
        name="stick_breaking",
    )(P3, P3, P3)
    return out.reshape(B * S, width)


def _merge_kernel(ya_ref, yb_ref, ga_ref, gb_ref, x_ref, mod_ref, g1_ref,
                  wua_ref, wub_ref, wo_ref, o_ref, *, d):
    ua = _dot(ya_ref[...].astype(BF16), wua_ref[...])
    ub = _dot(yb_ref[...].astype(BF16), wub_ref[...])
    merged = jax.nn.sigmoid(ga_ref[...]) * ua + jax.nn.sigmoid(gb_ref[...]) * ub
    mix = _dot(merged.astype(BF16), wo_ref[...])
    gate_m = mod_ref[:, 2 * d:3 * d]
    o_ref[...] = x_ref[...] + gate_m * _rms(mix, g1_ref[...])


def _merge(ya, yb, P, x2, mod3, g1, wua, wub, wo, seq, tm=256):
    T, D = x2.shape
    C = ya.shape[1]

    def const(shape):
        return pl.BlockSpec(shape, lambda i: (0, 0), pipeline_mode=pl.Buffered(1))

    return pl.pallas_call(
        functools.partial(_merge_kernel, d=D),
        out_shape=jax.ShapeDtypeStruct((T, D), F32),
        grid=(T // tm,),
        in_specs=[pl.BlockSpec((tm, C), lambda i: (i, 0)),
                  pl.BlockSpec((tm, C), lambda i: (i, 0)),
                  pl.BlockSpec((tm, D), lambda i: (i, 0)),
                  pl.BlockSpec((tm, D), lambda i: (i, 1)),
                  pl.BlockSpec((tm, D), lambda i: (i, 0)),
                  pl.BlockSpec((None, 1, mod3.shape[2]), lambda i: ((i * tm) // seq, 0, 0)),
                  pl.BlockSpec((1, D), lambda i: (0, 0)),
                  const((C, D)), const((C, D)), const((D, D))],
        out_specs=pl.BlockSpec((tm, D), lambda i: (i, 0)),
        compiler_params=_cparams(("parallel",)),
        name="merge_out",
    )(ya, yb, P, P, x2, mod3, g1, wua, wub, wo)


def _mlp_kernel(x_ref, mod_ref, g2_ref, g3_ref, w1_ref, w2_ref, o_ref, h_ref, acc_ref, *, d):
    j = pl.program_id(1)

    @pl.when(j == 0)
    def _():
        shift = mod_ref[:, 3 * d:4 * d]
        scale = mod_ref[:, 4 * d:5 * d]
        h = _rms(x_ref[...], g2_ref[...]) * (1.0 + scale) + shift
        h_ref[...] = h.astype(BF16)
        acc_ref[...] = jnp.zeros_like(acc_ref)

    u = jnp.maximum(_dot(h_ref[...], w1_ref[...]), 0.0)
    acc_ref[...] += _dot((u * u).astype(BF16), w2_ref[...])

    @pl.when(j == pl.num_programs(1) - 1)
    def _():
        gate_f = mod_ref[:, 5 * d:6 * d]
        o_ref[...] = x_ref[...] + gate_f * _rms(acc_ref[...], g3_ref[...])


def _mlp(x2, mod3, g2, g3, w1, w2, seq, tm=512, tf=512):
    T, D = x2.shape
    F = w1.shape[1]
    return pl.pallas_call(
        functools.partial(_mlp_kernel, d=D),
        out_shape=jax.ShapeDtypeStruct((T, D), F32),
        grid=(T // tm, F // tf),
        in_specs=[pl.BlockSpec((tm, D), lambda i, j: (i, 0)),
                  pl.BlockSpec((None, 1, mod3.shape[2]), lambda i, j: ((i * tm) // seq, 0, 0)),
                  pl.BlockSpec((1, D), lambda i, j: (0, 0)),
                  pl.BlockSpec((1, D), lambda i, j: (0, 0)),
                  pl.BlockSpec((D, tf), lambda i, j: (0, j)),
                  pl.BlockSpec((tf, D), lambda i, j: (j, 0))],
        out_specs=pl.BlockSpec((tm, D), lambda i, j: (i, 0)),
        scratch_shapes=[pltpu.VMEM((tm, D), BF16), pltpu.VMEM((tm, D), F32)],
        compiler_params=_cparams(("parallel", "arbitrary")),
        name="mlp",
    )(x2, mod3, g2, g3, w1, w2)


def _pad_rows(w, offset, total):
    return jnp.zeros((total, w.shape[1]), w.dtype).at[offset:offset + w.shape[0]].set(w)


def kernel(x, c, w_ada, b_ada, norm_g, w_in, mu_shift, w0, w2, a0, a2, g2, k_k, k_a,
           r_k, ln_x_w, ln_x_b, w_up_rwkv, w_up_sb, w_out, w_mlp_in, w_mlp_out):
    B, S, D = x.shape
    depth = w_in.shape[0]
    C = w0.shape[1]
    W = w_up_sb.shape[1]
    rwkv_cols = 3 * C + DECAY_LORA + ICLR_LORA + GATE_LORA
    sb_cols = 3 * W
    n_lora = DECAY_LORA + ICLR_LORA + GATE_LORA
    assert C == 1024 and W == 1024 and D == 2 * C and S % (2 * CHUNK) == 0

    hi = lax.broadcasted_iota(jnp.int32, (LANES, LANES), 0) // HEAD_DIM
    hj = lax.broadcasted_iota(jnp.int32, (LANES, LANES), 1) // HEAD_DIM
    bd = (hi == hj).astype(BF16)

    x2 = x.reshape(B * S, D)
    for l in range(depth):
        mod = _ada(c, w_ada[l], b_ada[l])
        mod3 = mod.reshape(B, 1, mod.shape[1])

        wl = w_in[l]
        g_off = rwkv_cols + sb_cols
        lora_w = jnp.pad(wl[:, 3 * C:rwkv_cols], ((0, 0), (0, LORA_PAD - n_lora)))
        w_perm = jnp.concatenate(
            [wl[:, g_off:], wl[:, :3 * C], wl[:, rwkv_cols:g_off], lora_w], axis=1).astype(BF16)
        P = _inproj(x2, mod3, norm_g[l, 0].reshape(1, D), w_perm, S)

        mu = mu_shift[l]
        consts = dict(
            col_r=2 * D // C, col_k=2 * D // C + 1, col_v=2 * D // C + 2,
            col_l=(2 * D + 3 * C + 3 * W) // LORA_PAD,
            mu_r=mu[None, :C], mu_k=mu[None, C:2 * C], mu_v=mu[None, 2 * C:3 * C],
            mu_l=jnp.pad(mu[None, 3 * C:], ((0, 0), (0, LORA_PAD - n_lora))),
            w0=w0[l][None], a0=a0[l][None], k_k=k_k[l][None], k_a=k_a[l][None],
            w2p=_pad_rows(w2[l], 0, LORA_PAD).astype(BF16),
            a2p=_pad_rows(a2[l], DECAY_LORA, LORA_PAD).astype(BF16),
            g2p=_pad_rows(g2[l], DECAY_LORA + ICLR_LORA, LORA_PAD).astype(BF16),
            bd=bd)
        prep = _prep(P, S, consts)
        ya = _scan(prep, r_k[l].reshape(1, C), ln_x_w[l][None], ln_x_b[l][None], bd, B, S)

        col_sb = (2 * D + 3 * C) // LANES
        yb = _stick_breaking(P.reshape(B, S, P.shape[1]), col_sb, col_sb + W // LANES,
                             col_sb + 2 * W // LANES, W)

        x2 = _merge(ya, yb, P, x2, mod3, norm_g[l, 1].reshape(1, D),
                    w_up_rwkv[l].astype(BF16), w_up_sb[l].astype(BF16), w_out[l].astype(BF16), S)
        x2 = _mlp(x2, mod3, norm_g[l, 2].reshape(1, D), norm_g[l, 3].reshape(1, D),
                  w_mlp_in[l].astype(BF16), w_mlp_out[l].astype(BF16), S)
    return x2.reshape(B, S, D)
```

```python
import functools
import math

import jax
import jax.numpy as jnp
from jax import lax
from jax.experimental import pallas as pl
from jax.experimental.pallas import tpu as pltpu

F32 = jnp.float32
BF16 = jnp.bfloat16

HEAD_DIM = 64
LANES = 128
DECAY_LORA = 64
ICLR_LORA = 64
GATE_LORA = 160
LORA_PAD = 512
NORM_EPS = 1e-6
GN_EPS = 64e-5
CHUNK = 128
SB_BLOCK = 128
SB_BLOCKS_PER_STEP = 3
SB_UNDERFLOW = 104.0
VMEM_LIMIT = 56 * 1024 * 1024


def _cparams(sem, vmem=VMEM_LIMIT):
    return pltpu.CompilerParams(dimension_semantics=sem, vmem_limit_bytes=vmem)


def _dot(a, b):
    return jnp.dot(a, b, preferred_element_type=F32)


def _dot_nt(a, b):
    return lax.dot_general(a, b, (((1,), (1,)), ((), ())), preferred_element_type=F32)


def _dot_tn(a, b):
    return lax.dot_general(a, b, (((0,), (0,)), ((), ())), preferred_element_type=F32)


def _split2(x):
    hi = x.astype(BF16)
    lo = (x - hi.astype(F32)).astype(BF16)
    return hi, lo


def _dot_exact01(x, m01):
    n = x.shape[0]
    res = _dot(jnp.concatenate(_split2(x), axis=0), m01)
    return res[:n] + res[n:]


def _exact01_dot(m01, x):
    n = x.shape[1]
    res = _dot(m01, jnp.concatenate(_split2(x), axis=1))
    return res[:, :n] + res[:, n:]


def _head_sum(x, bd):
    return _dot_exact01(x, bd)


def _rms(x, g):
    return x * lax.rsqrt(jnp.mean(x * x, axis=-1, keepdims=True) + NORM_EPS) * g


def _ada_kernel(c_ref, w_ref, b_ref, o_ref):
    c = c_ref[...]
    s = c * jax.nn.sigmoid(c)
    o_ref[...] = _dot(s.astype(BF16), w_ref[...].astype(BF16)) + b_ref[...]


def _ada(c, w_ada, b_ada, tn=1536):
    B, D = c.shape
    N = w_ada.shape[1]
    rows = 8
    c_pad = jnp.zeros((rows, D), F32).at[:B].set(c)
    out = pl.pallas_call(
        _ada_kernel,
        out_shape=jax.ShapeDtypeStruct((rows, N), F32),
        grid=(N // tn,),
        in_specs=[pl.BlockSpec((rows, D), lambda j: (0, 0)),
                  pl.BlockSpec((D, tn), lambda j: (0, j)),
                  pl.BlockSpec((1, tn), lambda j: (0, j))],
        out_specs=pl.BlockSpec((rows, tn), lambda j: (0, j)),
        compiler_params=_cparams(("parallel",)),
        name="ada",
    )(c_pad, w_ada, b_ada.reshape(1, N))
    return out[:B]


def _inproj_kernel(x_ref, mod_ref, g_ref, w_ref, o_ref, sb_ref, h_ref, *, d, n_main):
    j = pl.program_id(1)

    @pl.when(j == 0)
    def _():
        shift = mod_ref[:, 0:d]
        scale = mod_ref[:, d:2 * d]
        h = _rms(x_ref[...], g_ref[...]) * (1.0 + scale) + shift
        h_ref[...] = h.astype(BF16)

    @pl.when(j < n_main)
    def _():
        o_ref[...] = _dot(h_ref[...], w_ref[...])

    @pl.when(j >= n_main)
    def _():
        sb_ref[...] = _dot(h_ref[...], w_ref[...]).astype(BF16)


def _inproj(x2, mod3, g0, w_bf, n_main_cols, seq, tm=1024, tn=512):
    T, D = x2.shape
    N = w_bf.shape[1]
    n_main = n_main_cols // tn
    return pl.pallas_call(
        functools.partial(_inproj_kernel, d=D, n_main=n_main),
        out_shape=[jax.ShapeDtypeStruct((T, n_main_cols), F32),
                   jax.ShapeDtypeStruct((T, N - n_main_cols), BF16)],
        grid=(T // tm, N // tn),
        in_specs=[pl.BlockSpec((tm, D), lambda i, j: (i, 0)),
                  pl.BlockSpec((None, 1, mod3.shape[2]), lambda i, j: ((i * tm) // seq, 0, 0)),
                  pl.BlockSpec((1, D), lambda i, j: (0, 0)),
                  pl.BlockSpec((D, tn), lambda i, j: (0, j))],
        out_specs=[pl.BlockSpec((tm, tn), lambda i, j: (i, jnp.minimum(j, n_main - 1))),
                   pl.BlockSpec((tm, tn), lambda i, j: (i, jnp.maximum(j - n_main, 0)))],
        scratch_shapes=[pltpu.VMEM((tm, D), BF16)],
        compiler_params=_cparams(("parallel", "arbitrary")),
        name="inproj",
    )(x2, mod3, g0, w_bf)


def _prep_kernel(r_ref, k_ref, v_ref, l_ref, rp_ref, kp_ref, vp_ref, lp_ref,
                 mur_ref, muk_ref, muv_ref, mul_ref, w0_ref, a0_ref, kk_ref, ka_ref,
                 w2_ref, a2_ref, g2_ref, bd_ref,
                 ro_ref, ko_ref, vo_ref, kko_ref, bo_ref, lwo_ref, go_ref, *, tm, seq):
    i = pl.program_id(0)
    first = (i * tm) % seq == 0
    row = lax.broadcasted_iota(jnp.int32, (tm, 1), 0)

    def mixed(cur_ref, prev_ref, mu_ref):
        cur = cur_ref[...]
        prev_row = jnp.where(first, 0.0, prev_ref[7:8, :])
        shifted = jnp.where(row == 0, prev_row, pltpu.roll(cur, 1, 0))
        return cur + (shifted - cur) * mu_ref[...]

    r = mixed(r_ref, rp_ref, mur_ref)
    k = mixed(k_ref, kp_ref, muk_ref)
    v = mixed(v_ref, vp_ref, muv_ref)
    lo = mixed(l_ref, lp_ref, mul_ref)

    dw = _dot(jnp.tanh(lo).astype(BF16), w2_ref[...])
    da = _dot(lo.astype(BF16), a2_ref[...])
    g = _dot(jax.nn.sigmoid(lo).astype(BF16), g2_ref[...])

    w_log = -jax.nn.softplus(-(w0_ref[...] + dw)) - 0.5
    lw = -jnp.exp(w_log)
    a = jax.nn.sigmoid(a0_ref[...] + da)
    kx = k * kk_ref[...]
    bd = bd_ref[...]
    n_blk = kx.shape[1] // LANES
    sq = kx * kx
    ssum = jnp.concatenate(
        [_head_sum(sq[:, c * LANES:(c + 1) * LANES], bd) for c in range(n_blk)], axis=1)
    kk = kx * lax.rsqrt(jnp.maximum(ssum, 1e-24))
    k2 = k * (1.0 + (a - 1.0) * ka_ref[...])

    ro_ref[...] = r
    ko_ref[...] = k2
    vo_ref[...] = v
    kko_ref[...] = kk
    bo_ref[...] = kk * a
    lwo_ref[...] = lw
    go_ref[...] = g


def _prep(P, seq, consts, tm=256):
    T = P.shape[0]
    C = consts["w0"].shape[1]
    cb = C // 1024
    assert cb == 1
    col_r, col_k, col_v, col_l = consts["col_r"], consts["col_k"], consts["col_v"], consts["col_l"]

    def cur(col, w):
        return pl.BlockSpec((tm, w), lambda i: (i, col))

    def prev(col, w):
        return pl.BlockSpec((8, w), lambda i: (jnp.maximum(i * (tm // 8) - 1, 0), col))

    def const(shape):
        return pl.BlockSpec(shape, lambda i: (0, 0))

    out_sd = jax.ShapeDtypeStruct((T, C), F32)
    outs = pl.pallas_call(
        functools.partial(_prep_kernel, tm=tm, seq=seq),
        out_shape=[out_sd] * 7,
        grid=(T // tm,),
        in_specs=[cur(col_r, C), cur(col_k, C), cur(col_v, C), cur(col_l, LORA_PAD),
                  prev(col_r, C), prev(col_k, C), prev(col_v, C), prev(col_l, LORA_PAD),
                  const((1, C)), const((1, C)), const((1, C)), const((1, LORA_PAD)),
                  const((1, C)), const((1, C)), const((1, C)), const((1, C)),
                  const((LORA_PAD, C)), const((LORA_PAD, C)), const((LORA_PAD, C)),
                  const((LANES, LANES))],
        out_specs=[pl.BlockSpec((tm, C), lambda i: (i, 0))] * 7,
        compiler_params=_cparams(("parallel",)),
        name="rwkv_prep",
    )(P, P, P, P, P, P, P, P,
      consts["mu_r"], consts["mu_k"], consts["mu_v"], consts["mu_l"],
      consts["w0"], consts["a0"], consts["k_k"], consts["k_a"],
      consts["w2p"], consts["a2p"], consts["g2p"], consts["bd"])
    return outs


def _scan_kernel(r_ref, k_ref, v_ref, kk_ref, b_ref, lw_ref, g_ref,
                 rk_ref, lnw_ref, lnb_ref, bd_ref, o_ref, h_ref, *, nc, npair):
    L = CHUNK
    half = L // 2

    @pl.when(pl.program_id(2) == 0)
    def _():
        h_ref[...] = jnp.zeros_like(h_ref)

    ri = lax.broadcasted_iota(jnp.int32, (L, L), 0)
    ci = lax.broadcasted_iota(jnp.int32, (L, L), 1)
    lower_incl = ri >= ci
    lower_strict = ri > ci
    eye = ri == ci
    same_head = (ri < HEAD_DIM) == (ci < HEAD_DIM)
    tri01 = lower_incl.astype(BF16)
    eye_f = eye.astype(F32)
    lane = lax.broadcasted_iota(jnp.int32, (1, LANES), 1)
    head0 = lane < HEAD_DIM
    bd = bd_ref[...]

    lane2 = lax.broadcasted_iota(jnp.int32, (1, 2 * LANES), 1)
    head0_2 = (lane2 % LANES) < HEAD_DIM
    tri_stack = jnp.concatenate([lower_strict, lower_incl], axis=0)
    chunks = range(nc * npair)
    pairs = [(c, h) for c in chunks for h in range(2)]

    def sel(x0, x1):
        return jnp.where(head0 if x0.shape[1] == LANES else head0_2, x0, x1)

    def lanes_of(u):
        return slice((u // nc) * LANES, (u // nc + 1) * LANES)

    def ld(ref, u):
        return ref[pl.ds((u % nc) * L, L), lanes_of(u)]

    r = [ld(r_ref, c) for c in chunks]
    k = [ld(k_ref, c) for c in chunks]
    v = [ld(v_ref, c) for c in chunks]
    kk = [ld(kk_ref, c) for c in chunks]
    b = [ld(b_ref, c) for c in chunks]
    lw = [ld(lw_ref, c) for c in chunks]

    cum = [_exact01_dot(tri01, lw[c]) for c in chunks]
    lhs_mid, rhs_mid, kd_bf, r_dec, k_end, b_end, p_end, v_bf = [], [], [], [], [], [], [], []
    for c in chunks:
        mid = cum[c][half - 1:half, :]
        end = cum[c][L - 1:L, :]
        r_mid = r[c] * jnp.exp(cum[c] - mid)
        kk_mid = kk[c] * jnp.exp(cum[c] - lw[c] - mid)
        e_back = jnp.exp(mid - cum[c])
        lhs_mid.append(jnp.concatenate([kk_mid, r_mid], axis=0))
        rhs_mid.append(jnp.concatenate([b[c] * e_back, k[c] * e_back], axis=0).astype(BF16))
        kd_bf.append((kk[c] * jnp.exp(cum[c] - lw[c])).astype(BF16))
        r_dec.append(r[c] * jnp.exp(cum[c]))
        e_end = jnp.exp(end - cum[c])
        k_end.append((k[c] * e_end).astype(BF16))
        b_end.append((b[c] * e_end).astype(BF16))
        p_end.append(jnp.exp(end))
        v_bf.append(v[c].astype(BF16))

    aa = {}
    for (c, h) in pairs:
        hm = head0 if h == 0 else jnp.logical_not(head0)
        aa[c, h] = _dot_nt(jnp.where(hm, lhs_mid[c], 0.0).astype(BF16), rhs_mid[c])
    def side(f):
        return jnp.concatenate([f(0), f(1)], axis=1)

    tri2 = jnp.concatenate([tri_stack, tri_stack], axis=1)
    strict2 = jnp.concatenate([lower_strict, lower_strict], axis=1)
    incl2 = jnp.concatenate([lower_incl, lower_incl], axis=1)
    eye2 = jnp.concatenate([eye_f, eye_f], axis=1)
    row2 = lax.broadcasted_iota(jnp.int32, (2 * L, 2 * L), 0) < L
    col2 = lax.broadcasted_iota(jnp.int32, (2 * L, 2 * L), 1) < L
    diag_blocks = row2 == col2

    def by_head_rows(x):
        hm = head0 if x.shape[1] == LANES else head0_2
        zero = jnp.zeros_like(x)
        return jnp.concatenate([jnp.where(hm, x, zero), jnp.where(hm, zero, x)], axis=0)

    def block_diag(x2):
        return jnp.where(diag_blocks, jnp.concatenate([x2, x2], axis=0), jnp.zeros((), x2.dtype))

    a_kb = [jnp.where(strict2, side(lambda h: aa[c, h][:L, :L]), 0.0) for c in chunks]
    a_rb = [jnp.where(incl2, side(lambda h: aa[c, h][L:, :L]), 0.0).astype(BF16) for c in chunks]
    a_xk = [jnp.where(tri2, side(lambda h: aa[c, h][:, L:]), 0.0).astype(BF16) for c in chunks]
    avr = [_dot(a_xk[c], by_head_rows(v_bf[c])) for c in chunks]

    n_lvl = int(math.log2(L)) - 1
    p, t = [], []
    for c in chunks:
        a_bf = a_kb[c].astype(BF16)
        p.append(_dot(a_bf, block_diag(a_bf)))
        t.append(eye2 - a_kb[c])
    for lvl in range(n_lvl):
        for c in chunks:
            p_bd = block_diag(p[c].astype(BF16))
            if lvl + 1 < n_lvl:
                tp = _dot(jnp.concatenate([t[c], p[c]], axis=0).astype(BF16), p_bd)
                t[c] = t[c] + tp[:L]
                p[c] = tp[L:]
            else:
                t[c] = t[c] + _dot(t[c].astype(BF16), p_bd)

    rq, y0, gmat, cmat = [], [], [], []
    for c in chunks:
        av = avr[c][:L]
        arkv = avr[c][L:]
        x = jnp.concatenate([kd_bf[c], av.astype(BF16)], axis=1)
        w_bf = _dot(t[c].astype(BF16), by_head_rows(x)).astype(BF16)
        aw = _dot(a_rb[c], by_head_rows(w_bf))
        bw = _dot_tn(b_end[c], w_bf)
        kv = _dot_tn(k_end[c], v_bf[c])
        rq.append(r_dec[c] - aw[:, :LANES])
        y0.append(arkv - aw[:, LANES:])
        gmat.append(eye_f * p_end[c] - jnp.where(same_head, bw[:, :LANES], 0.0))
        cmat.append(jnp.where(same_head, kv - bw[:, LANES:], 0.0))

    bd2 = ((lax.broadcasted_iota(jnp.int32, (2 * LANES, 2 * LANES), 0) // HEAD_DIM)
           == (lax.broadcasted_iota(jnp.int32, (2 * LANES, 2 * LANES), 1) // HEAD_DIM)).astype(BF16)

    def head_sum_all(x):
        n = x.shape[0] // 2
        s2 = _dot_exact01(jnp.concatenate([x[:n], x[n:]], axis=1), bd2)
        return jnp.concatenate([s2[:, :LANES], s2[:, LANES:]], axis=0)

    rkr = head_sum_all(jnp.concatenate(
        [r[u] * k[u] * rk_ref[:, lanes_of(u)] for u in chunks], axis=0))
    bonus = [rkr[u * L:(u + 1) * L] * v[u] for u in chunks]

    inv_n = 1.0 / HEAD_DIM
    hstate = [h_ref[q] for q in range(npair)]
    ys = [None] * len(chunks)
    for c in range(nc):
        for q in range(npair):
            u = q * nc + c
            res = _dot(jnp.concatenate([rq[u], gmat[u]], axis=0).astype(BF16),
                       hstate[q].astype(BF16))
            ys[u] = res[:L] + y0[u]
            hstate[q] = res[L:] + cmat[u]
    for q in range(npair):
        h_ref[q] = hstate[q]
    y = jnp.concatenate(ys, axis=0)
    mean = head_sum_all(y) * inv_n
    yc = y - mean
    var = head_sum_all(yc * yc) * inv_n
    yn = yc * lax.rsqrt(var + GN_EPS)
    for u in chunks:
        rows = pl.ds((u % nc) * L, L)
        o_ref[rows, lanes_of(u)] = (
            (yn[u * L:(u + 1) * L] * lnw_ref[:, lanes_of(u)] + lnb_ref[:, lanes_of(u)] + bonus[u])
            * g_ref[rows, lanes_of(u)])


def _scan(arrs, rk, lnw, lnb, bd, batch, seq, nc=4, npair=4):
    r, k, v, kk, b, lw, g = [a.reshape(batch, seq, a.shape[1]) for a in arrs]
    C = r.shape[2]
    width = LANES * npair
    blk = pl.BlockSpec((None, CHUNK * nc, width), lambda bi, p, c: (bi, c, p))
    par = pl.BlockSpec((1, width), lambda bi, p, c: (0, p))
    out = pl.pallas_call(
        functools.partial(_scan_kernel, nc=nc, npair=npair),
        out_shape=jax.ShapeDtypeStruct((batch, seq, C), F32),
        grid=(batch, C // width, seq // (CHUNK * nc)),
        in_specs=[blk] * 7 + [par, par, par,
                              pl.BlockSpec((LANES, LANES), lambda bi, p, c: (0, 0))],
        out_specs=blk,
        scratch_shapes=[pltpu.VMEM((npair, LANES, LANES), F32)],
        compiler_params=_cparams(("parallel", "parallel", "arbitrary")),
        name="rwkv_scan",
    )(r, k, v, kk, b, lw, g, rk, lnw, lnb, bd)
    return out.reshape(batch * seq, C)


def _sb_kernel(q_ref, k_ref, v_ref, o_ref, *, npair):
    tq = tk = SB_BLOCK
    nu = SB_BLOCKS_PER_STEP
    i = pl.program_id(2)
    lane = lax.broadcasted_iota(jnp.int32, (1, LANES), 1)
    head0 = lane < HEAD_DIM
    scale = jnp.asarray(1.0 / math.sqrt(HEAD_DIM), BF16)
    q = [q_ref[:, p * LANES:(p + 1) * LANES] * scale for p in range(npair)]
    r2 = lax.broadcasted_iota(jnp.int32, (2 * tk, 2 * tk), 0)
    c2 = lax.broadcasted_iota(jnp.int32, (2 * tk, 2 * tk), 1)
    after01 = jnp.logical_and(r2 > c2, (r2 < tk) == (c2 < tk)).astype(BF16)
    key_minus_query = (lax.broadcasted_iota(jnp.int32, (tq, 2 * tk), 1) % tk
                       - lax.broadcasted_iota(jnp.int32, (tq, 2 * tk), 0))
    units = [(p, u) for p in range(npair) for u in range(nu)]

    def by_head_rows(x):
        zero = jnp.zeros_like(x)
        return jnp.concatenate([jnp.where(head0, x, zero), jnp.where(head0, zero, x)], axis=0)

    def both(c0, c1):
        return jnp.concatenate([jnp.broadcast_to(c0, (tq, tk)), jnp.broadcast_to(c1, (tq, tk))],
                               axis=1)

    def body(carry):
        jt, _, offs, accs = carry
        z, sp, log_keep, mask, vb = {}, {}, {}, {}, {}
        for (p, u) in units:
            jb = jt - u
            start = pl.multiple_of(jnp.maximum(jb, 0) * tk, tk)
            lanes = slice(p * LANES, (p + 1) * LANES)
            kb = by_head_rows(k_ref[pl.ds(start, tk), lanes])
            vb[p, u] = by_head_rows(v_ref[pl.ds(start, tk), lanes])
            zu = _dot_nt(q[p], kb)
            if u > 0:
                zu = zu + jnp.where(jb >= 0, 0.0, -1e30)
            spu = jnp.maximum(zu, 0.0) + jnp.log(1.0 + jnp.exp(-jnp.abs(zu)))
            z[p, u], sp[p, u] = zu, spu
            if u == 0:
                mask[p, u] = key_minus_query < (i * tq - jb * tk)
                log_keep[p, u] = jnp.where(mask[p, u], -spu, 0.0)
            else:
                log_keep[p, u] = -spu
        parts = []
        for key in units:
            parts.extend(_split2(log_keep[key]))
        loc = _dot(jnp.concatenate(parts, axis=0), after01)
        offs, accs = list(offs), list(accs)
        for n, (p, u) in enumerate(units):
            local = loc[2 * n * tq:(2 * n + 1) * tq] + loc[(2 * n + 1) * tq:(2 * n + 2) * tq]
            later = local + both(offs[2 * p], offs[2 * p + 1])
            att = jnp.exp(z[p, u] - sp[p, u] + later)
            if u == 0:
                att = jnp.where(mask[p, u], att, 0.0)
            accs[p] = accs[p] + _dot(att.astype(BF16), vb[p, u])
            offs[2 * p] = offs[2 * p] + (local[:, 0:1] + log_keep[p, u][:, 0:1])
            offs[2 * p + 1] = offs[2 * p + 1] + (local[:, tk:tk + 1] + log_keep[p, u][:, tk:tk + 1])
        top = offs[0]
        for o in offs[1:]:
            top = jnp.maximum(top, o)
        alive = (jnp.max(top) > -SB_UNDERFLOW).astype(jnp.int32)
        return jt - nu, alive, tuple(offs), tuple(accs)

    def cond(carry):
        jt, alive = carry[0], carry[1]
        return jnp.logical_and(jt >= 0, alive > 0)

    init = (i, jnp.int32(1),
            tuple(jnp.zeros((tq, 1), F32) for _ in range(2 * npair)),
            tuple(jnp.zeros((tq, LANES), F32) for _ in range(npair)))
    accs = lax.while_loop(cond, body, init)[3]
    for p in range(npair):
        o_ref[:, p * LANES:(p + 1) * LANES] = accs[p]


def _stick_breaking(qkv, width, npair=2):
    B, S, _ = qkv.shape
    w = LANES * npair
    nblk = width // w
    out = pl.pallas_call(
        functools.partial(_sb_kernel, npair=npair),
        out_shape=jax.ShapeDtypeStruct((B, S, width), F32),
        grid=(B, nblk, S // SB_BLOCK),
        in_specs=[pl.BlockSpec((None, SB_BLOCK, w), lambda b, p, i: (b, i, p)),
                  pl.BlockSpec((None, S, w), lambda b, p, i: (b, 0, nblk + p)),
                  pl.BlockSpec((None, S, w), lambda b, p, i: (b, 0, 2 * nblk + p))],
        out_specs=pl.BlockSpec((None, SB_BLOCK, w), lambda b, p, i: (b, i, p)),
        compiler_params=_cparams(("parallel", "parallel", "arbitrary")),
        name="stick_breaking",
    )(qkv, qkv, qkv)
    return out.reshape(B * S, width)


def _merge_kernel(ya_ref, yb_ref, ga_ref, gb_ref, x_ref, mod_ref, g1_ref,
                  wua_ref, wub_ref, wo_ref, o_ref, *, d):
    ua = _dot(ya_ref[...].astype(BF16), wua_ref[...])
    ub = _dot(yb_ref[...].astype(BF16), wub_ref[...])
    merged = jax.nn.sigmoid(ga_ref[...]) * ua + jax.nn.sigmoid(gb_ref[...]) * ub
    mix = _dot(merged.astype(BF16), wo_ref[...])
    gate_m = mod_ref[:, 2 * d:3 * d]
    o_ref[...] = x_ref[...] + gate_m * _rms(mix, g1_ref[...])


def _merge(ya, yb, P, x2, mod3, g1, wua, wub, wo, seq, tm=256):
    T, D = x2.shape
    C = ya.shape[1]

    def const(shape):
        return pl.BlockSpec(shape, lambda i: (0, 0), pipeline_mode=pl.Buffered(1))

    return pl.pallas_call(
        functools.partial(_merge_kernel, d=D),
        out_shape=jax.ShapeDtypeStruct((T, D), F32),
        grid=(T // tm,),
        in_specs=[pl.BlockSpec((tm, C), lambda i: (i, 0)),
                  pl.BlockSpec((tm, C), lambda i: (i, 0)),
                  pl.BlockSpec((tm, D), lambda i: (i, 0)),
                  pl.BlockSpec((tm, D), lambda i: (i, 1)),
                  pl.BlockSpec((tm, D), lambda i: (i, 0)),
                  pl.BlockSpec((None, 1, mod3.shape[2]), lambda i: ((i * tm) // seq, 0, 0)),
                  pl.BlockSpec((1, D), lambda i: (0, 0)),
                  const((C, D)), const((C, D)), const((D, D))],
        out_specs=pl.BlockSpec((tm, D), lambda i: (i, 0)),
        compiler_params=_cparams(("parallel",)),
        name="merge_out",
    )(ya, yb, P, P, x2, mod3, g1, wua, wub, wo)


def _mlp_kernel(x_ref, mod_ref, g2_ref, g3_ref, w1_ref, w2_ref, o_ref, h_ref, acc_ref, *, d):
    j = pl.program_id(1)

    @pl.when(j == 0)
    def _():
        shift = mod_ref[:, 3 * d:4 * d]
        scale = mod_ref[:, 4 * d:5 * d]
        h = _rms(x_ref[...], g2_ref[...]) * (1.0 + scale) + shift
        h_ref[...] = h.astype(BF16)
        acc_ref[...] = jnp.zeros_like(acc_ref)

    u = jnp.maximum(_dot(h_ref[...], w1_ref[...]), 0.0)
    acc_ref[...] += _dot((u * u).astype(BF16), w2_ref[...])

    @pl.when(j == pl.num_programs(1) - 1)
    def _():
        gate_f = mod_ref[:, 5 * d:6 * d]
        o_ref[...] = x_ref[...] + gate_f * _rms(acc_ref[...], g3_ref[...])


def _mlp(x2, mod3, g2, g3, w1, w2, seq, tm=512, tf=512):
    T, D = x2.shape
    F = w1.shape[1]
    return pl.pallas_call(
        functools.partial(_mlp_kernel, d=D),
        out_shape=jax.ShapeDtypeStruct((T, D), F32),
        grid=(T // tm, F // tf),
        in_specs=[pl.BlockSpec((tm, D), lambda i, j: (i, 0)),
                  pl.BlockSpec((None, 1, mod3.shape[2]), lambda i, j: ((i * tm) // seq, 0, 0)),
                  pl.BlockSpec((1, D), lambda i, j: (0, 0)),
                  pl.BlockSpec((1, D), lambda i, j: (0, 0)),
                  pl.BlockSpec((D, tf), lambda i, j: (0, j)),
                  pl.BlockSpec((tf, D), lambda i, j: (j, 0))],
        out_specs=pl.BlockSpec((tm, D), lambda i, j: (i, 0)),
        scratch_shapes=[pltpu.VMEM((tm, D), BF16), pltpu.VMEM((tm, D), F32)],
        compiler_params=_cparams(("parallel", "arbitrary")),
        name="mlp",
    )(x2, mod3, g2, g3, w1, w2)


def _pad_rows(w, offset, total):
    return jnp.zeros((total, w.shape[1]), w.dtype).at[offset:offset + w.shape[0]].set(w)


def kernel(x, c, w_ada, b_ada, norm_g, w_in, mu_shift, w0, w2, a0, a2, g2, k_k, k_a,
           r_k, ln_x_w, ln_x_b, w_up_rwkv, w_up_sb, w_out, w_mlp_in, w_mlp_out):
    B, S, D = x.shape
    depth = w_in.shape[0]
    C = w0.shape[1]
    W = w_up_sb.shape[1]
    rwkv_cols = 3 * C + DECAY_LORA + ICLR_LORA + GATE_LORA
    sb_cols = 3 * W
    n_lora = DECAY_LORA + ICLR_LORA + GATE_LORA
    assert C == 1024 and W == 1024 and D == 2 * C and S % (2 * CHUNK) == 0

    hi = lax.broadcasted_iota(jnp.int32, (LANES, LANES), 0) // HEAD_DIM
    hj = lax.broadcasted_iota(jnp.int32, (LANES, LANES), 1) // HEAD_DIM
    bd = (hi == hj).astype(BF16)

    x2 = x.reshape(B * S, D)
    for l in range(depth):
        mod = _ada(c, w_ada[l], b_ada[l])
        mod3 = mod.reshape(B, 1, mod.shape[1])

        wl = w_in[l]
        g_off = rwkv_cols + sb_cols
        lora_w = jnp.pad(wl[:, 3 * C:rwkv_cols], ((0, 0), (0, LORA_PAD - n_lora)))
        w_perm = jnp.concatenate(
            [wl[:, g_off:], wl[:, :3 * C], lora_w, wl[:, rwkv_cols:g_off]], axis=1).astype(BF16)
        n_main_cols = 2 * D + 3 * C + LORA_PAD
        P, qkv = _inproj(x2, mod3, norm_g[l, 0].reshape(1, D), w_perm, n_main_cols, S)

        mu = mu_shift[l]
        consts = dict(
            col_r=2 * D // C, col_k=2 * D // C + 1, col_v=2 * D // C + 2,
            col_l=(2 * D + 3 * C) // LORA_PAD,
            mu_r=mu[None, :C], mu_k=mu[None, C:2 * C], mu_v=mu[None, 2 * C:3 * C],
            mu_l=jnp.pad(mu[None, 3 * C:], ((0, 0), (0, LORA_PAD - n_lora))),
            w0=w0[l][None], a0=a0[l][None], k_k=k_k[l][None], k_a=k_a[l][None],
            w2p=_pad_rows(w2[l], 0, LORA_PAD).astype(BF16),
            a2p=_pad_rows(a2[l], DECAY_LORA, LORA_PAD).astype(BF16),
            g2p=_pad_rows(g2[l], DECAY_LORA + ICLR_LORA, LORA_PAD).astype(BF16),
            bd=bd)
        prep = _prep(P, S, consts)
        ya = _scan(prep, r_k[l].reshape(1, C), ln_x_w[l][None], ln_x_b[l][None], bd, B, S)

        yb = _stick_breaking(qkv.reshape(B, S, 3 * W), W)

        x2 = _merge(ya, yb, P, x2, mod3, norm_g[l, 1].reshape(1, D),
                    w_up_rwkv[l].astype(BF16), w_up_sb[l].astype(BF16), w_out[l].astype(BF16), S)
        x2 = _mlp(x2, mod3, norm_g[l, 2].reshape(1, D), norm_g[l, 3].reshape(1, D),
                  w_mlp_in[l].astype(BF16), w_mlp_out[l].astype(BF16), S)
    return x2.reshape(B, S, D)
```

```python
import functools
import math

import jax
import jax.numpy as jnp
from jax import lax
from jax.experimental import pallas as pl
from jax.experimental.pallas import tpu as pltpu

F32 = jnp.float32
BF16 = jnp.bfloat16

HEAD_DIM = 64
LANES = 128
DECAY_LORA = 64
ICLR_LORA = 64
GATE_LORA = 160
LORA_PAD = 512
NORM_EPS = 1e-6
GN_EPS = 64e-5
CHUNK = 128
SB_BLOCK = 128
SB_BLOCKS_PER_STEP = 3
SB_UNDERFLOW = 104.0
VMEM_LIMIT = 56 * 1024 * 1024


def _cparams(sem, vmem=VMEM_LIMIT):
    return pltpu.CompilerParams(dimension_semantics=sem, vmem_limit_bytes=vmem)


def _dot(a, b):
    return jnp.dot(a, b, preferred_element_type=F32)


def _dot_nt(a, b):
    return lax.dot_general(a, b, (((1,), (1,)), ((), ())), preferred_element_type=F32)


def _dot_tn(a, b):
    return lax.dot_general(a, b, (((0,), (0,)), ((), ())), preferred_element_type=F32)


def _split2(x):
    hi = x.astype(BF16)
    lo = (x - hi.astype(F32)).astype(BF16)
    return hi, lo


def _dot_exact01(x, m01):
    n = x.shape[0]
    res = _dot(jnp.concatenate(_split2(x), axis=0), m01)
    return res[:n] + res[n:]


def _exact01_dot(m01, x):
    n = x.shape[1]
    res = _dot(m01, jnp.concatenate(_split2(x), axis=1))
    return res[:, :n] + res[:, n:]


def _head_sum(x, bd):
    return _dot_exact01(x, bd)


def _rms(x, g):
    return x * lax.rsqrt(jnp.mean(x * x, axis=-1, keepdims=True) + NORM_EPS) * g


def _ada_kernel(c_ref, w_ref, b_ref, o_ref):
    c = c_ref[...]
    s = c * jax.nn.sigmoid(c)
    o_ref[...] = _dot(s.astype(BF16), w_ref[...].astype(BF16)) + b_ref[...]


def _ada(c, w_ada, b_ada, tn=1536):
    B, D = c.shape
    N = w_ada.shape[1]
    rows = 8
    c_pad = jnp.zeros((rows, D), F32).at[:B].set(c)
    out = pl.pallas_call(
        _ada_kernel,
        out_shape=jax.ShapeDtypeStruct((rows, N), F32),
        grid=(N // tn,),
        in_specs=[pl.BlockSpec((rows, D), lambda j: (0, 0)),
                  pl.BlockSpec((D, tn), lambda j: (0, j)),
                  pl.BlockSpec((1, tn), lambda j: (0, j))],
        out_specs=pl.BlockSpec((rows, tn), lambda j: (0, j)),
        compiler_params=_cparams(("parallel",)),
        name="ada",
    )(c_pad, w_ada, b_ada.reshape(1, N))
    return out[:B]


def _inproj_kernel(x_ref, mod_ref, g_ref, w_ref, o_ref, sb_ref, h_ref, *, d, n_main):
    j = pl.program_id(1)

    @pl.when(j == 0)
    def _():
        shift = mod_ref[:, 0:d]
        scale = mod_ref[:, d:2 * d]
        h = _rms(x_ref[...], g_ref[...]) * (1.0 + scale) + shift
        h_ref[...] = h.astype(BF16)

    @pl.when(j < n_main)
    def _():
        o_ref[...] = _dot(h_ref[...], w_ref[...])

    @pl.when(j >= n_main)
    def _():
        sb_ref[...] = _dot(h_ref[...], w_ref[...]).astype(BF16)


def _inproj(x2, mod3, g0, w_bf, n_main_cols, seq, tm=1024, tn=768):
    T, D = x2.shape
    N = w_bf.shape[1]
    n_main = n_main_cols // tn
    return pl.pallas_call(
        functools.partial(_inproj_kernel, d=D, n_main=n_main),
        out_shape=[jax.ShapeDtypeStruct((T, n_main_cols), F32),
                   jax.ShapeDtypeStruct((T, N - n_main_cols), BF16)],
        grid=(T // tm, N // tn),
        in_specs=[pl.BlockSpec((tm, D), lambda i, j: (i, 0)),
                  pl.BlockSpec((None, 1, mod3.shape[2]), lambda i, j: ((i * tm) // seq, 0, 0)),
                  pl.BlockSpec((1, D), lambda i, j: (0, 0)),
                  pl.BlockSpec((D, tn), lambda i, j: (0, j))],
        out_specs=[pl.BlockSpec((tm, tn), lambda i, j: (i, jnp.minimum(j, n_main - 1))),
                   pl.BlockSpec((tm, tn), lambda i, j: (i, jnp.maximum(j - n_main, 0)))],
        scratch_shapes=[pltpu.VMEM((tm, D), BF16)],
        compiler_params=_cparams(("parallel", "arbitrary")),
        name="inproj",
    )(x2, mod3, g0, w_bf)


def _prep_kernel(r_ref, k_ref, v_ref, l_ref, rp_ref, kp_ref, vp_ref, lp_ref,
                 mur_ref, muk_ref, muv_ref, mul_ref, w0_ref, a0_ref, kk_ref, ka_ref,
                 w2_ref, a2_ref, g2_ref, bd_ref,
                 ro_ref, ko_ref, vo_ref, kko_ref, bo_ref, lwo_ref, go_ref, *, tm, seq):
    i = pl.program_id(0)
    first = (i * tm) % seq == 0
    row = lax.broadcasted_iota(jnp.int32, (tm, 1), 0)

    def mixed(cur_ref, prev_ref, mu_ref):
        cur = cur_ref[...]
        prev_row = jnp.where(first, 0.0, prev_ref[7:8, :])
        shifted = jnp.where(row == 0, prev_row, pltpu.roll(cur, 1, 0))
        return cur + (shifted - cur) * mu_ref[...]

    r = mixed(r_ref, rp_ref, mur_ref)
    k = mixed(k_ref, kp_ref, muk_ref)
    v = mixed(v_ref, vp_ref, muv_ref)
    lo = mixed(l_ref, lp_ref, mul_ref)

    dw = _dot(jnp.tanh(lo).astype(BF16), w2_ref[...])
    da = _dot(lo.astype(BF16), a2_ref[...])
    g = _dot(jax.nn.sigmoid(lo).astype(BF16), g2_ref[...])

    w_log = -jax.nn.softplus(-(w0_ref[...] + dw)) - 0.5
    lw = -jnp.exp(w_log)
    a = jax.nn.sigmoid(a0_ref[...] + da)
    kx = k * kk_ref[...]
    bd = bd_ref[...]
    n_blk = kx.shape[1] // LANES
    sq = kx * kx
    ssum = jnp.concatenate(
        [_head_sum(sq[:, c * LANES:(c + 1) * LANES], bd) for c in range(n_blk)], axis=1)
    kk = kx * lax.rsqrt(jnp.maximum(ssum, 1e-24))
    k2 = k * (1.0 + (a - 1.0) * ka_ref[...])

    ro_ref[...] = r
    ko_ref[...] = k2
    vo_ref[...] = v
    kko_ref[...] = kk
    bo_ref[...] = kk * a
    lwo_ref[...] = lw
    go_ref[...] = g


def _prep(P, seq, consts, tm=256):
    T = P.shape[0]
    C = consts["w0"].shape[1]
    cb = C // 1024
    assert cb == 1
    col_r, col_k, col_v, col_l = consts["col_r"], consts["col_k"], consts["col_v"], consts["col_l"]

    def cur(col, w):
        return pl.BlockSpec((tm, w), lambda i: (i, col))

    def prev(col, w):
        return pl.BlockSpec((8, w), lambda i: (jnp.maximum(i * (tm // 8) - 1, 0), col))

    def const(shape):
        return pl.BlockSpec(shape, lambda i: (0, 0))

    out_sd = jax.ShapeDtypeStruct((T, C), F32)
    outs = pl.pallas_call(
        functools.partial(_prep_kernel, tm=tm, seq=seq),
        out_shape=[out_sd] * 7,
        grid=(T // tm,),
        in_specs=[cur(col_r, C), cur(col_k, C), cur(col_v, C), cur(col_l, LORA_PAD),
                  prev(col_r, C), prev(col_k, C), prev(col_v, C), prev(col_l, LORA_PAD),
                  const((1, C)), const((1, C)), const((1, C)), const((1, LORA_PAD)),
                  const((1, C)), const((1, C)), const((1, C)), const((1, C)),
                  const((LORA_PAD, C)), const((LORA_PAD, C)), const((LORA_PAD, C)),
                  const((LANES, LANES))],
        out_specs=[pl.BlockSpec((tm, C), lambda i: (i, 0))] * 7,
        compiler_params=_cparams(("parallel",)),
        name="rwkv_prep",
    )(P, P, P, P, P, P, P, P,
      consts["mu_r"], consts["mu_k"], consts["mu_v"], consts["mu_l"],
      consts["w0"], consts["a0"], consts["k_k"], consts["k_a"],
      consts["w2p"], consts["a2p"], consts["g2p"], consts["bd"])
    return outs


def _scan_kernel(r_ref, k_ref, v_ref, kk_ref, b_ref, lw_ref, g_ref,
                 rk_ref, lnw_ref, lnb_ref, o_ref, h_ref, *, nc, npair):
    L = CHUNK
    half = L // 2

    @pl.when(pl.program_id(2) == 0)
    def _():
        h_ref[...] = jnp.zeros_like(h_ref)

    ri = lax.broadcasted_iota(jnp.int32, (L, L), 0)
    ci = lax.broadcasted_iota(jnp.int32, (L, L), 1)
    lower_incl = ri >= ci
    lower_strict = ri > ci
    eye = ri == ci
    same_head = (ri < HEAD_DIM) == (ci < HEAD_DIM)
    tri01 = lower_incl.astype(BF16)
    eye_f = eye.astype(F32)
    lane = lax.broadcasted_iota(jnp.int32, (1, LANES), 1)
    head0 = lane < HEAD_DIM
    lane2 = lax.broadcasted_iota(jnp.int32, (1, 2 * LANES), 1)
    head0_2 = (lane2 % LANES) < HEAD_DIM
    tri_stack = jnp.concatenate([lower_strict, lower_incl], axis=0)
    chunks = range(nc * npair)
    pairs = [(c, h) for c in chunks for h in range(2)]

    def sel(x0, x1):
        return jnp.where(head0 if x0.shape[1] == LANES else head0_2, x0, x1)

    def lanes_of(u):
        return slice((u // nc) * LANES, (u // nc + 1) * LANES)

    def ld(ref, u):
        return ref[pl.ds((u % nc) * L, L), lanes_of(u)]

    r = [ld(r_ref, c) for c in chunks]
    k = [ld(k_ref, c) for c in chunks]
    v = [ld(v_ref, c) for c in chunks]
    kk = [ld(kk_ref, c) for c in chunks]
    b = [ld(b_ref, c) for c in chunks]
    lw = [ld(lw_ref, c) for c in chunks]

    cum = [_exact01_dot(tri01, lw[c]) for c in chunks]
    lhs_mid, rhs_mid, kd_bf, r_dec, k_end, b_end, p_end, v_bf = [], [], [], [], [], [], [], []
    for c in chunks:
        mid = cum[c][half - 1:half, :]
        end = cum[c][L - 1:L, :]
        r_mid = r[c] * jnp.exp(cum[c] - mid)
        kk_mid = kk[c] * jnp.exp(cum[c] - lw[c] - mid)
        e_back = jnp.exp(mid - cum[c])
        lhs_mid.append(jnp.concatenate([kk_mid, r_mid], axis=0))
        rhs_mid.append(jnp.concatenate([b[c] * e_back, k[c] * e_back], axis=0).astype(BF16))
        kd_bf.append((kk[c] * jnp.exp(cum[c] - lw[c])).astype(BF16))
        r_dec.append(r[c] * jnp.exp(cum[c]))
        e_end = jnp.exp(end - cum[c])
        k_end.append((k[c] * e_end).astype(BF16))
        b_end.append((b[c] * e_end).astype(BF16))
        p_end.append(jnp.exp(end))
        v_bf.append(v[c].astype(BF16))

    aa = {}
    for (c, h) in pairs:
        hm = head0 if h == 0 else jnp.logical_not(head0)
        aa[c, h] = _dot_nt(jnp.where(hm, lhs_mid[c], 0.0).astype(BF16), rhs_mid[c])
    def side(f):
        return jnp.concatenate([f(0), f(1)], axis=1)

    tri2 = jnp.concatenate([tri_stack, tri_stack], axis=1)
    strict2 = jnp.concatenate([lower_strict, lower_strict], axis=1)
    incl2 = jnp.concatenate([lower_incl, lower_incl], axis=1)
    eye2 = jnp.concatenate([eye_f, eye_f], axis=1)
    row2 = lax.broadcasted_iota(jnp.int32, (2 * L, 2 * L), 0) < L
    col2 = lax.broadcasted_iota(jnp.int32, (2 * L, 2 * L), 1) < L
    diag_blocks = row2 == col2

    def by_head_rows(x):
        hm = head0 if x.shape[1] == LANES else head0_2
        zero = jnp.zeros_like(x)
        return jnp.concatenate([jnp.where(hm, x, zero), jnp.where(hm, zero, x)], axis=0)

    def block_diag(x2):
        return jnp.where(diag_blocks, jnp.concatenate([x2, x2], axis=0), jnp.zeros((), x2.dtype))

    a_kb = [jnp.where(strict2, side(lambda h: aa[c, h][:L, :L]), 0.0) for c in chunks]
    a_rb = [jnp.where(incl2, side(lambda h: aa[c, h][L:, :L]), 0.0).astype(BF16) for c in chunks]
    a_xk = [jnp.where(tri2, side(lambda h: aa[c, h][:, L:]), 0.0).astype(BF16) for c in chunks]
    avr = [_dot(a_xk[c], by_head_rows(v_bf[c])) for c in chunks]

    n_lvl = int(math.log2(L)) - 1
    p, t = [], []
    for c in chunks:
        a_bf = a_kb[c].astype(BF16)
        p.append(_dot(a_bf, block_diag(a_bf)))
        t.append(eye2 - a_kb[c])
    for lvl in range(n_lvl):
        for c in chunks:
            p_bd = block_diag(p[c].astype(BF16))
            if lvl + 1 < n_lvl:
                tp = _dot(jnp.concatenate([t[c], p[c]], axis=0).astype(BF16), p_bd)
                t[c] = t[c] + tp[:L]
                p[c] = tp[L:]
            else:
                t[c] = t[c] + _dot(t[c].astype(BF16), p_bd)

    rq, y0, gmat, cmat = [], [], [], []
    for c in chunks:
        av = avr[c][:L]
        arkv = avr[c][L:]
        x = jnp.concatenate([kd_bf[c], av.astype(BF16)], axis=1)
        w_bf = _dot(t[c].astype(BF16), by_head_rows(x)).astype(BF16)
        aw = _dot(a_rb[c], by_head_rows(w_bf))
        bw = _dot_tn(b_end[c], w_bf)
        kv = _dot_tn(k_end[c], v_bf[c])
        rq.append(r_dec[c] - aw[:, :LANES])
        y0.append(arkv - aw[:, LANES:])
        gmat.append(eye_f * p_end[c] - jnp.where(same_head, bw[:, :LANES], 0.0))
        cmat.append(jnp.where(same_head, kv - bw[:, LANES:], 0.0))

    bd2 = ((lax.broadcasted_iota(jnp.int32, (2 * LANES, 2 * LANES), 0) // HEAD_DIM)
           == (lax.broadcasted_iota(jnp.int32, (2 * LANES, 2 * LANES), 1) // HEAD_DIM)).astype(BF16)

    def head_sum_all(x):
        n = x.shape[0] // 2
        s2 = _dot_exact01(jnp.concatenate([x[:n], x[n:]], axis=1), bd2)
        return jnp.concatenate([s2[:, :LANES], s2[:, LANES:]], axis=0)

    rkr = head_sum_all(jnp.concatenate(
        [r[u] * k[u] * rk_ref[:, lanes_of(u)] for u in chunks], axis=0))
    bonus = [rkr[u * L:(u + 1) * L] * v[u] for u in chunks]

    inv_n = 1.0 / HEAD_DIM
    hstate = [h_ref[q] for q in range(npair)]
    ys = [None] * len(chunks)
    for c in range(nc):
        for q in range(npair):
            u = q * nc + c
            res = _dot(jnp.concatenate([rq[u], gmat[u]], axis=0).astype(BF16),
                       hstate[q].astype(BF16))
            ys[u] = res[:L] + y0[u]
            hstate[q] = res[L:] + cmat[u]
    for q in range(npair):
        h_ref[q] = hstate[q]
    y = jnp.concatenate(ys, axis=0)
    mean = head_sum_all(y) * inv_n
    yc = y - mean
    var = head_sum_all(yc * yc) * inv_n
    yn = yc * lax.rsqrt(var + GN_EPS)
    for u in chunks:
        rows = pl.ds((u % nc) * L, L)
        o_ref[rows, lanes_of(u)] = (
            (yn[u * L:(u + 1) * L] * lnw_ref[:, lanes_of(u)] + lnb_ref[:, lanes_of(u)] + bonus[u])
            * g_ref[rows, lanes_of(u)])


def _scan(arrs, rk, lnw, lnb, batch, seq, nc=4, npair=4):
    r, k, v, kk, b, lw, g = [a.reshape(batch, seq, a.shape[1]) for a in arrs]
    C = r.shape[2]
    width = LANES * npair
    blk = pl.BlockSpec((None, CHUNK * nc, width), lambda bi, p, c: (bi, c, p))
    par = pl.BlockSpec((1, width), lambda bi, p, c: (0, p))
    out = pl.pallas_call(
        functools.partial(_scan_kernel, nc=nc, npair=npair),
        out_shape=jax.ShapeDtypeStruct((batch, seq, C), F32),
        grid=(batch, C // width, seq // (CHUNK * nc)),
        in_specs=[blk] * 7 + [par, par, par],
        out_specs=blk,
        scratch_shapes=[pltpu.VMEM((npair, LANES, LANES), F32)],
        compiler_params=_cparams(("parallel", "parallel", "arbitrary")),
        name="rwkv_scan",
    )(r, k, v, kk, b, lw, g, rk, lnw, lnb)
    return out.reshape(batch * seq, C)


def _sb_kernel(q_ref, k_ref, v_ref, o_ref, *, npair, nq):
    tq = tk = SB_BLOCK
    nu = SB_BLOCKS_PER_STEP
    i = pl.program_id(2)
    lane = lax.broadcasted_iota(jnp.int32, (1, LANES), 1)
    head0 = lane < HEAD_DIM
    scale = jnp.asarray(1.0 / math.sqrt(HEAD_DIM), BF16)
    streams = [(s, hp) for s in range(nq) for hp in range(npair)]
    ns = len(streams)
    q = [q_ref[s * tq:(s + 1) * tq, hp * LANES:(hp + 1) * LANES] * scale for (s, hp) in streams]
    r2 = lax.broadcasted_iota(jnp.int32, (2 * tk, 2 * tk), 0)
    c2 = lax.broadcasted_iota(jnp.int32, (2 * tk, 2 * tk), 1)
    after01 = jnp.logical_and(r2 > c2, (r2 < tk) == (c2 < tk)).astype(BF16)
    key_minus_query = (lax.broadcasted_iota(jnp.int32, (tq, 2 * tk), 1) % tk
                       - lax.broadcasted_iota(jnp.int32, (tq, 2 * tk), 0))
    units = [(p, u) for p in range(ns) for u in range(nu)]

    def by_head_rows(x):
        zero = jnp.zeros_like(x)
        return jnp.concatenate([jnp.where(head0, x, zero), jnp.where(head0, zero, x)], axis=0)

    def both(c0, c1):
        return jnp.concatenate([jnp.broadcast_to(c0, (tq, tk)), jnp.broadcast_to(c1, (tq, tk))],
                               axis=1)

    def body(carry):
        jt, _, offs, accs = carry
        z, sp, log_keep, mask, vb = {}, {}, {}, {}, {}
        for (p, u) in units:
            s, hp = streams[p]
            jb = jt - (nq - 1 - s) - u
            start = pl.multiple_of(jnp.maximum(jb, 0) * tk, tk)
            lanes = slice(hp * LANES, (hp + 1) * LANES)
            kb = by_head_rows(k_ref[pl.ds(start, tk), lanes])
            vb[p, u] = by_head_rows(v_ref[pl.ds(start, tk), lanes])
            zu = _dot_nt(q[p], kb)
            if u > 0:
                zu = zu + jnp.where(jb >= 0, 0.0, -1e30)
            spu = jnp.maximum(zu, 0.0) + jnp.log(1.0 + jnp.exp(-jnp.abs(zu)))
            z[p, u], sp[p, u] = zu, spu
            if u == 0:
                q_start = (i * nq + s) * tq
                mask[p, u] = key_minus_query < jnp.where(jb >= 0, q_start - jb * tk, -tq)
                log_keep[p, u] = jnp.where(mask[p, u], -spu, 0.0)
            else:
                log_keep[p, u] = -spu
        parts = []
        for key in units:
            parts.extend(_split2(log_keep[key]))
        loc = _dot(jnp.concatenate(parts, axis=0), after01)
        offs, accs = list(offs), list(accs)
        for n, (p, u) in enumerate(units):
            local = loc[2 * n * tq:(2 * n + 1) * tq] + loc[(2 * n + 1) * tq:(2 * n + 2) * tq]
            later = local + both(offs[2 * p], offs[2 * p + 1])
            att = jnp.exp(z[p, u] - sp[p, u] + later)
            if u == 0:
                att = jnp.where(mask[p, u], att, 0.0)
            accs[p] = accs[p] + _dot(att.astype(BF16), vb[p, u])
            offs[2 * p] = offs[2 * p] + (local[:, 0:1] + log_keep[p, u][:, 0:1])
            offs[2 * p + 1] = offs[2 * p + 1] + (local[:, tk:tk + 1] + log_keep[p, u][:, tk:tk + 1])
        top = offs[0]
        for o in offs[1:]:
            top = jnp.maximum(top, o)
        alive = (jnp.max(top) > -SB_UNDERFLOW).astype(jnp.int32)
        return jt - nu, alive, tuple(offs), tuple(accs)

    def cond(carry):
        jt, alive = carry[0], carry[1]
        return jnp.logical_and(jt >= 0, alive > 0)

    init = (i * nq + nq - 1, jnp.int32(1),
            tuple(jnp.zeros((tq, 1), F32) for _ in range(2 * ns)),
            tuple(jnp.zeros((tq, LANES), F32) for _ in range(ns)))
    accs = lax.while_loop(cond, body, init)[3]
    for p, (s, hp) in enumerate(streams):
        o_ref[s * tq:(s + 1) * tq, hp * LANES:(hp + 1) * LANES] = accs[p]


def _stick_breaking(qkv, width, npair=2, nq=2):
    B, S, _ = qkv.shape
    w = LANES * npair
    nblk = width // w
    rows = SB_BLOCK * nq
    out = pl.pallas_call(
        functools.partial(_sb_kernel, npair=npair, nq=nq),
        out_shape=jax.ShapeDtypeStruct((B, S, width), F32),
        grid=(B, nblk, S // rows),
        in_specs=[pl.BlockSpec((None, rows, w), lambda b, p, i: (b, i, p)),
                  pl.BlockSpec((None, S, w), lambda b, p, i: (b, 0, nblk + p)),
                  pl.BlockSpec((None, S, w), lambda b, p, i: (b, 0, 2 * nblk + p))],
        out_specs=pl.BlockSpec((None, rows, w), lambda b, p, i: (b, i, p)),
        compiler_params=_cparams(("parallel", "parallel", "arbitrary")),
        name="stick_breaking",
    )(qkv, qkv, qkv)
    return out.reshape(B * S, width)


def _merge_kernel(ya_ref, yb_ref, ga_ref, gb_ref, x_ref, mod_ref, g1_ref,
                  g2_ref, wua_ref, wub_ref, wo_ref, o_ref, hf_ref, *, d):
    ua = _dot(ya_ref[...].astype(BF16), wua_ref[...])
    ub = _dot(yb_ref[...].astype(BF16), wub_ref[...])
    merged = jax.nn.sigmoid(ga_ref[...]) * ua + jax.nn.sigmoid(gb_ref[...]) * ub
    mix = _dot(merged.astype(BF16), wo_ref[...])
    gate_m = mod_ref[:, 2 * d:3 * d]
    x1 = x_ref[...] + gate_m * _rms(mix, g1_ref[...])
    o_ref[...] = x1
    shift_f = mod_ref[:, 3 * d:4 * d]
    scale_f = mod_ref[:, 4 * d:5 * d]
    hf_ref[...] = (_rms(x1, g2_ref[...]) * (1.0 + scale_f) + shift_f).astype(BF16)


def _merge(ya, yb, P, x2, mod3, g1, g2, wua, wub, wo, seq, tm=256):
    T, D = x2.shape
    C = ya.shape[1]

    def const(shape):
        return pl.BlockSpec(shape, lambda i: (0, 0), pipeline_mode=pl.Buffered(1))

    return pl.pallas_call(
        functools.partial(_merge_kernel, d=D),
        out_shape=[jax.ShapeDtypeStruct((T, D), F32), jax.ShapeDtypeStruct((T, D), BF16)],
        grid=(T // tm,),
        in_specs=[pl.BlockSpec((tm, C), lambda i: (i, 0)),
                  pl.BlockSpec((tm, C), lambda i: (i, 0)),
                  pl.BlockSpec((tm, D), lambda i: (i, 0)),
                  pl.BlockSpec((tm, D), lambda i: (i, 1)),
                  pl.BlockSpec((tm, D), lambda i: (i, 0)),
                  pl.BlockSpec((None, 1, mod3.shape[2]), lambda i: ((i * tm) // seq, 0, 0)),
                  pl.BlockSpec((1, D), lambda i: (0, 0)),
                  pl.BlockSpec((1, D), lambda i: (0, 0)),
                  const((C, D)), const((C, D)), const((D, D))],
        out_specs=[pl.BlockSpec((tm, D), lambda i: (i, 0)), pl.BlockSpec((tm, D), lambda i: (i, 0))],
        compiler_params=_cparams(("parallel",)),
        name="merge_out",
    )(ya, yb, P, P, x2, mod3, g1, g2, wua, wub, wo)


def _mlp_kernel(x_ref, hf_ref, mod_ref, g3_ref, w1_ref, w2_ref, o_ref, acc_ref, *, d):
    j = pl.program_id(1)

    @pl.when(j == 0)
    def _():
        acc_ref[...] = jnp.zeros_like(acc_ref)

    u = jnp.maximum(_dot(hf_ref[...], w1_ref[...]), 0.0)
    acc_ref[...] += _dot((u * u).astype(BF16), w2_ref[...])

    @pl.when(j == pl.num_programs(1) - 1)
    def _():
        gate_f = mod_ref[:, 5 * d:6 * d]
        o_ref[...] = x_ref[...] + gate_f * _rms(acc_ref[...], g3_ref[...])


def _mlp(x2, hf, mod3, g3, w1, w2, seq, tm=512, tf=1024):
    T, D = x2.shape
    F = w1.shape[1]
    return pl.pallas_call(
        functools.partial(_mlp_kernel, d=D),
        out_shape=jax.ShapeDtypeStruct((T, D), F32),
        grid=(T // tm, F // tf),
        in_specs=[pl.BlockSpec((tm, D), lambda i, j: (i, 0)),
                  pl.BlockSpec((tm, D), lambda i, j: (i, 0)),
                  pl.BlockSpec((None, 1, mod3.shape[2]), lambda i, j: ((i * tm) // seq, 0, 0)),
                  pl.BlockSpec((1, D), lambda i, j: (0, 0)),
                  pl.BlockSpec((D, tf), lambda i, j: (0, j)),
                  pl.BlockSpec((tf, D), lambda i, j: (j, 0))],
        out_specs=pl.BlockSpec((tm, D), lambda i, j: (i, 0)),
        scratch_shapes=[pltpu.VMEM((tm, D), F32)],
        compiler_params=_cparams(("parallel", "arbitrary")),
        name="mlp",
    )(x2, hf, mod3, g3, w1, w2)


def _pad_rows(w, offset, total):
    return jnp.zeros((total, w.shape[1]), w.dtype).at[offset:offset + w.shape[0]].set(w)


def kernel(x, c, w_ada, b_ada, norm_g, w_in, mu_shift, w0, w2, a0, a2, g2, k_k, k_a,
           r_k, ln_x_w, ln_x_b, w_up_rwkv, w_up_sb, w_out, w_mlp_in, w_mlp_out):
    B, S, D = x.shape
    depth = w_in.shape[0]
    C = w0.shape[1]
    W = w_up_sb.shape[1]
    rwkv_cols = 3 * C + DECAY_LORA + ICLR_LORA + GATE_LORA
    sb_cols = 3 * W
    n_lora = DECAY_LORA + ICLR_LORA + GATE_LORA
    assert C == 1024 and W == 1024 and D == 2 * C and S % (2 * CHUNK) == 0

    hi = lax.broadcasted_iota(jnp.int32, (LANES, LANES), 0) // HEAD_DIM
    hj = lax.broadcasted_iota(jnp.int32, (LANES, LANES), 1) // HEAD_DIM
    bd = (hi == hj).astype(BF16)

    x2 = x.reshape(B * S, D)
    for l in range(depth):
        mod = _ada(c, w_ada[l], b_ada[l])
        mod3 = mod.reshape(B, 1, mod.shape[1])

        wl = w_in[l]
        g_off = rwkv_cols + sb_cols
        lora_w = jnp.pad(wl[:, 3 * C:rwkv_cols], ((0, 0), (0, LORA_PAD - n_lora)))
        w_perm = jnp.concatenate(
            [wl[:, g_off:], wl[:, :3 * C], lora_w, wl[:, rwkv_cols:g_off]], axis=1).astype(BF16)
        n_main_cols = 2 * D + 3 * C + LORA_PAD
        P, qkv = _inproj(x2, mod3, norm_g[l, 0].reshape(1, D), w_perm, n_main_cols, S)

        mu = mu_shift[l]
        consts = dict(
            col_r=2 * D // C, col_k=2 * D // C + 1, col_v=2 * D // C + 2,
            col_l=(2 * D + 3 * C) // LORA_PAD,
            mu_r=mu[None, :C], mu_k=mu[None, C:2 * C], mu_v=mu[None, 2 * C:3 * C],
            mu_l=jnp.pad(mu[None, 3 * C:], ((0, 0), (0, LORA_PAD - n_lora))),
            w0=w0[l][None], a0=a0[l][None], k_k=k_k[l][None], k_a=k_a[l][None],
            w2p=_pad_rows(w2[l], 0, LORA_PAD).astype(BF16),
            a2p=_pad_rows(a2[l], DECAY_LORA, LORA_PAD).astype(BF16),
            g2p=_pad_rows(g2[l], DECAY_LORA + ICLR_LORA, LORA_PAD).astype(BF16),
            bd=bd)
        prep = _prep(P, S, consts)
        ya = _scan(prep, r_k[l].reshape(1, C), ln_x_w[l][None], ln_x_b[l][None], B, S)

        yb = _stick_breaking(qkv.reshape(B, S, 3 * W), W)

        x2, hf = _merge(ya, yb, P, x2, mod3, norm_g[l, 1].reshape(1, D),
                        norm_g[l, 2].reshape(1, D), w_up_rwkv[l].astype(BF16),
                        w_up_sb[l].astype(BF16), w_out[l].astype(BF16), S)
        x2 = _mlp(x2, hf, mod3, norm_g[l, 3].reshape(1, D),
                  w_mlp_in[l].astype(BF16), w_mlp_out[l].astype(BF16), S)
    return x2.reshape(B, S, D)
```

```python
import functools
import math

import jax
import jax.numpy as jnp
from jax import lax
from jax.experimental import pallas as pl
from jax.experimental.pallas import tpu as pltpu

F32 = jnp.float32
BF16 = jnp.bfloat16

HEAD_DIM = 64
LANES = 128
DECAY_LORA = 64
ICLR_LORA = 64
GATE_LORA = 160
LORA_PAD = 512
NORM_EPS = 1e-6
GN_EPS = 64e-5
CHUNK = 128
SB_BLOCK = 128
SB_BLOCKS_PER_STEP = 3
SB_UNDERFLOW = 104.0
VMEM_LIMIT = 56 * 1024 * 1024


def _cparams(sem, vmem=VMEM_LIMIT):
    return pltpu.CompilerParams(dimension_semantics=sem, vmem_limit_bytes=vmem)


def _dot(a, b):
    return jnp.dot(a, b, preferred_element_type=F32)


def _dot_nt(a, b):
    return lax.dot_general(a, b, (((1,), (1,)), ((), ())), preferred_element_type=F32)


def _dot_tn(a, b):
    return lax.dot_general(a, b, (((0,), (0,)), ((), ())), preferred_element_type=F32)


def _split2(x):
    hi = x.astype(BF16)
    lo = (x - hi.astype(F32)).astype(BF16)
    return hi, lo


def _dot_exact01(x, m01):
    n = x.shape[0]
    res = _dot(jnp.concatenate(_split2(x), axis=0), m01)
    return res[:n] + res[n:]


def _exact01_dot(m01, x):
    n = x.shape[1]
    res = _dot(m01, jnp.concatenate(_split2(x), axis=1))
    return res[:, :n] + res[:, n:]


def _head_sum(x, bd):
    return _dot_exact01(x, bd)


def _rms(x, g):
    return x * lax.rsqrt(jnp.mean(x * x, axis=-1, keepdims=True) + NORM_EPS) * g


def _ada_kernel(c_ref, w_ref, b_ref, o_ref):
    c = c_ref[...]
    s = c * jax.nn.sigmoid(c)
    o_ref[...] = _dot(s.astype(BF16), w_ref[...].astype(BF16)) + b_ref[...]


def _ada(c, w_ada, b_ada, tn=1536):
    B, D = c.shape
    N = w_ada.shape[1]
    rows = 8
    c_pad = jnp.zeros((rows, D), F32).at[:B].set(c)
    out = pl.pallas_call(
        _ada_kernel,
        out_shape=jax.ShapeDtypeStruct((rows, N), F32),
        grid=(N // tn,),
        in_specs=[pl.BlockSpec((rows, D), lambda j: (0, 0)),
                  pl.BlockSpec((D, tn), lambda j: (0, j)),
                  pl.BlockSpec((1, tn), lambda j: (0, j))],
        out_specs=pl.BlockSpec((rows, tn), lambda j: (0, j)),
        compiler_params=_cparams(("parallel",)),
        name="ada",
    )(c_pad, w_ada, b_ada.reshape(1, N))
    return out[:B]


def _inproj_kernel(x_ref, mod_ref, g_ref, w_ref, o_ref, sb_ref, h_ref, *, d, n_main):
    j = pl.program_id(1)

    @pl.when(j == 0)
    def _():
        shift = mod_ref[:, 0:d]
        scale = mod_ref[:, d:2 * d]
        h = _rms(x_ref[...], g_ref[...]) * (1.0 + scale) + shift
        h_ref[...] = h.astype(BF16)

    @pl.when(j < n_main)
    def _():
        o_ref[...] = _dot_nt(h_ref[...], w_ref[...])

    @pl.when(j >= n_main)
    def _():
        sb_ref[...] = _dot_nt(h_ref[...], w_ref[...]).astype(BF16)


def _inproj(x2, mod3, g0, w_t, n_main_cols, seq, tm=1024, tn=768):
    T, D = x2.shape
    N = w_t.shape[0]
    n_main = n_main_cols // tn
    return pl.pallas_call(
        functools.partial(_inproj_kernel, d=D, n_main=n_main),
        out_shape=[jax.ShapeDtypeStruct((T, n_main_cols), F32),
                   jax.ShapeDtypeStruct((T, N - n_main_cols), BF16)],
        grid=(T // tm, N // tn),
        in_specs=[pl.BlockSpec((tm, D), lambda i, j: (i, 0)),
                  pl.BlockSpec((None, 1, mod3.shape[2]), lambda i, j: ((i * tm) // seq, 0, 0)),
                  pl.BlockSpec((1, D), lambda i, j: (0, 0)),
                  pl.BlockSpec((tn, D), lambda i, j: (j, 0))],
        out_specs=[pl.BlockSpec((tm, tn), lambda i, j: (i, jnp.minimum(j, n_main - 1))),
                   pl.BlockSpec((tm, tn), lambda i, j: (i, jnp.maximum(j - n_main, 0)))],
        scratch_shapes=[pltpu.VMEM((tm, D), BF16)],
        compiler_params=_cparams(("parallel", "arbitrary")),
        name="inproj",
    )(x2, mod3, g0, w_t)


def _token_shift_mix(cur_ref, prev_ref, mu_ref, first):
    cur = cur_ref[...]
    row = lax.broadcasted_iota(jnp.int32, (cur.shape[0], 1), 0)
    prev_row = jnp.where(first, 0.0, prev_ref[7:8, :])
    shifted = jnp.where(row == 0, prev_row, pltpu.roll(cur, 1, 0))
    return cur + (shifted - cur) * mu_ref[...]


def _prep_kernel(k_ref, l_ref, kp_ref, lp_ref, muk_ref, mul_ref, w0_ref, a0_ref, kk_ref, ka_ref,
                 w2_ref, a2_ref, g2_ref, bd_ref,
                 ko_ref, kko_ref, bo_ref, lwo_ref, go_ref, *, tm, seq):
    first = (pl.program_id(0) * tm) % seq == 0
    k = _token_shift_mix(k_ref, kp_ref, muk_ref, first)
    lo = _token_shift_mix(l_ref, lp_ref, mul_ref, first)

    dw = _dot(jnp.tanh(lo).astype(BF16), w2_ref[...])
    da = _dot(lo.astype(BF16), a2_ref[...])
    g = _dot(jax.nn.sigmoid(lo).astype(BF16), g2_ref[...])

    w_log = -jax.nn.softplus(-(w0_ref[...] + dw)) - 0.5
    lw = -jnp.exp(w_log)
    a = jax.nn.sigmoid(a0_ref[...] + da)
    kx = k * kk_ref[...]
    bd = bd_ref[...]
    n_blk = kx.shape[1] // LANES
    sq = kx * kx
    ssum = jnp.concatenate(
        [_head_sum(sq[:, c * LANES:(c + 1) * LANES], bd) for c in range(n_blk)], axis=1)
    kk = kx * lax.rsqrt(jnp.maximum(ssum, 1e-24))
    k2 = k * (1.0 + (a - 1.0) * ka_ref[...])

    ko_ref[...] = k2
    kko_ref[...] = kk
    bo_ref[...] = kk * a
    lwo_ref[...] = lw
    go_ref[...] = g


def _prep(P, seq, consts, tm=256):
    T = P.shape[0]
    C = consts["w0"].shape[1]
    col_k, col_l = consts["col_k"], consts["col_l"]

    def cur(col, w):
        return pl.BlockSpec((tm, w), lambda i: (i, col))

    def prev(col, w):
        return pl.BlockSpec((8, w), lambda i: (jnp.maximum(i * (tm // 8) - 1, 0), col))

    def const(shape):
        return pl.BlockSpec(shape, lambda i: (0, 0))

    out_sd = jax.ShapeDtypeStruct((T, C), F32)
    outs = pl.pallas_call(
        functools.partial(_prep_kernel, tm=tm, seq=seq),
        out_shape=[out_sd] * 5,
        grid=(T // tm,),
        in_specs=[cur(col_k, C), cur(col_l, LORA_PAD), prev(col_k, C), prev(col_l, LORA_PAD),
                  const((1, C)), const((1, LORA_PAD)),
                  const((1, C)), const((1, C)), const((1, C)), const((1, C)),
                  const((LORA_PAD, C)), const((LORA_PAD, C)), const((LORA_PAD, C)),
                  const((LANES, LANES))],
        out_specs=[pl.BlockSpec((tm, C), lambda i: (i, 0))] * 5,
        compiler_params=_cparams(("parallel",)),
        name="rwkv_prep",
    )(P, P, P, P, consts["mu_k"], consts["mu_l"],
      consts["w0"], consts["a0"], consts["k_k"], consts["k_a"],
      consts["w2p"], consts["a2p"], consts["g2p"], consts["bd"])
    return outs


def _scan_kernel(r_ref, v_ref, rp_ref, vp_ref, mur_ref, muv_ref, k_ref, kk_ref, b_ref, lw_ref,
                 g_ref, rk_ref, lnw_ref, lnb_ref, o_ref, h_ref, *, nc, npair):
    L = CHUNK
    half = L // 2

    @pl.when(pl.program_id(2) == 0)
    def _():
        h_ref[...] = jnp.zeros_like(h_ref)

    ri = lax.broadcasted_iota(jnp.int32, (L, L), 0)
    ci = lax.broadcasted_iota(jnp.int32, (L, L), 1)
    lower_incl = ri >= ci
    lower_strict = ri > ci
    eye = ri == ci
    same_head = (ri < HEAD_DIM) == (ci < HEAD_DIM)
    tri01 = lower_incl.astype(BF16)
    eye_f = eye.astype(F32)
    lane = lax.broadcasted_iota(jnp.int32, (1, LANES), 1)
    head0 = lane < HEAD_DIM
    lane2 = lax.broadcasted_iota(jnp.int32, (1, 2 * LANES), 1)
    head0_2 = (lane2 % LANES) < HEAD_DIM
    tri_stack = jnp.concatenate([lower_strict, lower_incl], axis=0)
    chunks = range(nc * npair)
    pairs = [(c, h) for c in chunks for h in range(2)]

    def sel(x0, x1):
        return jnp.where(head0 if x0.shape[1] == LANES else head0_2, x0, x1)

    def lanes_of(u):
        return slice((u // nc) * LANES, (u // nc + 1) * LANES)

    def ld(ref, u):
        return ref[pl.ds((u % nc) * L, L), lanes_of(u)]

    first = pl.program_id(2) == 0
    r_tile = _token_shift_mix(r_ref, rp_ref, mur_ref, first)
    v_tile = _token_shift_mix(v_ref, vp_ref, muv_ref, first)
    r = [r_tile[(u % nc) * L:(u % nc + 1) * L, lanes_of(u)] for u in chunks]
    v = [v_tile[(u % nc) * L:(u % nc + 1) * L, lanes_of(u)] for u in chunks]
    k = [ld(k_ref, c) for c in chunks]
    kk = [ld(kk_ref, c) for c in chunks]
    b = [ld(b_ref, c) for c in chunks]
    lw = [ld(lw_ref, c) for c in chunks]

    cum = [_exact01_dot(tri01, lw[c]) for c in chunks]
    lhs_mid, rhs_mid, kd_bf, r_dec, k_end, b_end, p_end, v_bf = [], [], [], [], [], [], [], []
    for c in chunks:
        mid = cum[c][half - 1:half, :]
        end = cum[c][L - 1:L, :]
        r_mid = r[c] * jnp.exp(cum[c] - mid)
        kk_mid = kk[c] * jnp.exp(cum[c] - lw[c] - mid)
        e_back = jnp.exp(mid - cum[c])
        lhs_mid.append(jnp.concatenate([kk_mid, r_mid], axis=0))
        rhs_mid.append(jnp.concatenate([b[c] * e_back, k[c] * e_back], axis=0).astype(BF16))
        kd_bf.append((kk[c] * jnp.exp(cum[c] - lw[c])).astype(BF16))
        r_dec.append(r[c] * jnp.exp(cum[c]))
        e_end = jnp.exp(end - cum[c])
        k_end.append((k[c] * e_end).astype(BF16))
        b_end.append((b[c] * e_end).astype(BF16))
        p_end.append(jnp.exp(end))
        v_bf.append(v[c].astype(BF16))

    aa = {}
    for (c, h) in pairs:
        hm = head0 if h == 0 else jnp.logical_not(head0)
        aa[c, h] = _dot_nt(jnp.where(hm, lhs_mid[c], 0.0).astype(BF16), rhs_mid[c])
    def side(f):
        return jnp.concatenate([f(0), f(1)], axis=1)

    tri2 = jnp.concatenate([tri_stack, tri_stack], axis=1)
    strict2 = jnp.concatenate([lower_strict, lower_strict], axis=1)
    incl2 = jnp.concatenate([lower_incl, lower_incl], axis=1)
    def by_head_rows(x):
        hm = head0 if x.shape[1] == LANES else head0_2
        zero = jnp.zeros_like(x)
        return jnp.concatenate([jnp.where(hm, x, zero), jnp.where(hm, zero, x)], axis=0)

    a_kb = [jnp.where(strict2, side(lambda h: aa[c, h][:L, :L]), 0.0) for c in chunks]
    a_rb = [jnp.where(incl2, side(lambda h: aa[c, h][L:, :L]), 0.0).astype(BF16) for c in chunks]
    a_xk = [jnp.where(tri2, side(lambda h: aa[c, h][:, L:]), 0.0).astype(BF16) for c in chunks]
    avr = [_dot(a_xk[c], by_head_rows(v_bf[c])) for c in chunks]

    hl = L // 2
    blk_r = lax.broadcasted_iota(jnp.int32, (2 * L, 2 * L), 0) // hl
    blk_c = lax.broadcasted_iota(jnp.int32, (2 * L, 2 * L), 1) // hl
    diag4 = blk_r == blk_c
    below4 = jnp.logical_and(blk_r == blk_c + 1, blk_c % 2 == 0)
    eye4 = (lax.broadcasted_iota(jnp.int32, (hl, 2 * L), 0)
            == lax.broadcasted_iota(jnp.int32, (hl, 2 * L), 1) % hl).astype(F32)

    def tile4(x, keep):
        return jnp.where(keep, jnp.concatenate([x, x, x, x], axis=0), jnp.zeros((), x.dtype))

    n_lvl = int(math.log2(hl)) - 1
    p, t, a21 = [], [], []
    for c in chunks:
        pd = jnp.where(head0_2, a_kb[c][:hl], a_kb[c][hl:])
        a21.append(jnp.where(head0_2, a_kb[c][hl:], 0.0).astype(BF16))
        pd_bf = pd.astype(BF16)
        p.append(_dot(pd_bf, tile4(pd_bf, diag4)))
        t.append(eye4 - pd)
    for lvl in range(n_lvl):
        for c in chunks:
            p_bd = tile4(p[c].astype(BF16), diag4)
            if lvl + 1 < n_lvl:
                tp = _dot(jnp.concatenate([t[c], p[c]], axis=0).astype(BF16), p_bd)
                t[c] = t[c] + tp[:hl]
                p[c] = tp[hl:]
            else:
                t[c] = t[c] + _dot(t[c].astype(BF16), p_bd)
    for c in chunks:
        td_bf = t[c].astype(BF16)
        y = _dot(a21[c], tile4(jnp.where(head0_2, td_bf, jnp.zeros((), BF16)), diag4))
        z = _dot(td_bf, tile4(y.astype(BF16), below4))
        t[c] = jnp.concatenate([jnp.where(head0_2, t[c], 0.0), jnp.where(head0_2, -z, t[c])], axis=0)

    rq, y0, gmat, cmat = [], [], [], []
    for c in chunks:
        av = avr[c][:L]
        arkv = avr[c][L:]
        x = jnp.concatenate([kd_bf[c], av.astype(BF16)], axis=1)
        w_bf = _dot(t[c].astype(BF16), by_head_rows(x)).astype(BF16)
        aw = _dot(a_rb[c], by_head_rows(w_bf))
        bw = _dot_tn(b_end[c], w_bf)
        kv = _dot_tn(k_end[c], v_bf[c])
        rq.append(r_dec[c] - aw[:, :LANES])
        y0.append(arkv - aw[:, LANES:])
        gmat.append(eye_f * p_end[c] - jnp.where(same_head, bw[:, :LANES], 0.0))
        cmat.append(jnp.where(same_head, kv - bw[:, LANES:], 0.0))

    bd2 = ((lax.broadcasted_iota(jnp.int32, (2 * LANES, 2 * LANES), 0) // HEAD_DIM)
           == (lax.broadcasted_iota(jnp.int32, (2 * LANES, 2 * LANES), 1) // HEAD_DIM)).astype(BF16)

    def head_sum_all(x):
        n = x.shape[0] // 2
        s2 = _dot_exact01(jnp.concatenate([x[:n], x[n:]], axis=1), bd2)
        return jnp.concatenate([s2[:, :LANES], s2[:, LANES:]], axis=0)

    rkr = head_sum_all(jnp.concatenate(
        [r[u] * k[u] * rk_ref[:, lanes_of(u)] for u in chunks], axis=0))
    bonus = [rkr[u * L:(u + 1) * L] * v[u] for u in chunks]

    inv_n = 1.0 / HEAD_DIM
    hstate = [h_ref[q] for q in range(npair)]
    ys = [None] * len(chunks)
    for c in range(nc):
        for q in range(npair):
            u = q * nc + c
            res = _dot(jnp.concatenate([rq[u], gmat[u]], axis=0).astype(BF16),
                       hstate[q].astype(BF16))
            ys[u] = res[:L] + y0[u]
            hstate[q] = res[L:] + cmat[u]
    for q in range(npair):
        h_ref[q] = hstate[q]
    y = jnp.concatenate(ys, axis=0)
    mean = head_sum_all(y) * inv_n
    yc = y - mean
    var = head_sum_all(yc * yc) * inv_n
    yn = yc * lax.rsqrt(var + GN_EPS)
    for u in chunks:
        rows = pl.ds((u % nc) * L, L)
        o_ref[rows, lanes_of(u)] = (
            (yn[u * L:(u + 1) * L] * lnw_ref[:, lanes_of(u)] + lnb_ref[:, lanes_of(u)] + bonus[u])
            * g_ref[rows, lanes_of(u)]).astype(BF16)


def _scan(P, col_r, col_v, mu_r, mu_v, arrs, rk, lnw, lnb, batch, seq, nc=4, npair=4):
    k, kk, b, lw, g = [a.reshape(batch, seq, a.shape[1]) for a in arrs]
    P3 = P.reshape(batch, seq, P.shape[1])
    C = k.shape[2]
    width = LANES * npair
    rows = CHUNK * nc
    blk = pl.BlockSpec((None, rows, width), lambda bi, p, c: (bi, c, p))
    par = pl.BlockSpec((1, width), lambda bi, p, c: (0, p))

    def raw(col):
        return pl.BlockSpec((None, rows, width), lambda bi, p, c: (bi, c, col // width + p))

    def raw_prev(col):
        return pl.BlockSpec((None, 8, width),
                            lambda bi, p, c: (bi, jnp.maximum(c * (rows // 8) - 1, 0),
                                              col // width + p))

    out = pl.pallas_call(
        functools.partial(_scan_kernel, nc=nc, npair=npair),
        out_shape=jax.ShapeDtypeStruct((batch, seq, C), BF16),
        grid=(batch, C // width, seq // rows),
        in_specs=[raw(col_r), raw(col_v), raw_prev(col_r), raw_prev(col_v), par, par]
                 + [blk] * 5 + [par, par, par],
        out_specs=blk,
        scratch_shapes=[pltpu.VMEM((npair, LANES, LANES), F32)],
        compiler_params=_cparams(("parallel", "parallel", "arbitrary")),
        name="rwkv_scan",
    )(P3, P3, P3, P3, mu_r, mu_v, k, kk, b, lw, g, rk, lnw, lnb)
    return out.reshape(batch * seq, C)


def _sb_kernel(q_ref, k_ref, v_ref, o_ref, *, npair, nq):
    tq = tk = SB_BLOCK
    nu = SB_BLOCKS_PER_STEP
    i = pl.program_id(2)
    lane = lax.broadcasted_iota(jnp.int32, (1, LANES), 1)
    head0 = lane < HEAD_DIM
    scale = jnp.asarray(1.0 / math.sqrt(HEAD_DIM), BF16)
    streams = [(s, hp) for s in range(nq) for hp in range(npair)]
    ns = len(streams)
    q = [q_ref[s * tq:(s + 1) * tq, hp * LANES:(hp + 1) * LANES] * scale for (s, hp) in streams]
    r2 = lax.broadcasted_iota(jnp.int32, (2 * tk, 2 * tk), 0)
    c2 = lax.broadcasted_iota(jnp.int32, (2 * tk, 2 * tk), 1)
    after01 = jnp.logical_and(r2 > c2, (r2 < tk) == (c2 < tk)).astype(BF16)
    key_minus_query = (lax.broadcasted_iota(jnp.int32, (tq, 2 * tk), 1) % tk
                       - lax.broadcasted_iota(jnp.int32, (tq, 2 * tk), 0))
    units = [(p, u) for p in range(ns) for u in range(nu)]

    def by_head_rows(x):
        zero = jnp.zeros_like(x)
        return jnp.concatenate([jnp.where(head0, x, zero), jnp.where(head0, zero, x)], axis=0)

    def both(c0, c1):
        return jnp.concatenate([jnp.broadcast_to(c0, (tq, tk)), jnp.broadcast_to(c1, (tq, tk))],
                               axis=1)

    def body(carry):
        jt, _, offs, accs = carry
        z, sp, log_keep, mask, vb = {}, {}, {}, {}, {}
        for (p, u) in units:
            s, hp = streams[p]
            jb = jt - (nq - 1 - s) - u
            start = pl.multiple_of(jnp.maximum(jb, 0) * tk, tk)
            lanes = slice(hp * LANES, (hp + 1) * LANES)
            kb = by_head_rows(k_ref[pl.ds(start, tk), lanes])
            vb[p, u] = by_head_rows(v_ref[pl.ds(start, tk), lanes])
            zu = _dot_nt(q[p], kb)
            if u > 0:
                zu = zu + jnp.where(jb >= 0, 0.0, -1e30)
            spu = jnp.maximum(zu, 0.0) + jnp.log(1.0 + jnp.exp(-jnp.abs(zu)))
            z[p, u], sp[p, u] = zu, spu
            if u == 0:
                q_start = (i * nq + s) * tq
                mask[p, u] = key_minus_query < jnp.where(jb >= 0, q_start - jb * tk, -tq)
                log_keep[p, u] = jnp.where(mask[p, u], -spu, 0.0)
            else:
                log_keep[p, u] = -spu
        parts = []
        for key in units:
            parts.extend(_split2(log_keep[key]))
        loc = _dot(jnp.concatenate(parts, axis=0), after01)
        offs, accs = list(offs), list(accs)
        for n, (p, u) in enumerate(units):
            local = loc[2 * n * tq:(2 * n + 1) * tq] + loc[(2 * n + 1) * tq:(2 * n + 2) * tq]
            later = local + both(offs[2 * p], offs[2 * p + 1])
            att = jnp.exp(z[p, u] - sp[p, u] + later)
            if u == 0:
                att = jnp.where(mask[p, u], att, 0.0)
            accs[p] = accs[p] + _dot(att.astype(BF16), vb[p, u])
            offs[2 * p] = offs[2 * p] + (local[:, 0:1] + log_keep[p, u][:, 0:1])
            offs[2 * p + 1] = offs[2 * p + 1] + (local[:, tk:tk + 1] + log_keep[p, u][:, tk:tk + 1])
        top = offs[0]
        for o in offs[1:]:
            top = jnp.maximum(top, o)
        alive = (jnp.max(top) > -SB_UNDERFLOW).astype(jnp.int32)
        return jt - nu, alive, tuple(offs), tuple(accs)

    def cond(carry):
        jt, alive = carry[0], carry[1]
        return jnp.logical_and(jt >= 0, alive > 0)

    init = (i * nq + nq - 1, jnp.int32(1),
            tuple(jnp.zeros((tq, 1), F32) for _ in range(2 * ns)),
            tuple(jnp.zeros((tq, LANES), F32) for _ in range(ns)))
    accs = lax.while_loop(cond, body, init)[3]
    for p, (s, hp) in enumerate(streams):
        o_ref[s * tq:(s + 1) * tq, hp * LANES:(hp + 1) * LANES] = accs[p].astype(BF16)


def _stick_breaking(qkv, width, npair=2, nq=2):
    B, S, _ = qkv.shape
    w = LANES * npair
    nblk = width // w
    rows = SB_BLOCK * nq
    out = pl.pallas_call(
        functools.partial(_sb_kernel, npair=npair, nq=nq),
        out_shape=jax.ShapeDtypeStruct((B, S, width), BF16),
        grid=(B, nblk, S // rows),
        in_specs=[pl.BlockSpec((None, rows, w), lambda b, p, i: (b, i, p)),
                  pl.BlockSpec((None, S, w), lambda b, p, i: (b, 0, nblk + p)),
                  pl.BlockSpec((None, S, w), lambda b, p, i: (b, 0, 2 * nblk + p))],
        out_specs=pl.BlockSpec((None, rows, w), lambda b, p, i: (b, i, p)),
        compiler_params=_cparams(("parallel", "parallel", "arbitrary")),
        name="stick_breaking",
    )(qkv, qkv, qkv)
    return out.reshape(B * S, width)


def _merge_kernel(ya_ref, yb_ref, ga_ref, gb_ref, x_ref, mod_ref, g1_ref,
                  g2_ref, wua_ref, wub_ref, wo_ref, o_ref, hf_ref, *, d):
    ua = _dot(ya_ref[...], wua_ref[...])
    ub = _dot(yb_ref[...], wub_ref[...])
    merged = jax.nn.sigmoid(ga_ref[...]) * ua + jax.nn.sigmoid(gb_ref[...]) * ub
    mix = _dot(merged.astype(BF16), wo_ref[...])
    gate_m = mod_ref[:, 2 * d:3 * d]
    x1 = x_ref[...] + gate_m * _rms(mix, g1_ref[...])
    o_ref[...] = x1
    shift_f = mod_ref[:, 3 * d:4 * d]
    scale_f = mod_ref[:, 4 * d:5 * d]
    hf_ref[...] = (_rms(x1, g2_ref[...]) * (1.0 + scale_f) + shift_f).astype(BF16)


def _merge(ya, yb, P, x2, mod3, g1, g2, wua, wub, wo, seq, tm=256):
    T, D = x2.shape
    C = ya.shape[1]

    def const(shape):
        return pl.BlockSpec(shape, lambda i: (0, 0), pipeline_mode=pl.Buffered(1))

    return pl.pallas_call(
        functools.partial(_merge_kernel, d=D),
        out_shape=[jax.ShapeDtypeStruct((T, D), F32), jax.ShapeDtypeStruct((T, D), BF16)],
        grid=(T // tm,),
        in_specs=[pl.BlockSpec((tm, C), lambda i: (i, 0)),
                  pl.BlockSpec((tm, C), lambda i: (i, 0)),
                  pl.BlockSpec((tm, D), lambda i: (i, 0)),
                  pl.BlockSpec((tm, D), lambda i: (i, 1)),
                  pl.BlockSpec((tm, D), lambda i: (i, 0)),
                  pl.BlockSpec((None, 1, mod3.shape[2]), lambda i: ((i * tm) // seq, 0, 0)),
                  pl.BlockSpec((1, D), lambda i: (0, 0)),
                  pl.BlockSpec((1, D), lambda i: (0, 0)),
                  const((C, D)), const((C, D)), const((D, D))],
        out_specs=[pl.BlockSpec((tm, D), lambda i: (i, 0)), pl.BlockSpec((tm, D), lambda i: (i, 0))],
        compiler_params=_cparams(("parallel",)),
        name="merge_out",
    )(ya, yb, P, P, x2, mod3, g1, g2, wua, wub, wo)


def _mlp_kernel(x_ref, hf_ref, mod_ref, g3_ref, w1_ref, w2_ref, o_ref, acc_ref, *, d):
    j = pl.program_id(1)

    @pl.when(j == 0)
    def _():
        acc_ref[...] = jnp.zeros_like(acc_ref)

    u = jnp.maximum(_dot(hf_ref[...], w1_ref[...]), 0.0)
    acc_ref[...] += _dot((u * u).astype(BF16), w2_ref[...])

    @pl.when(j == pl.num_programs(1) - 1)
    def _():
        gate_f = mod_ref[:, 5 * d:6 * d]
        o_ref[...] = x_ref[...] + gate_f * _rms(acc_ref[...], g3_ref[...])


def _mlp(x2, hf, mod3, g3, w1, w2, seq, tm=512, tf=1024):
    T, D = x2.shape
    F = w1.shape[1]
    return pl.pallas_call(
        functools.partial(_mlp_kernel, d=D),
        out_shape=jax.ShapeDtypeStruct((T, D), F32),
        grid=(T // tm, F // tf),
        in_specs=[pl.BlockSpec((tm, D), lambda i, j: (i, 0)),
                  pl.BlockSpec((tm, D), lambda i, j: (i, 0)),
                  pl.BlockSpec((None, 1, mod3.shape[2]), lambda i, j: ((i * tm) // seq, 0, 0)),
                  pl.BlockSpec((1, D), lambda i, j: (0, 0)),
                  pl.BlockSpec((D, tf), lambda i, j: (0, j)),
                  pl.BlockSpec((tf, D), lambda i, j: (j, 0))],
        out_specs=pl.BlockSpec((tm, D), lambda i, j: (i, 0)),
        scratch_shapes=[pltpu.VMEM((tm, D), F32)],
        compiler_params=_cparams(("parallel", "arbitrary")),
        name="mlp",
    )(x2, hf, mod3, g3, w1, w2)


def _pad_rows(w, offset, total):
    return jnp.zeros((total, w.shape[1]), w.dtype).at[offset:offset + w.shape[0]].set(w)


def kernel(x, c, w_ada, b_ada, norm_g, w_in, mu_shift, w0, w2, a0, a2, g2, k_k, k_a,
           r_k, ln_x_w, ln_x_b, w_up_rwkv, w_up_sb, w_out, w_mlp_in, w_mlp_out):
    B, S, D = x.shape
    depth = w_in.shape[0]
    C = w0.shape[1]
    W = w_up_sb.shape[1]
    rwkv_cols = 3 * C + DECAY_LORA + ICLR_LORA + GATE_LORA
    sb_cols = 3 * W
    n_lora = DECAY_LORA + ICLR_LORA + GATE_LORA
    assert C == 1024 and W == 1024 and D == 2 * C and S % (2 * CHUNK) == 0

    hi = lax.broadcasted_iota(jnp.int32, (LANES, LANES), 0) // HEAD_DIM
    hj = lax.broadcasted_iota(jnp.int32, (LANES, LANES), 1) // HEAD_DIM
    bd = (hi == hj).astype(BF16)

    x2 = x.reshape(B * S, D)
    for l in range(depth):
        mod = _ada(c, w_ada[l], b_ada[l])
        mod3 = mod.reshape(B, 1, mod.shape[1])

        wt = w_in[l].T
        g_off = rwkv_cols + sb_cols
        lora_w = jnp.pad(wt[3 * C:rwkv_cols], ((0, LORA_PAD - n_lora), (0, 0)))
        w_perm = jnp.concatenate(
            [wt[g_off:], wt[:3 * C], lora_w, wt[rwkv_cols:g_off]], axis=0).astype(BF16)
        n_main_cols = 2 * D + 3 * C + LORA_PAD
        P, qkv = _inproj(x2, mod3, norm_g[l, 0].reshape(1, D), w_perm, n_main_cols, S)

        mu = mu_shift[l]
        consts = dict(
            col_k=2 * D // C + 1, col_l=(2 * D + 3 * C) // LORA_PAD,
            mu_k=mu[None, C:2 * C],
            mu_l=jnp.pad(mu[None, 3 * C:], ((0, 0), (0, LORA_PAD - n_lora))),
            w0=w0[l][None], a0=a0[l][None], k_k=k_k[l][None], k_a=k_a[l][None],
            w2p=_pad_rows(w2[l], 0, LORA_PAD).astype(BF16),
            a2p=_pad_rows(a2[l], DECAY_LORA, LORA_PAD).astype(BF16),
            g2p=_pad_rows(g2[l], DECAY_LORA + ICLR_LORA, LORA_PAD).astype(BF16),
            bd=bd)
        prep = _prep(P, S, consts)
        ya = _scan(P, 2 * D, 2 * D + 2 * C, mu[None, :C], mu[None, 2 * C:3 * C], prep,
                   r_k[l].reshape(1, C), ln_x_w[l][None], ln_x_b[l][None], B, S)

        yb = _stick_breaking(qkv.reshape(B, S, 3 * W), W)

        x2, hf = _merge(ya, yb, P, x2, mod3, norm_g[l, 1].reshape(1, D),
                        norm_g[l, 2].reshape(1, D), w_up_rwkv[l].astype(BF16),
                        w_up_sb[l].astype(BF16), w_out[l].astype(BF16), S)
        x2 = _mlp(x2, hf, mod3, norm_g[l, 3].reshape(1, D),
                  w_mlp_in[l].astype(BF16), w_mlp_out[l].astype(BF16), S)
    return x2.reshape(B, S, D)
```

```python
import functools
import math

import jax
import jax.numpy as jnp
from jax import lax
from jax.experimental import pallas as pl
from jax.experimental.pallas import tpu as pltpu

F32 = jnp.float32
BF16 = jnp.bfloat16

HEAD_DIM = 64
LANES = 128
DECAY_LORA = 64
ICLR_LORA = 64
GATE_LORA = 160
LORA_PAD = 512
NORM_EPS = 1e-6
GN_EPS = 64e-5
CHUNK = 128
SB_BLOCK = 128
SB_BLOCKS_PER_STEP = 3
SB_UNDERFLOW = 104.0
VMEM_LIMIT = 56 * 1024 * 1024


def _cparams(sem, vmem=VMEM_LIMIT):
    return pltpu.CompilerParams(dimension_semantics=sem, vmem_limit_bytes=vmem)


def _dot(a, b):
    return jnp.dot(a, b, preferred_element_type=F32)


def _dot_nt(a, b):
    return lax.dot_general(a, b, (((1,), (1,)), ((), ())), preferred_element_type=F32)


def _dot_tn(a, b):
    return lax.dot_general(a, b, (((0,), (0,)), ((), ())), preferred_element_type=F32)


def _split2(x):
    hi = x.astype(BF16)
    lo = (x - hi.astype(F32)).astype(BF16)
    return hi, lo


def _dot_exact01(x, m01):
    n = x.shape[0]
    res = _dot(jnp.concatenate(_split2(x), axis=0), m01)
    return res[:n] + res[n:]


def _exact01_dot(m01, x):
    n = x.shape[1]
    res = _dot(m01, jnp.concatenate(_split2(x), axis=1))
    return res[:, :n] + res[:, n:]


def _head_sum(x, bd):
    return _dot_exact01(x, bd)


def _rms(x, g):
    return x * lax.rsqrt(jnp.mean(x * x, axis=-1, keepdims=True) + NORM_EPS) * g


def _ada_kernel(c_ref, w_ref, b_ref, o_ref):
    c = c_ref[...]
    s = c * jax.nn.sigmoid(c)
    o_ref[...] = _dot(s.astype(BF16), w_ref[...].astype(BF16)) + b_ref[...]


def _ada(c, w_ada, b_ada, tn=1536):
    B, D = c.shape
    N = w_ada.shape[1]
    rows = 8
    c_pad = jnp.zeros((rows, D), F32).at[:B].set(c)
    out = pl.pallas_call(
        _ada_kernel,
        out_shape=jax.ShapeDtypeStruct((rows, N), F32),
        grid=(N // tn,),
        in_specs=[pl.BlockSpec((rows, D), lambda j: (0, 0)),
                  pl.BlockSpec((D, tn), lambda j: (0, j)),
                  pl.BlockSpec((1, tn), lambda j: (0, j))],
        out_specs=pl.BlockSpec((rows, tn), lambda j: (0, j)),
        compiler_params=_cparams(("parallel",)),
        name="ada",
    )(c_pad, w_ada, b_ada.reshape(1, N))
    return out[:B]


def _regroup_kernel(offs_ref, w_ref, o_ref, *, zero_lo, zero_hi, rows):
    del offs_ref
    out_row = pl.program_id(0) * rows + lax.broadcasted_iota(jnp.int32, (rows, 1), 0)
    pad = jnp.logical_and(out_row >= zero_lo, out_row < zero_hi)
    o_ref[...] = jnp.where(pad, 0.0, w_ref[...]).astype(BF16)


def _regroup_rows(wt, groups, zero_lo, zero_hi, rows=256):
    D = wt.shape[1]
    sub = 8
    assert all(s % sub == 0 and n % rows == 0 for s, n in groups)
    offs = jnp.asarray([(s + r) // sub for (s, n) in groups for r in range(0, n, rows)], jnp.int32)
    n_out = sum(n for _, n in groups)
    return pl.pallas_call(
        functools.partial(_regroup_kernel, zero_lo=zero_lo, zero_hi=zero_hi, rows=rows),
        out_shape=jax.ShapeDtypeStruct((n_out, D), BF16),
        grid_spec=pltpu.PrefetchScalarGridSpec(
            num_scalar_prefetch=1, grid=(n_out // rows,),
            in_specs=[pl.BlockSpec((pl.Element(rows), pl.Element(D)),
                                   lambda t, offs: (offs[t] * sub, 0))],
            out_specs=pl.BlockSpec((rows, D), lambda t, offs: (t, 0))),
        compiler_params=_cparams(("parallel",)),
        name="regroup_w_in",
    )(offs, wt)


def _inproj_kernel(x_ref, mod_ref, g_ref, w_ref, o_ref, sb_ref, h_ref, *, d, n_main):
    j = pl.program_id(1)

    @pl.when(j == 0)
    def _():
        shift = mod_ref[:, 0:d]
        scale = mod_ref[:, d:2 * d]
        h = _rms(x_ref[...], g_ref[...]) * (1.0 + scale) + shift
        h_ref[...] = h.astype(BF16)

    @pl.when(j < n_main)
    def _():
        o_ref[...] = _dot_nt(h_ref[...], w_ref[...])

    @pl.when(j >= n_main)
    def _():
        sb_ref[...] = _dot_nt(h_ref[...], w_ref[...]).astype(BF16)


def _inproj(x2, mod3, g0, w_t, n_main_cols, seq, tm=1024, tn=768):
    T, D = x2.shape
    N = w_t.shape[0]
    n_main = n_main_cols // tn
    return pl.pallas_call(
        functools.partial(_inproj_kernel, d=D, n_main=n_main),
        out_shape=[jax.ShapeDtypeStruct((T, n_main_cols), F32),
                   jax.ShapeDtypeStruct((T, N - n_main_cols), BF16)],
        grid=(T // tm, N // tn),
        in_specs=[pl.BlockSpec((tm, D), lambda i, j: (i, 0)),
                  pl.BlockSpec((None, 1, mod3.shape[2]), lambda i, j: ((i * tm) // seq, 0, 0)),
                  pl.BlockSpec((1, D), lambda i, j: (0, 0)),
                  pl.BlockSpec((tn, D), lambda i, j: (j, 0))],
        out_specs=[pl.BlockSpec((tm, tn), lambda i, j: (i, jnp.minimum(j, n_main - 1))),
                   pl.BlockSpec((tm, tn), lambda i, j: (i, jnp.maximum(j - n_main, 0)))],
        scratch_shapes=[pltpu.VMEM((tm, D), BF16)],
        compiler_params=_cparams(("parallel", "arbitrary")),
        name="inproj",
    )(x2, mod3, g0, w_t)


def _token_shift_mix(cur_ref, prev_ref, mu_ref, first):
    cur = cur_ref[...]
    row = lax.broadcasted_iota(jnp.int32, (cur.shape[0], 1), 0)
    prev_row = jnp.where(first, 0.0, prev_ref[7:8, :])
    shifted = jnp.where(row == 0, prev_row, pltpu.roll(cur, 1, 0))
    return cur + (shifted - cur) * mu_ref[...]


def _prep_kernel(k_ref, l_ref, kp_ref, lp_ref, muk_ref, mul_ref, w0_ref, a0_ref, kk_ref, ka_ref,
                 w2_ref, a2_ref, g2_ref, bd_ref,
                 ko_ref, kko_ref, bo_ref, lwo_ref, go_ref, *, tm, seq):
    first = (pl.program_id(0) * tm) % seq == 0
    k = _token_shift_mix(k_ref, kp_ref, muk_ref, first)
    lo = _token_shift_mix(l_ref, lp_ref, mul_ref, first)

    dw = _dot(jnp.tanh(lo).astype(BF16), w2_ref[...])
    da = _dot(lo.astype(BF16), a2_ref[...])
    g = _dot(jax.nn.sigmoid(lo).astype(BF16), g2_ref[...])

    w_log = -jax.nn.softplus(-(w0_ref[...] + dw)) - 0.5
    lw = -jnp.exp(w_log)
    a = jax.nn.sigmoid(a0_ref[...] + da)
    kx = k * kk_ref[...]
    bd = bd_ref[...]
    n_blk = kx.shape[1] // LANES
    sq = kx * kx
    ssum = jnp.concatenate(
        [_head_sum(sq[:, c * LANES:(c + 1) * LANES], bd) for c in range(n_blk)], axis=1)
    kk = kx * lax.rsqrt(jnp.maximum(ssum, 1e-24))
    k2 = k * (1.0 + (a - 1.0) * ka_ref[...])

    ko_ref[...] = k2
    kko_ref[...] = kk
    bo_ref[...] = kk * a
    lwo_ref[...] = lw
    go_ref[...] = g


def _prep(P, seq, consts, tm=256):
    T = P.shape[0]
    C = consts["w0"].shape[1]
    col_k, col_l = consts["col_k"], consts["col_l"]

    def cur(col, w):
        return pl.BlockSpec((tm, w), lambda i: (i, col))

    def prev(col, w):
        return pl.BlockSpec((8, w), lambda i: (jnp.maximum(i * (tm // 8) - 1, 0), col))

    def const(shape):
        return pl.BlockSpec(shape, lambda i: (0, 0))

    out_sd = jax.ShapeDtypeStruct((T, C), F32)
    outs = pl.pallas_call(
        functools.partial(_prep_kernel, tm=tm, seq=seq),
        out_shape=[out_sd] * 5,
        grid=(T // tm,),
        in_specs=[cur(col_k, C), cur(col_l, LORA_PAD), prev(col_k, C), prev(col_l, LORA_PAD),
                  const((1, C)), const((1, LORA_PAD)),
                  const((1, C)), const((1, C)), const((1, C)), const((1, C)),
                  const((LORA_PAD, C)), const((LORA_PAD, C)), const((LORA_PAD, C)),
                  const((LANES, LANES))],
        out_specs=[pl.BlockSpec((tm, C), lambda i: (i, 0))] * 5,
        compiler_params=_cparams(("parallel",)),
        name="rwkv_prep",
    )(P, P, P, P, consts["mu_k"], consts["mu_l"],
      consts["w0"], consts["a0"], consts["k_k"], consts["k_a"],
      consts["w2p"], consts["a2p"], consts["g2p"], consts["bd"])
    return outs


def _scan_kernel(r_ref, v_ref, rp_ref, vp_ref, mur_ref, muv_ref, k_ref, kk_ref, b_ref, lw_ref,
                 g_ref, rk_ref, lnw_ref, lnb_ref, o_ref, h_ref, *, nc, npair):
    L = CHUNK
    half = L // 2

    @pl.when(pl.program_id(2) == 0)
    def _():
        h_ref[...] = jnp.zeros_like(h_ref)

    ri = lax.broadcasted_iota(jnp.int32, (L, L), 0)
    ci = lax.broadcasted_iota(jnp.int32, (L, L), 1)
    lower_incl = ri >= ci
    lower_strict = ri > ci
    eye = ri == ci
    same_head = (ri < HEAD_DIM) == (ci < HEAD_DIM)
    tri01 = lower_incl.astype(BF16)
    eye_f = eye.astype(F32)
    lane = lax.broadcasted_iota(jnp.int32, (1, LANES), 1)
    head0 = lane < HEAD_DIM
    lane2 = lax.broadcasted_iota(jnp.int32, (1, 2 * LANES), 1)
    head0_2 = (lane2 % LANES) < HEAD_DIM
    tri_stack = jnp.concatenate([lower_strict, lower_incl], axis=0)
    chunks = range(nc * npair)
    pairs = [(c, h) for c in chunks for h in range(2)]

    def sel(x0, x1):
        return jnp.where(head0 if x0.shape[1] == LANES else head0_2, x0, x1)

    def lanes_of(u):
        return slice((u // nc) * LANES, (u // nc + 1) * LANES)

    def ld(ref, u):
        return ref[pl.ds((u % nc) * L, L), lanes_of(u)]

    first = pl.program_id(2) == 0
    r_tile = _token_shift_mix(r_ref, rp_ref, mur_ref, first)
    v_tile = _token_shift_mix(v_ref, vp_ref, muv_ref, first)
    r = [r_tile[(u % nc) * L:(u % nc + 1) * L, lanes_of(u)] for u in chunks]
    v = [v_tile[(u % nc) * L:(u % nc + 1) * L, lanes_of(u)] for u in chunks]
    k = [ld(k_ref, c) for c in chunks]
    kk = [ld(kk_ref, c) for c in chunks]
    b = [ld(b_ref, c) for c in chunks]
    lw = [ld(lw_ref, c) for c in chunks]

    cum = [_exact01_dot(tri01, lw[c]) for c in chunks]
    lhs_mid, rhs_mid, kd_bf, r_dec, k_end, b_end, p_end, v_bf = [], [], [], [], [], [], [], []
    for c in chunks:
        mid = cum[c][half - 1:half, :]
        end = cum[c][L - 1:L, :]
        r_mid = r[c] * jnp.exp(cum[c] - mid)
        kk_mid = kk[c] * jnp.exp(cum[c] - lw[c] - mid)
        e_back = jnp.exp(mid - cum[c])
        lhs_mid.append(jnp.concatenate([kk_mid, r_mid], axis=0))
        rhs_mid.append(jnp.concatenate([b[c] * e_back, k[c] * e_back], axis=0).astype(BF16))
        kd_bf.append((kk[c] * jnp.exp(cum[c] - lw[c])).astype(BF16))
        r_dec.append(r[c] * jnp.exp(cum[c]))
        e_end = jnp.exp(end - cum[c])
        k_end.append((k[c] * e_end).astype(BF16))
        b_end.append((b[c] * e_end).astype(BF16))
        p_end.append(jnp.exp(end))
        v_bf.append(v[c].astype(BF16))

    aa = {}
    for (c, h) in pairs:
        hm = head0 if h == 0 else jnp.logical_not(head0)
        aa[c, h] = _dot_nt(jnp.where(hm, lhs_mid[c], 0.0).astype(BF16), rhs_mid[c])
    def side(f):
        return jnp.concatenate([f(0), f(1)], axis=1)

    tri2 = jnp.concatenate([tri_stack, tri_stack], axis=1)
    strict2 = jnp.concatenate([lower_strict, lower_strict], axis=1)
    incl2 = jnp.concatenate([lower_incl, lower_incl], axis=1)
    def by_head_rows(x):
        hm = head0 if x.shape[1] == LANES else head0_2
        zero = jnp.zeros_like(x)
        return jnp.concatenate([jnp.where(hm, x, zero), jnp.where(hm, zero, x)], axis=0)

    a_kb = [jnp.where(strict2, side(lambda h: aa[c, h][:L, :L]), 0.0) for c in chunks]
    a_rb = [jnp.where(incl2, side(lambda h: aa[c, h][L:, :L]), 0.0).astype(BF16) for c in chunks]
    a_xk = [jnp.where(tri2, side(lambda h: aa[c, h][:, L:]), 0.0).astype(BF16) for c in chunks]
    avr = [_dot(a_xk[c], by_head_rows(v_bf[c])) for c in chunks]

    hl = L // 2
    blk_r = lax.broadcasted_iota(jnp.int32, (2 * L, 2 * L), 0) // hl
    blk_c = lax.broadcasted_iota(jnp.int32, (2 * L, 2 * L), 1) // hl
    diag4 = blk_r == blk_c
    below4 = jnp.logical_and(blk_r == blk_c + 1, blk_c % 2 == 0)
    eye4 = (lax.broadcasted_iota(jnp.int32, (hl, 2 * L), 0)
            == lax.broadcasted_iota(jnp.int32, (hl, 2 * L), 1) % hl).astype(F32)

    def tile4(x, keep):
        return jnp.where(keep, jnp.concatenate([x, x, x, x], axis=0), jnp.zeros((), x.dtype))

    n_lvl = int(math.log2(hl)) - 1
    p, t, a21 = [], [], []
    for c in chunks:
        pd = jnp.where(head0_2, a_kb[c][:hl], a_kb[c][hl:])
        a21.append(jnp.where(head0_2, a_kb[c][hl:], 0.0).astype(BF16))
        pd_bf = pd.astype(BF16)
        p.append(_dot(pd_bf, tile4(pd_bf, diag4)))
        t.append(eye4 - pd)
    for lvl in range(n_lvl):
        for c in chunks:
            p_bd = tile4(p[c].astype(BF16), diag4)
            if lvl + 1 < n_lvl:
                tp = _dot(jnp.concatenate([t[c], p[c]], axis=0).astype(BF16), p_bd)
                t[c] = t[c] + tp[:hl]
                p[c] = tp[hl:]
            else:
                t[c] = t[c] + _dot(t[c].astype(BF16), p_bd)
    for c in chunks:
        td_bf = t[c].astype(BF16)
        y = _dot(a21[c], tile4(jnp.where(head0_2, td_bf, jnp.zeros((), BF16)), diag4))
        z = _dot(td_bf, tile4(y.astype(BF16), below4))
        t[c] = jnp.concatenate([jnp.where(head0_2, t[c], 0.0), jnp.where(head0_2, -z, t[c])], axis=0)

    rq, y0, gmat, cmat = [], [], [], []
    for c in chunks:
        av = avr[c][:L]
        arkv = avr[c][L:]
        x = jnp.concatenate([kd_bf[c], av.astype(BF16)], axis=1)
        w_bf = _dot(t[c].astype(BF16), by_head_rows(x)).astype(BF16)
        aw = _dot(a_rb[c], by_head_rows(w_bf))
        bw = _dot_tn(b_end[c], w_bf)
        kv = _dot_tn(k_end[c], v_bf[c])
        rq.append(r_dec[c] - aw[:, :LANES])
        y0.append(arkv - aw[:, LANES:])
        gmat.append(eye_f * p_end[c] - jnp.where(same_head, bw[:, :LANES], 0.0))
        cmat.append(jnp.where(same_head, kv - bw[:, LANES:], 0.0))

    bd2 = ((lax.broadcasted_iota(jnp.int32, (2 * LANES, 2 * LANES), 0) // HEAD_DIM)
           == (lax.broadcasted_iota(jnp.int32, (2 * LANES, 2 * LANES), 1) // HEAD_DIM)).astype(BF16)

    def head_sum_all(x):
        n = x.shape[0] // 2
        s2 = _dot_exact01(jnp.concatenate([x[:n], x[n:]], axis=1), bd2)
        return jnp.concatenate([s2[:, :LANES], s2[:, LANES:]], axis=0)

    rkr = head_sum_all(jnp.concatenate(
        [r[u] * k[u] * rk_ref[:, lanes_of(u)] for u in chunks], axis=0))
    bonus = [rkr[u * L:(u + 1) * L] * v[u] for u in chunks]

    inv_n = 1.0 / HEAD_DIM
    hstate = [h_ref[q] for q in range(npair)]
    ys = [None] * len(chunks)
    for c in range(nc):
        for q in range(npair):
            u = q * nc + c
            res = _dot(jnp.concatenate([rq[u], gmat[u]], axis=0).astype(BF16),
                       hstate[q].astype(BF16))
            ys[u] = res[:L] + y0[u]
            hstate[q] = res[L:] + cmat[u]
    for q in range(npair):
        h_ref[q] = hstate[q]
    y = jnp.concatenate(ys, axis=0)
    mean = head_sum_all(y) * inv_n
    yc = y - mean
    var = head_sum_all(yc * yc) * inv_n
    yn = yc * lax.rsqrt(var + GN_EPS)
    for u in chunks:
        rows = pl.ds((u % nc) * L, L)
        o_ref[rows, lanes_of(u)] = (
            (yn[u * L:(u + 1) * L] * lnw_ref[:, lanes_of(u)] + lnb_ref[:, lanes_of(u)] + bonus[u])
            * g_ref[rows, lanes_of(u)]).astype(BF16)


def _scan(P, col_r, col_v, mu_r, mu_v, arrs, rk, lnw, lnb, batch, seq, nc=4, npair=4):
    k, kk, b, lw, g = [a.reshape(batch, seq, a.shape[1]) for a in arrs]
    P3 = P.reshape(batch, seq, P.shape[1])
    C = k.shape[2]
    width = LANES * npair
    rows = CHUNK * nc
    blk = pl.BlockSpec((None, rows, width), lambda bi, p, c: (bi, c, p))
    par = pl.BlockSpec((1, width), lambda bi, p, c: (0, p))

    def raw(col):
        return pl.BlockSpec((None, rows, width), lambda bi, p, c: (bi, c, col // width + p))

    def raw_prev(col):
        return pl.BlockSpec((None, 8, width),
                            lambda bi, p, c: (bi, jnp.maximum(c * (rows // 8) - 1, 0),
                                              col // width + p))

    out = pl.pallas_call(
        functools.partial(_scan_kernel, nc=nc, npair=npair),
        out_shape=jax.ShapeDtypeStruct((batch, seq, C), BF16),
        grid=(batch, C // width, seq // rows),
        in_specs=[raw(col_r), raw(col_v), raw_prev(col_r), raw_prev(col_v), par, par]
                 + [blk] * 5 + [par, par, par],
        out_specs=blk,
        scratch_shapes=[pltpu.VMEM((npair, LANES, LANES), F32)],
        compiler_params=_cparams(("parallel", "parallel", "arbitrary")),
        name="rwkv_scan",
    )(P3, P3, P3, P3, mu_r, mu_v, k, kk, b, lw, g, rk, lnw, lnb)
    return out.reshape(batch * seq, C)


def _sb_kernel(q_ref, k_ref, v_ref, o_ref, *, npair, nq):
    tq = tk = SB_BLOCK
    nu = SB_BLOCKS_PER_STEP
    i = pl.program_id(2)
    lane = lax.broadcasted_iota(jnp.int32, (1, LANES), 1)
    head0 = lane < HEAD_DIM
    scale = jnp.asarray(1.0 / math.sqrt(HEAD_DIM), BF16)
    streams = [(s, hp) for s in range(nq) for hp in range(npair)]
    ns = len(streams)
    q = [q_ref[s * tq:(s + 1) * tq, hp * LANES:(hp + 1) * LANES] * scale for (s, hp) in streams]
    r2 = lax.broadcasted_iota(jnp.int32, (2 * tk, 2 * tk), 0)
    c2 = lax.broadcasted_iota(jnp.int32, (2 * tk, 2 * tk), 1)
    after01 = jnp.logical_and(r2 > c2, (r2 < tk) == (c2 < tk)).astype(BF16)
    key_minus_query = (lax.broadcasted_iota(jnp.int32, (tq, 2 * tk), 1) % tk
                       - lax.broadcasted_iota(jnp.int32, (tq, 2 * tk), 0))
    units = [(p, u) for p in range(ns) for u in range(nu)]

    def by_head_rows(x):
        zero = jnp.zeros_like(x)
        return jnp.concatenate([jnp.where(head0, x, zero), jnp.where(head0, zero, x)], axis=0)

    def both(c0, c1):
        return jnp.concatenate([jnp.broadcast_to(c0, (tq, tk)), jnp.broadcast_to(c1, (tq, tk))],
                               axis=1)

    def body(carry):
        jt, _, offs, accs = carry
        z, sp, log_keep, mask, vb = {}, {}, {}, {}, {}
        for (p, u) in units:
            s, hp = streams[p]
            jb = jt - (nq - 1 - s) - u
            start = pl.multiple_of(jnp.maximum(jb, 0) * tk, tk)
            lanes = slice(hp * LANES, (hp + 1) * LANES)
            kb = by_head_rows(k_ref[pl.ds(start, tk), lanes])
            vb[p, u] = by_head_rows(v_ref[pl.ds(start, tk), lanes])
            zu = _dot_nt(q[p], kb)
            if u > 0:
                zu = zu + jnp.where(jb >= 0, 0.0, -1e30)
            spu = jnp.maximum(zu, 0.0) + jnp.log(1.0 + jnp.exp(-jnp.abs(zu)))
            z[p, u], sp[p, u] = zu, spu
            if u == 0:
                q_start = (i * nq + s) * tq
                mask[p, u] = key_minus_query < jnp.where(jb >= 0, q_start - jb * tk, -tq)
                log_keep[p, u] = jnp.where(mask[p, u], -spu, 0.0)
            else:
                log_keep[p, u] = -spu
        parts = []
        for key in units:
            parts.extend(_split2(log_keep[key]))
        loc = _dot(jnp.concatenate(parts, axis=0), after01)
        offs, accs = list(offs), list(accs)
        for n, (p, u) in enumerate(units):
            local = loc[2 * n * tq:(2 * n + 1) * tq] + loc[(2 * n + 1) * tq:(2 * n + 2) * tq]
            later = local + both(offs[2 * p], offs[2 * p + 1])
            att = jnp.exp(z[p, u] - sp[p, u] + later)
            if u == 0:
                att = jnp.where(mask[p, u], att, 0.0)
            accs[p] = accs[p] + _dot(att.astype(BF16), vb[p, u])
            offs[2 * p] = offs[2 * p] + (local[:, 0:1] + log_keep[p, u][:, 0:1])
            offs[2 * p + 1] = offs[2 * p + 1] + (local[:, tk:tk + 1] + log_keep[p, u][:, tk:tk + 1])
        top = offs[0]
        for o in offs[1:]:
            top = jnp.maximum(top, o)
        alive = (jnp.max(top) > -SB_UNDERFLOW).astype(jnp.int32)
        return jt - nu, alive, tuple(offs), tuple(accs)

    def cond(carry):
        jt, alive = carry[0], carry[1]
        return jnp.logical_and(jt >= 0, alive > 0)

    init = (i * nq + nq - 1, jnp.int32(1),
            tuple(jnp.zeros((tq, 1), F32) for _ in range(2 * ns)),
            tuple(jnp.zeros((tq, LANES), F32) for _ in range(ns)))
    accs = lax.while_loop(cond, body, init)[3]
    for p, (s, hp) in enumerate(streams):
        o_ref[s * tq:(s + 1) * tq, hp * LANES:(hp + 1) * LANES] = accs[p].astype(BF16)


def _stick_breaking(qkv, width, npair=2, nq=2):
    B, S, _ = qkv.shape
    w = LANES * npair
    nblk = width // w
    rows = SB_BLOCK * nq
    out = pl.pallas_call(
        functools.partial(_sb_kernel, npair=npair, nq=nq),
        out_shape=jax.ShapeDtypeStruct((B, S, width), BF16),
        grid=(B, nblk, S // rows),
        in_specs=[pl.BlockSpec((None, rows, w), lambda b, p, i: (b, i, p)),
                  pl.BlockSpec((None, S, w), lambda b, p, i: (b, 0, nblk + p)),
                  pl.BlockSpec((None, S, w), lambda b, p, i: (b, 0, 2 * nblk + p))],
        out_specs=pl.BlockSpec((None, rows, w), lambda b, p, i: (b, i, p)),
        compiler_params=_cparams(("parallel", "parallel", "arbitrary")),
        name="stick_breaking",
    )(qkv, qkv, qkv)
    return out.reshape(B * S, width)


def _merge_kernel(ya_ref, yb_ref, ga_ref, gb_ref, x_ref, mod_ref, g1_ref,
                  g2_ref, wua_ref, wub_ref, wo_ref, o_ref, hf_ref, *, d):
    ua = _dot(ya_ref[...], wua_ref[...])
    ub = _dot(yb_ref[...], wub_ref[...])
    merged = jax.nn.sigmoid(ga_ref[...]) * ua + jax.nn.sigmoid(gb_ref[...]) * ub
    mix = _dot(merged.astype(BF16), wo_ref[...])
    gate_m = mod_ref[:, 2 * d:3 * d]
    x1 = x_ref[...] + gate_m * _rms(mix, g1_ref[...])
    o_ref[...] = x1
    shift_f = mod_ref[:, 3 * d:4 * d]
    scale_f = mod_ref[:, 4 * d:5 * d]
    hf_ref[...] = (_rms(x1, g2_ref[...]) * (1.0 + scale_f) + shift_f).astype(BF16)


def _merge(ya, yb, P, x2, mod3, g1, g2, wua, wub, wo, seq, tm=256):
    T, D = x2.shape
    C = ya.shape[1]

    def const(shape):
        return pl.BlockSpec(shape, lambda i: (0, 0), pipeline_mode=pl.Buffered(1))

    return pl.pallas_call(
        functools.partial(_merge_kernel, d=D),
        out_shape=[jax.ShapeDtypeStruct((T, D), F32), jax.ShapeDtypeStruct((T, D), BF16)],
        grid=(T // tm,),
        in_specs=[pl.BlockSpec((tm, C), lambda i: (i, 0)),
                  pl.BlockSpec((tm, C), lambda i: (i, 0)),
                  pl.BlockSpec((tm, D), lambda i: (i, 0)),
                  pl.BlockSpec((tm, D), lambda i: (i, 1)),
                  pl.BlockSpec((tm, D), lambda i: (i, 0)),
                  pl.BlockSpec((None, 1, mod3.shape[2]), lambda i: ((i * tm) // seq, 0, 0)),
                  pl.BlockSpec((1, D), lambda i: (0, 0)),
                  pl.BlockSpec((1, D), lambda i: (0, 0)),
                  const((C, D)), const((C, D)), const((D, D))],
        out_specs=[pl.BlockSpec((tm, D), lambda i: (i, 0)), pl.BlockSpec((tm, D), lambda i: (i, 0))],
        compiler_params=_cparams(("parallel",)),
        name="merge_out",
    )(ya, yb, P, P, x2, mod3, g1, g2, wua, wub, wo)


def _mlp_kernel(x_ref, hf_ref, mod_ref, g3_ref, w1_ref, w2_ref, o_ref, acc_ref, *, d):
    j = pl.program_id(1)

    @pl.when(j == 0)
    def _():
        acc_ref[...] = jnp.zeros_like(acc_ref)

    u = jnp.maximum(_dot(hf_ref[...], w1_ref[...]), 0.0)
    acc_ref[...] += _dot((u * u).astype(BF16), w2_ref[...])

    @pl.when(j == pl.num_programs(1) - 1)
    def _():
        gate_f = mod_ref[:, 5 * d:6 * d]
        o_ref[...] = x_ref[...] + gate_f * _rms(acc_ref[...], g3_ref[...])


def _mlp(x2, hf, mod3, g3, w1, w2, seq, tm=512, tf=1024):
    T, D = x2.shape
    F = w1.shape[1]
    return pl.pallas_call(
        functools.partial(_mlp_kernel, d=D),
        out_shape=jax.ShapeDtypeStruct((T, D), F32),
        grid=(T // tm, F // tf),
        in_specs=[pl.BlockSpec((tm, D), lambda i, j: (i, 0)),
                  pl.BlockSpec((tm, D), lambda i, j: (i, 0)),
                  pl.BlockSpec((None, 1, mod3.shape[2]), lambda i, j: ((i * tm) // seq, 0, 0)),
                  pl.BlockSpec((1, D), lambda i, j: (0, 0)),
                  pl.BlockSpec((D, tf), lambda i, j: (0, j)),
                  pl.BlockSpec((tf, D), lambda i, j: (j, 0))],
        out_specs=pl.BlockSpec((tm, D), lambda i, j: (i, 0)),
        scratch_shapes=[pltpu.VMEM((tm, D), F32)],
        compiler_params=_cparams(("parallel", "arbitrary")),
        name="mlp",
    )(x2, hf, mod3, g3, w1, w2)


def _pad_rows(w, offset, total):
    return jnp.zeros((total, w.shape[1]), w.dtype).at[offset:offset + w.shape[0]].set(w)


def kernel(x, c, w_ada, b_ada, norm_g, w_in, mu_shift, w0, w2, a0, a2, g2, k_k, k_a,
           r_k, ln_x_w, ln_x_b, w_up_rwkv, w_up_sb, w_out, w_mlp_in, w_mlp_out):
    B, S, D = x.shape
    depth = w_in.shape[0]
    C = w0.shape[1]
    W = w_up_sb.shape[1]
    rwkv_cols = 3 * C + DECAY_LORA + ICLR_LORA + GATE_LORA
    sb_cols = 3 * W
    n_lora = DECAY_LORA + ICLR_LORA + GATE_LORA
    assert C == 1024 and W == 1024 and D == 2 * C and S % (2 * CHUNK) == 0

    hi = lax.broadcasted_iota(jnp.int32, (LANES, LANES), 0) // HEAD_DIM
    hj = lax.broadcasted_iota(jnp.int32, (LANES, LANES), 1) // HEAD_DIM
    bd = (hi == hj).astype(BF16)

    x2 = x.reshape(B * S, D)
    for l in range(depth):
        mod = _ada(c, w_ada[l], b_ada[l])
        mod3 = mod.reshape(B, 1, mod.shape[1])

        g_off = rwkv_cols + sb_cols
        n_main_cols = 2 * D + 3 * C + LORA_PAD
        w_perm = _regroup_rows(
            w_in[l].T,
            [(g_off, 2 * D), (0, 3 * C), (3 * C, LORA_PAD), (rwkv_cols, sb_cols)],
            zero_lo=2 * D + 3 * C + n_lora, zero_hi=n_main_cols)
        P, qkv = _inproj(x2, mod3, norm_g[l, 0].reshape(1, D), w_perm, n_main_cols, S)

        mu = mu_shift[l]
        consts = dict(
            col_k=2 * D // C + 1, col_l=(2 * D + 3 * C) // LORA_PAD,
            mu_k=mu[None, C:2 * C],
            mu_l=jnp.pad(mu[None, 3 * C:], ((0, 0), (0, LORA_PAD - n_lora))),
            w0=w0[l][None], a0=a0[l][None], k_k=k_k[l][None], k_a=k_a[l][None],
            w2p=_pad_rows(w2[l], 0, LORA_PAD).astype(BF16),
            a2p=_pad_rows(a2[l], DECAY_LORA, LORA_PAD).astype(BF16),
            g2p=_pad_rows(g2[l], DECAY_LORA + ICLR_LORA, LORA_PAD).astype(BF16),
            bd=bd)
        prep = _prep(P, S, consts)
        ya = _scan(P, 2 * D, 2 * D + 2 * C, mu[None, :C], mu[None, 2 * C:3 * C], prep,
                   r_k[l].reshape(1, C), ln_x_w[l][None], ln_x_b[l][None], B, S)

        yb = _stick_breaking(qkv.reshape(B, S, 3 * W), W)

        x2, hf = _merge(ya, yb, P, x2, mod3, norm_g[l, 1].reshape(1, D),
                        norm_g[l, 2].reshape(1, D), w_up_rwkv[l].astype(BF16),
                        w_up_sb[l].astype(BF16), w_out[l].astype(BF16), S)
        x2 = _mlp(x2, hf, mod3, norm_g[l, 3].reshape(1, D),
                  w_mlp_in[l].astype(BF16), w_mlp_out[l].astype(BF16), S)
    return x2.reshape(B, S, D)
```

```python
import functools
import math

import jax
import jax.numpy as jnp
from jax import lax
from jax.experimental import pallas as pl
from jax.experimental.pallas import tpu as pltpu

F32 = jnp.float32
BF16 = jnp.bfloat16

HEAD_DIM = 64
LANES = 128
DECAY_LORA = 64
ICLR_LORA = 64
GATE_LORA = 160
LORA_PAD = 512
NORM_EPS = 1e-6
GN_EPS = 64e-5
CHUNK = 128
SB_BLOCK = 128
SB_BLOCKS_PER_STEP = 3
SB_UNDERFLOW = 104.0
VMEM_LIMIT = 56 * 1024 * 1024


def _cparams(sem, vmem=VMEM_LIMIT):
    return pltpu.CompilerParams(dimension_semantics=sem, vmem_limit_bytes=vmem)


def _dot(a, b):
    return jnp.dot(a, b, preferred_element_type=F32)


def _dot_nt(a, b):
    return lax.dot_general(a, b, (((1,), (1,)), ((), ())), preferred_element_type=F32)


def _dot_tn(a, b):
    return lax.dot_general(a, b, (((0,), (0,)), ((), ())), preferred_element_type=F32)


def _split2(x):
    hi = x.astype(BF16)
    lo = (x - hi.astype(F32)).astype(BF16)
    return hi, lo


def _dot_exact01(x, m01):
    n = x.shape[0]
    res = _dot(jnp.concatenate(_split2(x), axis=0), m01)
    return res[:n] + res[n:]


def _exact01_dot(m01, x):
    n = x.shape[1]
    res = _dot(m01, jnp.concatenate(_split2(x), axis=1))
    return res[:, :n] + res[:, n:]


def _head_sum(x, bd):
    return _dot_exact01(x, bd)


def _rms(x, g):
    return x * lax.rsqrt(jnp.mean(x * x, axis=-1, keepdims=True) + NORM_EPS) * g


def _ada_kernel(c_ref, w_ref, b_ref, o_ref):
    c = c_ref[...]
    s = c * jax.nn.sigmoid(c)
    o_ref[...] = _dot(s.astype(BF16), w_ref[...].astype(BF16)) + b_ref[...]


def _ada(c, w_ada, b_ada, tn=1536):
    B, D = c.shape
    N = w_ada.shape[1]
    rows = 8
    c_pad = jnp.zeros((rows, D), F32).at[:B].set(c)
    out = pl.pallas_call(
        _ada_kernel,
        out_shape=jax.ShapeDtypeStruct((rows, N), F32),
        grid=(N // tn,),
        in_specs=[pl.BlockSpec((rows, D), lambda j: (0, 0)),
                  pl.BlockSpec((D, tn), lambda j: (0, j)),
                  pl.BlockSpec((1, tn), lambda j: (0, j))],
        out_specs=pl.BlockSpec((rows, tn), lambda j: (0, j)),
        compiler_params=_cparams(("parallel",)),
        name="ada",
    )(c_pad, w_ada, b_ada.reshape(1, N))
    return out[:B]


def _regroup_kernel(offs_ref, w_ref, o_ref, *, zero_lo, zero_hi, rows):
    del offs_ref
    out_row = pl.program_id(0) * rows + lax.broadcasted_iota(jnp.int32, (rows, 1), 0)
    pad = jnp.logical_and(out_row >= zero_lo, out_row < zero_hi)
    o_ref[...] = jnp.where(pad, 0.0, w_ref[...]).astype(BF16)


def _regroup_rows(wt, groups, zero_lo, zero_hi, rows=256):
    D = wt.shape[1]
    sub = 8
    assert all(s % sub == 0 and n % rows == 0 for s, n in groups)
    offs = jnp.asarray([(s + r) // sub for (s, n) in groups for r in range(0, n, rows)], jnp.int32)
    n_out = sum(n for _, n in groups)
    return pl.pallas_call(
        functools.partial(_regroup_kernel, zero_lo=zero_lo, zero_hi=zero_hi, rows=rows),
        out_shape=jax.ShapeDtypeStruct((n_out, D), BF16),
        grid_spec=pltpu.PrefetchScalarGridSpec(
            num_scalar_prefetch=1, grid=(n_out // rows,),
            in_specs=[pl.BlockSpec((pl.Element(rows), pl.Element(D)),
                                   lambda t, offs: (offs[t] * sub, 0))],
            out_specs=pl.BlockSpec((rows, D), lambda t, offs: (t, 0))),
        compiler_params=_cparams(("parallel",)),
        name="regroup_w_in",
    )(offs, wt)


def _inproj_kernel(x0_ref, mod0_ref, xn_ref, modn_ref, g_ref, w_ref, o_ref, sb_ref, h_ref,
                   *, d, n_main, chunk):
    i = pl.program_id(0)
    j = pl.program_id(1)
    slot = i % 2
    tm = xn_ref.shape[0]

    def modulated(x, mod_ref):
        shift = mod_ref[:, 0:d]
        scale = mod_ref[:, d:2 * d]
        return (_rms(x, g_ref[...]) * (1.0 + scale) + shift).astype(BF16)

    @pl.when(jnp.logical_and(i == 0, j == 0))
    def _():
        h_ref[0] = modulated(x0_ref[...], mod0_ref)

    def prepare_next_rows():
        r0 = pl.multiple_of(jnp.minimum(j * chunk, tm - chunk), 16)
        h_ref[1 - slot, pl.ds(r0, chunk), :] = modulated(xn_ref[pl.ds(r0, chunk), :], modn_ref)

    @pl.when(j < n_main)
    def _():
        o_ref[...] = _dot_nt(h_ref[slot], w_ref[...])
        prepare_next_rows()

    @pl.when(j >= n_main)
    def _():
        sb_ref[...] = _dot_nt(h_ref[slot], w_ref[...]).astype(BF16)
        prepare_next_rows()


def _inproj(x2, mod3, g0, w_t, n_main_cols, seq, tm=1024, tn=768):
    T, D = x2.shape
    N = w_t.shape[0]
    n_main = n_main_cols // tn
    n_i, n_j = T // tm, N // tn
    chunk = 16 * pl.cdiv(pl.cdiv(tm, n_j), 16)

    def nxt(i):
        return jnp.minimum(i + 1, n_i - 1)

    return pl.pallas_call(
        functools.partial(_inproj_kernel, d=D, n_main=n_main, chunk=chunk),
        out_shape=[jax.ShapeDtypeStruct((T, n_main_cols), F32),
                   jax.ShapeDtypeStruct((T, N - n_main_cols), BF16)],
        grid=(n_i, n_j),
        in_specs=[pl.BlockSpec((tm, D), lambda i, j: (0, 0), pipeline_mode=pl.Buffered(1)),
                  pl.BlockSpec((None, 1, mod3.shape[2]), lambda i, j: (0, 0, 0)),
                  pl.BlockSpec((tm, D), lambda i, j: (nxt(i), 0)),
                  pl.BlockSpec((None, 1, mod3.shape[2]), lambda i, j: ((nxt(i) * tm) // seq, 0, 0)),
                  pl.BlockSpec((1, D), lambda i, j: (0, 0)),
                  pl.BlockSpec((tn, D), lambda i, j: (j, 0))],
        out_specs=[pl.BlockSpec((tm, tn), lambda i, j: (i, jnp.minimum(j, n_main - 1))),
                   pl.BlockSpec((tm, tn), lambda i, j: (i, jnp.maximum(j - n_main, 0)))],
        scratch_shapes=[pltpu.VMEM((2, tm, D), BF16)],
        compiler_params=_cparams(("arbitrary", "arbitrary")),
        name="inproj",
    )(x2, mod3, x2, mod3, g0, w_t)


def _token_shift_mix(cur_ref, prev_ref, mu_ref, first):
    cur = cur_ref[...]
    row = lax.broadcasted_iota(jnp.int32, (cur.shape[0], 1), 0)
    prev_row = jnp.where(first, 0.0, prev_ref[7:8, :])
    shifted = jnp.where(row == 0, prev_row, pltpu.roll(cur, 1, 0))
    return cur + (shifted - cur) * mu_ref[...]


def _prep_kernel(k_ref, l_ref, kp_ref, lp_ref, muk_ref, mul_ref, w0_ref, a0_ref, kk_ref, ka_ref,
                 w2_ref, a2_ref, g2_ref, bd_ref,
                 ko_ref, kko_ref, bo_ref, lwo_ref, go_ref, *, tm, seq):
    first = (pl.program_id(0) * tm) % seq == 0
    k = _token_shift_mix(k_ref, kp_ref, muk_ref, first)
    lo = _token_shift_mix(l_ref, lp_ref, mul_ref, first)

    dw = _dot(jnp.tanh(lo).astype(BF16), w2_ref[...])
    da = _dot(lo.astype(BF16), a2_ref[...])
    g = _dot(jax.nn.sigmoid(lo).astype(BF16), g2_ref[...])

    w_log = -jax.nn.softplus(-(w0_ref[...] + dw)) - 0.5
    lw = -jnp.exp(w_log)
    a = jax.nn.sigmoid(a0_ref[...] + da)
    kx = k * kk_ref[...]
    bd = bd_ref[...]
    n_blk = kx.shape[1] // LANES
    sq = kx * kx
    ssum = jnp.concatenate(
        [_head_sum(sq[:, c * LANES:(c + 1) * LANES], bd) for c in range(n_blk)], axis=1)
    kk = kx * lax.rsqrt(jnp.maximum(ssum, 1e-24))
    k2 = k * (1.0 + (a - 1.0) * ka_ref[...])

    ko_ref[...] = k2
    kko_ref[...] = kk
    bo_ref[...] = kk * a
    lwo_ref[...] = lw
    go_ref[...] = g


def _prep(P, seq, consts, tm=512):
    T = P.shape[0]
    C = consts["w0"].shape[1]
    col_k, col_l = consts["col_k"], consts["col_l"]

    def cur(col, w):
        return pl.BlockSpec((tm, w), lambda i: (i, col))

    def prev(col, w):
        return pl.BlockSpec((8, w), lambda i: (jnp.maximum(i * (tm // 8) - 1, 0), col))

    def const(shape):
        return pl.BlockSpec(shape, lambda i: (0, 0))

    out_sd = jax.ShapeDtypeStruct((T, C), F32)
    outs = pl.pallas_call(
        functools.partial(_prep_kernel, tm=tm, seq=seq),
        out_shape=[out_sd] * 5,
        grid=(T // tm,),
        in_specs=[cur(col_k, C), cur(col_l, LORA_PAD), prev(col_k, C), prev(col_l, LORA_PAD),
                  const((1, C)), const((1, LORA_PAD)),
                  const((1, C)), const((1, C)), const((1, C)), const((1, C)),
                  const((LORA_PAD, C)), const((LORA_PAD, C)), const((LORA_PAD, C)),
                  const((LANES, LANES))],
        out_specs=[pl.BlockSpec((tm, C), lambda i: (i, 0))] * 5,
        compiler_params=_cparams(("parallel",)),
        name="rwkv_prep",
    )(P, P, P, P, consts["mu_k"], consts["mu_l"],
      consts["w0"], consts["a0"], consts["k_k"], consts["k_a"],
      consts["w2p"], consts["a2p"], consts["g2p"], consts["bd"])
    return outs


def _scan_kernel(r_ref, v_ref, rp_ref, vp_ref, mur_ref, muv_ref, k_ref, kk_ref, b_ref, lw_ref,
                 g_ref, rk_ref, lnw_ref, lnb_ref, o_ref, h_ref, *, nc, npair):
    L = CHUNK
    half = L // 2

    @pl.when(pl.program_id(2) == 0)
    def _():
        h_ref[...] = jnp.zeros_like(h_ref)

    ri = lax.broadcasted_iota(jnp.int32, (L, L), 0)
    ci = lax.broadcasted_iota(jnp.int32, (L, L), 1)
    lower_incl = ri >= ci
    lower_strict = ri > ci
    eye = ri == ci
    same_head = (ri < HEAD_DIM) == (ci < HEAD_DIM)
    tri01 = lower_incl.astype(BF16)
    eye_f = eye.astype(F32)
    lane = lax.broadcasted_iota(jnp.int32, (1, LANES), 1)
    head0 = lane < HEAD_DIM
    lane2 = lax.broadcasted_iota(jnp.int32, (1, 2 * LANES), 1)
    head0_2 = (lane2 % LANES) < HEAD_DIM
    tri_stack = jnp.concatenate([lower_strict, lower_incl], axis=0)
    chunks = range(nc * npair)
    pairs = [(c, h) for c in chunks for h in range(2)]

    def sel(x0, x1):
        return jnp.where(head0 if x0.shape[1] == LANES else head0_2, x0, x1)

    def lanes_of(u):
        return slice((u // nc) * LANES, (u // nc + 1) * LANES)

    def ld(ref, u):
        return ref[pl.ds((u % nc) * L, L), lanes_of(u)]

    first = pl.program_id(2) == 0
    r_tile = _token_shift_mix(r_ref, rp_ref, mur_ref, first)
    v_tile = _token_shift_mix(v_ref, vp_ref, muv_ref, first)
    r = [r_tile[(u % nc) * L:(u % nc + 1) * L, lanes_of(u)] for u in chunks]
    v = [v_tile[(u % nc) * L:(u % nc + 1) * L, lanes_of(u)] for u in chunks]
    k = [ld(k_ref, c) for c in chunks]
    kk = [ld(kk_ref, c) for c in chunks]
    b = [ld(b_ref, c) for c in chunks]
    lw = [ld(lw_ref, c) for c in chunks]

    cum = [_exact01_dot(tri01, lw[c]) for c in chunks]
    lhs_mid, rhs_mid, kd_bf, r_dec, k_end, b_end, p_end, v_bf = [], [], [], [], [], [], [], []
    for c in chunks:
        mid = cum[c][half - 1:half, :]
        end = cum[c][L - 1:L, :]
        r_mid = r[c] * jnp.exp(cum[c] - mid)
        kk_mid = kk[c] * jnp.exp(cum[c] - lw[c] - mid)
        e_back = jnp.exp(mid - cum[c])
        lhs_mid.append(jnp.concatenate([kk_mid, r_mid], axis=0))
        rhs_mid.append(jnp.concatenate([b[c] * e_back, k[c] * e_back], axis=0).astype(BF16))
        kd_bf.append((kk[c] * jnp.exp(cum[c] - lw[c])).astype(BF16))
        r_dec.append(r[c] * jnp.exp(cum[c]))
        e_end = jnp.exp(end - cum[c])
        k_end.append((k[c] * e_end).astype(BF16))
        b_end.append((b[c] * e_end).astype(BF16))
        p_end.append(jnp.exp(end))
        v_bf.append(v[c].astype(BF16))

    aa = {}
    for (c, h) in pairs:
        hm = head0 if h == 0 else jnp.logical_not(head0)
        aa[c, h] = _dot_nt(jnp.where(hm, lhs_mid[c], 0.0).astype(BF16), rhs_mid[c])
    def side(f):
        return jnp.concatenate([f(0), f(1)], axis=1)

    tri2 = jnp.concatenate([tri_stack, tri_stack], axis=1)
    strict2 = jnp.concatenate([lower_strict, lower_strict], axis=1)
    incl2 = jnp.concatenate([lower_incl, lower_incl], axis=1)
    def by_head_rows(x):
        hm = head0 if x.shape[1] == LANES else head0_2
        zero = jnp.zeros_like(x)
        return jnp.concatenate([jnp.where(hm, x, zero), jnp.where(hm, zero, x)], axis=0)

    a_kb = [jnp.where(strict2, side(lambda h: aa[c, h][:L, :L]), 0.0) for c in chunks]
    a_rb = [jnp.where(incl2, side(lambda h: aa[c, h][L:, :L]), 0.0).astype(BF16) for c in chunks]
    a_xk = [jnp.where(tri2, side(lambda h: aa[c, h][:, L:]), 0.0).astype(BF16) for c in chunks]
    avr = [_dot(a_xk[c], by_head_rows(v_bf[c])) for c in chunks]

    hl = L // 2
    blk_r = lax.broadcasted_iota(jnp.int32, (2 * L, 2 * L), 0) // hl
    blk_c = lax.broadcasted_iota(jnp.int32, (2 * L, 2 * L), 1) // hl
    diag4 = blk_r == blk_c
    below4 = jnp.logical_and(blk_r == blk_c + 1, blk_c % 2 == 0)
    eye4 = (lax.broadcasted_iota(jnp.int32, (hl, 2 * L), 0)
            == lax.broadcasted_iota(jnp.int32, (hl, 2 * L), 1) % hl).astype(F32)

    def tile4(x, keep):
        return jnp.where(keep, jnp.concatenate([x, x, x, x], axis=0), jnp.zeros((), x.dtype))

    n_lvl = int(math.log2(hl)) - 1
    p, t, a21 = [], [], []
    for c in chunks:
        pd = jnp.where(head0_2, a_kb[c][:hl], a_kb[c][hl:])
        a21.append(jnp.where(head0_2, a_kb[c][hl:], 0.0).astype(BF16))
        pd_bf = pd.astype(BF16)
        p.append(_dot(pd_bf, tile4(pd_bf, diag4)))
        t.append(eye4 - pd)
    for lvl in range(n_lvl):
        for c in chunks:
            p_bd = tile4(p[c].astype(BF16), diag4)
            if lvl + 1 < n_lvl:
                tp = _dot(jnp.concatenate([t[c], p[c]], axis=0).astype(BF16), p_bd)
                t[c] = t[c] + tp[:hl]
                p[c] = tp[hl:]
            else:
                t[c] = t[c] + _dot(t[c].astype(BF16), p_bd)
    for c in chunks:
        td_bf = t[c].astype(BF16)
        y = _dot(a21[c], tile4(jnp.where(head0_2, td_bf, jnp.zeros((), BF16)), diag4))
        z = _dot(td_bf, tile4(y.astype(BF16), below4))
        t[c] = jnp.concatenate([jnp.where(head0_2, t[c], 0.0), jnp.where(head0_2, -z, t[c])], axis=0)

    rq, y0, gmat, cmat = [], [], [], []
    for c in chunks:
        av = avr[c][:L]
        arkv = avr[c][L:]
        x = jnp.concatenate([kd_bf[c], av.astype(BF16)], axis=1)
        w_bf = _dot(t[c].astype(BF16), by_head_rows(x)).astype(BF16)
        aw = _dot(a_rb[c], by_head_rows(w_bf))
        bw = _dot_tn(b_end[c], w_bf)
        kv = _dot_tn(k_end[c], v_bf[c])
        rq.append(r_dec[c] - aw[:, :LANES])
        y0.append(arkv - aw[:, LANES:])
        gmat.append(eye_f * p_end[c] - jnp.where(same_head, bw[:, :LANES], 0.0))
        cmat.append(jnp.where(same_head, kv - bw[:, LANES:], 0.0))

    bd2 = ((lax.broadcasted_iota(jnp.int32, (2 * LANES, 2 * LANES), 0) // HEAD_DIM)
           == (lax.broadcasted_iota(jnp.int32, (2 * LANES, 2 * LANES), 1) // HEAD_DIM)).astype(BF16)

    def head_sum_all(x):
        n = x.shape[0] // 2
        s2 = _dot_exact01(jnp.concatenate([x[:n], x[n:]], axis=1), bd2)
        return jnp.concatenate([s2[:, :LANES], s2[:, LANES:]], axis=0)

    rkr = head_sum_all(jnp.concatenate(
        [r[u] * k[u] * rk_ref[:, lanes_of(u)] for u in chunks], axis=0))
    bonus = [rkr[u * L:(u + 1) * L] * v[u] for u in chunks]

    inv_n = 1.0 / HEAD_DIM
    hstate = [h_ref[q] for q in range(npair)]
    ys = [None] * len(chunks)
    for c in range(nc):
        for q in range(npair):
            u = q * nc + c
            res = _dot(jnp.concatenate([rq[u], gmat[u]], axis=0).astype(BF16),
                       hstate[q].astype(BF16))
            ys[u] = res[:L] + y0[u]
            hstate[q] = res[L:] + cmat[u]
    for q in range(npair):
        h_ref[q] = hstate[q]
    y = jnp.concatenate(ys, axis=0)
    mean = head_sum_all(y) * inv_n
    yc = y - mean
    var = head_sum_all(yc * yc) * inv_n
    yn = yc * lax.rsqrt(var + GN_EPS)
    for u in chunks:
        rows = pl.ds((u % nc) * L, L)
        o_ref[rows, lanes_of(u)] = (
            (yn[u * L:(u + 1) * L] * lnw_ref[:, lanes_of(u)] + lnb_ref[:, lanes_of(u)] + bonus[u])
            * g_ref[rows, lanes_of(u)]).astype(BF16)


def _scan(P, col_r, col_v, mu_r, mu_v, arrs, rk, lnw, lnb, batch, seq, nc=4, npair=4):
    k, kk, b, lw, g = [a.reshape(batch, seq, a.shape[1]) for a in arrs]
    P3 = P.reshape(batch, seq, P.shape[1])
    C = k.shape[2]
    width = LANES * npair
    rows = CHUNK * nc
    blk = pl.BlockSpec((None, rows, width), lambda bi, p, c: (bi, c, p))
    par = pl.BlockSpec((1, width), lambda bi, p, c: (0, p))

    def raw(col):
        return pl.BlockSpec((None, rows, width), lambda bi, p, c: (bi, c, col // width + p))

    def raw_prev(col):
        return pl.BlockSpec((None, 8, width),
                            lambda bi, p, c: (bi, jnp.maximum(c * (rows // 8) - 1, 0),
                                              col // width + p))

    out = pl.pallas_call(
        functools.partial(_scan_kernel, nc=nc, npair=npair),
        out_shape=jax.ShapeDtypeStruct((batch, seq, C), BF16),
        grid=(batch, C // width, seq // rows),
        in_specs=[raw(col_r), raw(col_v), raw_prev(col_r), raw_prev(col_v), par, par]
                 + [blk] * 5 + [par, par, par],
        out_specs=blk,
        scratch_shapes=[pltpu.VMEM((npair, LANES, LANES), F32)],
        compiler_params=_cparams(("parallel", "parallel", "arbitrary")),
        name="rwkv_scan",
    )(P3, P3, P3, P3, mu_r, mu_v, k, kk, b, lw, g, rk, lnw, lnb)
    return out.reshape(batch * seq, C)


def _sb_kernel(q_ref, k_ref, v_ref, o_ref, *, npair, nq):
    tq = tk = SB_BLOCK
    nu = SB_BLOCKS_PER_STEP
    i = pl.program_id(2)
    lane = lax.broadcasted_iota(jnp.int32, (1, LANES), 1)
    head0 = lane < HEAD_DIM
    scale = jnp.asarray(1.0 / math.sqrt(HEAD_DIM), BF16)
    streams = [(s, hp) for s in range(nq) for hp in range(npair)]
    ns = len(streams)
    q = [q_ref[s * tq:(s + 1) * tq, hp * LANES:(hp + 1) * LANES] * scale for (s, hp) in streams]
    r2 = lax.broadcasted_iota(jnp.int32, (2 * tk, 2 * tk), 0)
    c2 = lax.broadcasted_iota(jnp.int32, (2 * tk, 2 * tk), 1)
    after01 = jnp.logical_and(r2 > c2, (r2 < tk) == (c2 < tk)).astype(BF16)
    key_minus_query = (lax.broadcasted_iota(jnp.int32, (tq, 2 * tk), 1) % tk
                       - lax.broadcasted_iota(jnp.int32, (tq, 2 * tk), 0))
    units = [(p, u) for p in range(ns) for u in range(nu)]

    def by_head_rows(x):
        zero = jnp.zeros_like(x)
        return jnp.concatenate([jnp.where(head0, x, zero), jnp.where(head0, zero, x)], axis=0)

    def both(c0, c1):
        return jnp.concatenate([jnp.broadcast_to(c0, (tq, tk)), jnp.broadcast_to(c1, (tq, tk))],
                               axis=1)

    def body(carry):
        jt, _, offs, accs = carry
        z, sp, log_keep, mask, vb = {}, {}, {}, {}, {}
        for (p, u) in units:
            s, hp = streams[p]
            jb = jt - (nq - 1 - s) - u
            start = pl.multiple_of(jnp.maximum(jb, 0) * tk, tk)
            lanes = slice(hp * LANES, (hp + 1) * LANES)
            kb = by_head_rows(k_ref[pl.ds(start, tk), lanes])
            vb[p, u] = by_head_rows(v_ref[pl.ds(start, tk), lanes])
            zu = _dot_nt(q[p], kb)
            if u > 0:
                zu = zu + jnp.where(jb >= 0, 0.0, -1e30)
            spu = jnp.maximum(zu, 0.0) + jnp.log(1.0 + jnp.exp(-jnp.abs(zu)))
            z[p, u], sp[p, u] = zu, spu
            if u == 0:
                q_start = (i * nq + s) * tq
                mask[p, u] = key_minus_query < jnp.where(jb >= 0, q_start - jb * tk, -tq)
                log_keep[p, u] = jnp.where(mask[p, u], -spu, 0.0)
            else:
                log_keep[p, u] = -spu
        parts = []
        for key in units:
            parts.extend(_split2(log_keep[key]))
        loc = _dot(jnp.concatenate(parts, axis=0), after01)
        offs, accs = list(offs), list(accs)
        for n, (p, u) in enumerate(units):
            local = loc[2 * n * tq:(2 * n + 1) * tq] + loc[(2 * n + 1) * tq:(2 * n + 2) * tq]
            later = local + both(offs[2 * p], offs[2 * p + 1])
            att = jnp.exp(z[p, u] - sp[p, u] + later)
            if u == 0:
                att = jnp.where(mask[p, u], att, 0.0)
            accs[p] = accs[p] + _dot(att.astype(BF16), vb[p, u])
            offs[2 * p] = offs[2 * p] + (local[:, 0:1] + log_keep[p, u][:, 0:1])
            offs[2 * p + 1] = offs[2 * p + 1] + (local[:, tk:tk + 1] + log_keep[p, u][:, tk:tk + 1])
        top = offs[0]
        for o in offs[1:]:
            top = jnp.maximum(top, o)
        alive = (jnp.max(top) > -SB_UNDERFLOW).astype(jnp.int32)
        return jt - nu, alive, tuple(offs), tuple(accs)

    def cond(carry):
        jt, alive = carry[0], carry[1]
        return jnp.logical_and(jt >= 0, alive > 0)

    init = (i * nq + nq - 1, jnp.int32(1),
            tuple(jnp.zeros((tq, 1), F32) for _ in range(2 * ns)),
            tuple(jnp.zeros((tq, LANES), F32) for _ in range(ns)))
    accs = lax.while_loop(cond, body, init)[3]
    for p, (s, hp) in enumerate(streams):
        o_ref[s * tq:(s + 1) * tq, hp * LANES:(hp + 1) * LANES] = accs[p].astype(BF16)


def _stick_breaking(qkv, width, npair=2, nq=2):
    B, S, _ = qkv.shape
    w = LANES * npair
    nblk = width // w
    rows = SB_BLOCK * nq
    out = pl.pallas_call(
        functools.partial(_sb_kernel, npair=npair, nq=nq),
        out_shape=jax.ShapeDtypeStruct((B, S, width), BF16),
        grid=(B, nblk, S // rows),
        in_specs=[pl.BlockSpec((None, rows, w), lambda b, p, i: (b, i, p)),
                  pl.BlockSpec((None, S, w), lambda b, p, i: (b, 0, nblk + p)),
                  pl.BlockSpec((None, S, w), lambda b, p, i: (b, 0, 2 * nblk + p))],
        out_specs=pl.BlockSpec((None, rows, w), lambda b, p, i: (b, i, p)),
        compiler_params=_cparams(("parallel", "parallel", "arbitrary")),
        name="stick_breaking",
    )(qkv, qkv, qkv)
    return out.reshape(B * S, width)


def _merge_kernel(ya_ref, yb_ref, ga_ref, gb_ref, x_ref, mod_ref, g1_ref,
                  g2_ref, wua_ref, wub_ref, wo_ref, o_ref, hf_ref, *, d):
    ua = _dot(ya_ref[...], wua_ref[...])
    ub = _dot(yb_ref[...], wub_ref[...])
    merged = jax.nn.sigmoid(ga_ref[...]) * ua + jax.nn.sigmoid(gb_ref[...]) * ub
    mix = _dot(merged.astype(BF16), wo_ref[...])
    gate_m = mod_ref[:, 2 * d:3 * d]
    x1 = x_ref[...] + gate_m * _rms(mix, g1_ref[...])
    o_ref[...] = x1
    shift_f = mod_ref[:, 3 * d:4 * d]
    scale_f = mod_ref[:, 4 * d:5 * d]
    hf_ref[...] = (_rms(x1, g2_ref[...]) * (1.0 + scale_f) + shift_f).astype(BF16)


def _merge(ya, yb, P, x2, mod3, g1, g2, wua, wub, wo, seq, tm=256):
    T, D = x2.shape
    C = ya.shape[1]

    def const(shape):
        return pl.BlockSpec(shape, lambda i: (0, 0), pipeline_mode=pl.Buffered(1))

    return pl.pallas_call(
        functools.partial(_merge_kernel, d=D),
        out_shape=[jax.ShapeDtypeStruct((T, D), F32), jax.ShapeDtypeStruct((T, D), BF16)],
        grid=(T // tm,),
        in_specs=[pl.BlockSpec((tm, C), lambda i: (i, 0)),
                  pl.BlockSpec((tm, C), lambda i: (i, 0)),
                  pl.BlockSpec((tm, D), lambda i: (i, 0)),
                  pl.BlockSpec((tm, D), lambda i: (i, 1)),
                  pl.BlockSpec((tm, D), lambda i: (i, 0)),
                  pl.BlockSpec((None, 1, mod3.shape[2]), lambda i: ((i * tm) // seq, 0, 0)),
                  pl.BlockSpec((1, D), lambda i: (0, 0)),
                  pl.BlockSpec((1, D), lambda i: (0, 0)),
                  const((C, D)), const((C, D)), const((D, D))],
        out_specs=[pl.BlockSpec((tm, D), lambda i: (i, 0)), pl.BlockSpec((tm, D), lambda i: (i, 0))],
        compiler_params=_cparams(("parallel",)),
        name="merge_out",
    )(ya, yb, P, P, x2, mod3, g1, g2, wua, wub, wo)


def _mlp_kernel(x_ref, hf_ref, mod_ref, g3_ref, w1_ref, w2_ref, o_ref, acc_ref, *, d):
    j = pl.program_id(1)

    @pl.when(j == 0)
    def _():
        acc_ref[...] = jnp.zeros_like(acc_ref)

    u = jnp.maximum(_dot(hf_ref[...], w1_ref[...]), 0.0)
    acc_ref[...] += _dot((u * u).astype(BF16), w2_ref[...])

    @pl.when(j == pl.num_programs(1) - 1)
    def _():
        gate_f = mod_ref[:, 5 * d:6 * d]
        o_ref[...] = x_ref[...] + gate_f * _rms(acc_ref[...], g3_ref[...])


def _mlp(x2, hf, mod3, g3, w1, w2, seq, tm=512, tf=1024):
    T, D = x2.shape
    F = w1.shape[1]
    return pl.pallas_call(
        functools.partial(_mlp_kernel, d=D),
        out_shape=jax.ShapeDtypeStruct((T, D), F32),
        grid=(T // tm, F // tf),
        in_specs=[pl.BlockSpec((tm, D), lambda i, j: (i, 0)),
                  pl.BlockSpec((tm, D), lambda i, j: (i, 0)),
                  pl.BlockSpec((None, 1, mod3.shape[2]), lambda i, j: ((i * tm) // seq, 0, 0)),
                  pl.BlockSpec((1, D), lambda i, j: (0, 0)),
                  pl.BlockSpec((D, tf), lambda i, j: (0, j)),
                  pl.BlockSpec((tf, D), lambda i, j: (j, 0))],
        out_specs=pl.BlockSpec((tm, D), lambda i, j: (i, 0)),
        scratch_shapes=[pltpu.VMEM((tm, D), F32)],
        compiler_params=_cparams(("parallel", "arbitrary")),
        name="mlp",
    )(x2, hf, mod3, g3, w1, w2)


def _pad_rows(w, offset, total):
    return jnp.zeros((total, w.shape[1]), w.dtype).at[offset:offset + w.shape[0]].set(w)


def kernel(x, c, w_ada, b_ada, norm_g, w_in, mu_shift, w0, w2, a0, a2, g2, k_k, k_a,
           r_k, ln_x_w, ln_x_b, w_up_rwkv, w_up_sb, w_out, w_mlp_in, w_mlp_out):
    B, S, D = x.shape
    depth = w_in.shape[0]
    C = w0.shape[1]
    W = w_up_sb.shape[1]
    rwkv_cols = 3 * C + DECAY_LORA + ICLR_LORA + GATE_LORA
    sb_cols = 3 * W
    n_lora = DECAY_LORA + ICLR_LORA + GATE_LORA
    assert C == 1024 and W == 1024 and D == 2 * C and S % (2 * CHUNK) == 0

    hi = lax.broadcasted_iota(jnp.int32, (LANES, LANES), 0) // HEAD_DIM
    hj = lax.broadcasted_iota(jnp.int32, (LANES, LANES), 1) // HEAD_DIM
    bd = (hi == hj).astype(BF16)

    x2 = x.reshape(B * S, D)
    for l in range(depth):
        mod = _ada(c, w_ada[l], b_ada[l])
        mod3 = mod.reshape(B, 1, mod.shape[1])

        g_off = rwkv_cols + sb_cols
        n_main_cols = 2 * D + 3 * C + LORA_PAD
        w_perm = _regroup_rows(
            w_in[l].T,
            [(g_off, 2 * D), (0, 3 * C), (3 * C, LORA_PAD), (rwkv_cols, sb_cols)],
            zero_lo=2 * D + 3 * C + n_lora, zero_hi=n_main_cols)
        P, qkv = _inproj(x2, mod3, norm_g[l, 0].reshape(1, D), w_perm, n_main_cols, S)

        mu = mu_shift[l]
        consts = dict(
            col_k=2 * D // C + 1, col_l=(2 * D + 3 * C) // LORA_PAD,
            mu_k=mu[None, C:2 * C],
            mu_l=jnp.pad(mu[None, 3 * C:], ((0, 0), (0, LORA_PAD - n_lora))),
            w0=w0[l][None], a0=a0[l][None], k_k=k_k[l][None], k_a=k_a[l][None],
            w2p=_pad_rows(w2[l], 0, LORA_PAD).astype(BF16),
            a2p=_pad_rows(a2[l], DECAY_LORA, LORA_PAD).astype(BF16),
            g2p=_pad_rows(g2[l], DECAY_LORA + ICLR_LORA, LORA_PAD).astype(BF16),
            bd=bd)
        prep = _prep(P, S, consts)
        ya = _scan(P, 2 * D, 2 * D + 2 * C, mu[None, :C], mu[None, 2 * C:3 * C], prep,
                   r_k[l].reshape(1, C), ln_x_w[l][None], ln_x_b[l][None], B, S)

        yb = _stick_breaking(qkv.reshape(B, S, 3 * W), W)

        x2, hf = _merge(ya, yb, P, x2, mod3, norm_g[l, 1].reshape(1, D),
                        norm_g[l, 2].reshape(1, D), w_up_rwkv[l].astype(BF16),
                        w_up_sb[l].astype(BF16), w_out[l].astype(BF16), S)
        x2 = _mlp(x2, hf, mod3, norm_g[l, 3].reshape(1, D),
                  w_mlp_in[l].astype(BF16), w_mlp_out[l].astype(BF16), S)
    return x2.reshape(B, S, D)
```

```python
import functools
import math

import jax
import jax.numpy as jnp
from jax import lax
from jax.experimental import pallas as pl
from jax.experimental.pallas import tpu as pltpu

F32 = jnp.float32
BF16 = jnp.bfloat16

HEAD_DIM = 64
LANES = 128
DECAY_LORA = 64
ICLR_LORA = 64
GATE_LORA = 160
LORA_PAD = 512
NORM_EPS = 1e-6
GN_EPS = 64e-5
CHUNK = 128
SB_BLOCK = 128
SB_BLOCKS_PER_STEP = 3
SB_UNDERFLOW = 104.0
VMEM_LIMIT = 56 * 1024 * 1024


def _cparams(sem, vmem=VMEM_LIMIT):
    return pltpu.CompilerParams(dimension_semantics=sem, vmem_limit_bytes=vmem)


def _dot(a, b):
    return jnp.dot(a, b, preferred_element_type=F32)


def _dot_nt(a, b):
    return lax.dot_general(a, b, (((1,), (1,)), ((), ())), preferred_element_type=F32)


def _dot_tn(a, b):
    return lax.dot_general(a, b, (((0,), (0,)), ((), ())), preferred_element_type=F32)


def _split2(x):
    hi = x.astype(BF16)
    lo = (x - hi.astype(F32)).astype(BF16)
    return hi, lo


def _dot_exact01(x, m01):
    n = x.shape[0]
    res = _dot(jnp.concatenate(_split2(x), axis=0), m01)
    return res[:n] + res[n:]


def _exact01_dot(m01, x):
    n = x.shape[1]
    res = _dot(m01, jnp.concatenate(_split2(x), axis=1))
    return res[:, :n] + res[:, n:]


def _head_sum(x, bd):
    return _dot_exact01(x, bd)


def _rms(x, g):
    return x * lax.rsqrt(jnp.mean(x * x, axis=-1, keepdims=True) + NORM_EPS) * g


def _ada_kernel(c_ref, w_ref, b_ref, o_ref):
    c = c_ref[...]
    s = c * jax.nn.sigmoid(c)
    o_ref[...] = _dot(s.astype(BF16), w_ref[...].astype(BF16)) + b_ref[...]


def _ada(c, w_ada, b_ada, tn=1536):
    B, D = c.shape
    N = w_ada.shape[1]
    rows = 8
    c_pad = jnp.zeros((rows, D), F32).at[:B].set(c)
    out = pl.pallas_call(
        _ada_kernel,
        out_shape=jax.ShapeDtypeStruct((rows, N), F32),
        grid=(N // tn,),
        in_specs=[pl.BlockSpec((rows, D), lambda j: (0, 0)),
                  pl.BlockSpec((D, tn), lambda j: (0, j)),
                  pl.BlockSpec((1, tn), lambda j: (0, j))],
        out_specs=pl.BlockSpec((rows, tn), lambda j: (0, j)),
        compiler_params=_cparams(("parallel",)),
        name="ada",
    )(c_pad, w_ada, b_ada.reshape(1, N))
    return out[:B]


def _regroup_kernel(offs_ref, w_ref, o_ref, *, zero_lo, zero_hi, rows):
    del offs_ref
    out_row = pl.program_id(0) * rows + lax.broadcasted_iota(jnp.int32, (rows, 1), 0)
    pad = jnp.logical_and(out_row >= zero_lo, out_row < zero_hi)
    o_ref[...] = jnp.where(pad, 0.0, w_ref[...]).astype(BF16)


def _regroup_rows(wt, groups, zero_lo, zero_hi, rows=256):
    D = wt.shape[1]
    sub = 8
    assert all(s % sub == 0 and n % rows == 0 for s, n in groups)
    offs = jnp.asarray([(s + r) // sub for (s, n) in groups for r in range(0, n, rows)], jnp.int32)
    n_out = sum(n for _, n in groups)
    return pl.pallas_call(
        functools.partial(_regroup_kernel, zero_lo=zero_lo, zero_hi=zero_hi, rows=rows),
        out_shape=jax.ShapeDtypeStruct((n_out, D), BF16),
        grid_spec=pltpu.PrefetchScalarGridSpec(
            num_scalar_prefetch=1, grid=(n_out // rows,),
            in_specs=[pl.BlockSpec((pl.Element(rows), pl.Element(D)),
                                   lambda t, offs: (offs[t] * sub, 0))],
            out_specs=pl.BlockSpec((rows, D), lambda t, offs: (t, 0))),
        compiler_params=_cparams(("parallel",)),
        name="regroup_w_in",
    )(offs, wt)


def _inproj_kernel(x0_ref, mod0_ref, xn_ref, modn_ref, g_ref, w_ref, o_ref, sb_ref, h_ref,
                   *, d, n_main, chunk):
    i = pl.program_id(0)
    j = pl.program_id(1)
    slot = i % 2
    tm = xn_ref.shape[0]

    def modulated(x, mod_ref):
        shift = mod_ref[:, 0:d]
        scale = mod_ref[:, d:2 * d]
        return (_rms(x, g_ref[...]) * (1.0 + scale) + shift).astype(BF16)

    @pl.when(jnp.logical_and(i == 0, j == 0))
    def _():
        h_ref[0] = modulated(x0_ref[...], mod0_ref)

    def prepare_next_rows():
        r0 = pl.multiple_of(jnp.minimum(j * chunk, tm - chunk), 16)
        h_ref[1 - slot, pl.ds(r0, chunk), :] = modulated(xn_ref[pl.ds(r0, chunk), :], modn_ref)

    @pl.when(j < n_main)
    def _():
        o_ref[...] = _dot_nt(h_ref[slot], w_ref[...])
        prepare_next_rows()

    @pl.when(j >= n_main)
    def _():
        sb_ref[...] = _dot_nt(h_ref[slot], w_ref[...]).astype(BF16)
        prepare_next_rows()


def _inproj(x2, mod3, g0, w_t, n_main_cols, seq, tm=1024, tn=768):
    T, D = x2.shape
    N = w_t.shape[0]
    n_main = n_main_cols // tn
    n_i, n_j = T // tm, N // tn
    chunk = 16 * pl.cdiv(pl.cdiv(tm, n_j), 16)

    def nxt(i):
        return jnp.minimum(i + 1, n_i - 1)

    return pl.pallas_call(
        functools.partial(_inproj_kernel, d=D, n_main=n_main, chunk=chunk),
        out_shape=[jax.ShapeDtypeStruct((T, n_main_cols), F32),
                   jax.ShapeDtypeStruct((T, N - n_main_cols), BF16)],
        grid=(n_i, n_j),
        in_specs=[pl.BlockSpec((tm, D), lambda i, j: (0, 0), pipeline_mode=pl.Buffered(1)),
                  pl.BlockSpec((None, 1, mod3.shape[2]), lambda i, j: (0, 0, 0)),
                  pl.BlockSpec((tm, D), lambda i, j: (nxt(i), 0)),
                  pl.BlockSpec((None, 1, mod3.shape[2]), lambda i, j: ((nxt(i) * tm) // seq, 0, 0)),
                  pl.BlockSpec((1, D), lambda i, j: (0, 0)),
                  pl.BlockSpec((tn, D), lambda i, j: (j, 0))],
        out_specs=[pl.BlockSpec((tm, tn), lambda i, j: (i, jnp.minimum(j, n_main - 1))),
                   pl.BlockSpec((tm, tn), lambda i, j: (i, jnp.maximum(j - n_main, 0)))],
        scratch_shapes=[pltpu.VMEM((2, tm, D), BF16)],
        compiler_params=_cparams(("arbitrary", "arbitrary")),
        name="inproj",
    )(x2, mod3, x2, mod3, g0, w_t)


def _token_shift_mix(cur_ref, prev_ref, mu_ref, first):
    cur = cur_ref[...]
    row = lax.broadcasted_iota(jnp.int32, (cur.shape[0], 1), 0)
    prev_row = jnp.where(first, 0.0, prev_ref[7:8, :])
    shifted = jnp.where(row == 0, prev_row, pltpu.roll(cur, 1, 0))
    return cur + (shifted - cur) * mu_ref[...]


def _prep_kernel(k_ref, l_ref, kp_ref, lp_ref, muk_ref, mul_ref, w0_ref, a0_ref, kk_ref, ka_ref,
                 w2_ref, a2_ref, g2_ref, bd_ref,
                 ko_ref, kko_ref, bo_ref, lwo_ref, go_ref, *, tm, seq):
    first = (pl.program_id(0) * tm) % seq == 0
    k = _token_shift_mix(k_ref, kp_ref, muk_ref, first)
    lo = _token_shift_mix(l_ref, lp_ref, mul_ref, first)

    dw = _dot(jnp.tanh(lo).astype(BF16), w2_ref[...])
    da = _dot(lo.astype(BF16), a2_ref[...])
    g = _dot(jax.nn.sigmoid(lo).astype(BF16), g2_ref[...])

    w_log = -jax.nn.softplus(-(w0_ref[...] + dw)) - 0.5
    lw = -jnp.exp(w_log)
    a = jax.nn.sigmoid(a0_ref[...] + da)
    kx = k * kk_ref[...]
    bd = bd_ref[...]
    n_blk = kx.shape[1] // LANES
    sq = kx * kx
    ssum = jnp.concatenate(
        [_head_sum(sq[:, c * LANES:(c + 1) * LANES], bd) for c in range(n_blk)], axis=1)
    kk = kx * lax.rsqrt(jnp.maximum(ssum, 1e-24))
    k2 = k * (1.0 + (a - 1.0) * ka_ref[...])

    ko_ref[...] = k2
    kko_ref[...] = kk
    bo_ref[...] = kk * a
    lwo_ref[...] = lw
    go_ref[...] = g


def _prep(P, seq, consts, tm=512):
    T = P.shape[0]
    C = consts["w0"].shape[1]
    col_k, col_l = consts["col_k"], consts["col_l"]

    def cur(col, w):
        return pl.BlockSpec((tm, w), lambda i: (i, col))

    def prev(col, w):
        return pl.BlockSpec((8, w), lambda i: (jnp.maximum(i * (tm // 8) - 1, 0), col))

    def const(shape):
        return pl.BlockSpec(shape, lambda i: (0, 0))

    out_sd = jax.ShapeDtypeStruct((T, C), F32)
    outs = pl.pallas_call(
        functools.partial(_prep_kernel, tm=tm, seq=seq),
        out_shape=[out_sd] * 5,
        grid=(T // tm,),
        in_specs=[cur(col_k, C), cur(col_l, LORA_PAD), prev(col_k, C), prev(col_l, LORA_PAD),
                  const((1, C)), const((1, LORA_PAD)),
                  const((1, C)), const((1, C)), const((1, C)), const((1, C)),
                  const((LORA_PAD, C)), const((LORA_PAD, C)), const((LORA_PAD, C)),
                  const((LANES, LANES))],
        out_specs=[pl.BlockSpec((tm, C), lambda i: (i, 0))] * 5,
        compiler_params=_cparams(("parallel",)),
        name="rwkv_prep",
    )(P, P, P, P, consts["mu_k"], consts["mu_l"],
      consts["w0"], consts["a0"], consts["k_k"], consts["k_a"],
      consts["w2p"], consts["a2p"], consts["g2p"], consts["bd"])
    return outs


def _scan_kernel(r_ref, v_ref, rp_ref, vp_ref, mur_ref, muv_ref, k_ref, kk_ref, b_ref, lw_ref,
                 g_ref, rk_ref, lnw_ref, lnb_ref, o_ref, h_ref, *, nc, npair):
    L = CHUNK
    half = L // 2

    @pl.when(pl.program_id(2) == 0)
    def _():
        h_ref[...] = jnp.zeros_like(h_ref)

    ri = lax.broadcasted_iota(jnp.int32, (L, L), 0)
    ci = lax.broadcasted_iota(jnp.int32, (L, L), 1)
    lower_incl = ri >= ci
    lower_strict = ri > ci
    eye = ri == ci
    same_head = (ri < HEAD_DIM) == (ci < HEAD_DIM)
    tri01 = lower_incl.astype(BF16)
    eye_f = eye.astype(F32)
    lane = lax.broadcasted_iota(jnp.int32, (1, LANES), 1)
    head0 = lane < HEAD_DIM
    lane2 = lax.broadcasted_iota(jnp.int32, (1, 2 * LANES), 1)
    head0_2 = (lane2 % LANES) < HEAD_DIM
    tri_stack = jnp.concatenate([lower_strict, lower_incl], axis=0)
    chunks = range(nc * npair)
    pairs = [(c, h) for c in chunks for h in range(2)]

    def sel(x0, x1):
        return jnp.where(head0 if x0.shape[1] == LANES else head0_2, x0, x1)

    def lanes_of(u):
        return slice((u // nc) * LANES, (u // nc + 1) * LANES)

    def ld(ref, u):
        return ref[pl.ds((u % nc) * L, L), lanes_of(u)]

    first = pl.program_id(2) == 0
    r_tile = _token_shift_mix(r_ref, rp_ref, mur_ref, first)
    v_tile = _token_shift_mix(v_ref, vp_ref, muv_ref, first)
    r = [r_tile[(u % nc) * L:(u % nc + 1) * L, lanes_of(u)] for u in chunks]
    v = [v_tile[(u % nc) * L:(u % nc + 1) * L, lanes_of(u)] for u in chunks]
    k = [ld(k_ref, c) for c in chunks]
    kk = [ld(kk_ref, c) for c in chunks]
    b = [ld(b_ref, c) for c in chunks]
    lw = [ld(lw_ref, c) for c in chunks]

    cum = [_exact01_dot(tri01, lw[c]) for c in chunks]
    lhs_mid, rhs_mid, kd_bf, r_dec, k_end, b_end, p_end, v_bf = [], [], [], [], [], [], [], []
    for c in chunks:
        mid = cum[c][half - 1:half, :]
        end = cum[c][L - 1:L, :]
        r_mid = r[c] * jnp.exp(cum[c] - mid)
        kk_mid = kk[c] * jnp.exp(cum[c] - lw[c] - mid)
        e_back = jnp.exp(mid - cum[c])
        lhs_mid.append(jnp.concatenate([kk_mid, r_mid], axis=0))
        rhs_mid.append(jnp.concatenate([b[c] * e_back, k[c] * e_back], axis=0).astype(BF16))
        kd_bf.append((kk[c] * jnp.exp(cum[c] - lw[c])).astype(BF16))
        r_dec.append(r[c] * jnp.exp(cum[c]))
        e_end = jnp.exp(end - cum[c])
        k_end.append((k[c] * e_end).astype(BF16))
        b_end.append((b[c] * e_end).astype(BF16))
        p_end.append(jnp.exp(end))
        v_bf.append(v[c].astype(BF16))

    aa = {}
    for (c, h) in pairs:
        hm = head0 if h == 0 else jnp.logical_not(head0)
        aa[c, h] = _dot_nt(jnp.where(hm, lhs_mid[c], 0.0).astype(BF16), rhs_mid[c])
    def side(f):
        return jnp.concatenate([f(0), f(1)], axis=1)

    tri2 = jnp.concatenate([tri_stack, tri_stack], axis=1)
    strict2 = jnp.concatenate([lower_strict, lower_strict], axis=1)
    incl2 = jnp.concatenate([lower_incl, lower_incl], axis=1)
    def by_head_rows(x):
        hm = head0 if x.shape[1] == LANES else head0_2
        zero = jnp.zeros_like(x)
        return jnp.concatenate([jnp.where(hm, x, zero), jnp.where(hm, zero, x)], axis=0)

    a_kb = [jnp.where(strict2, side(lambda h: aa[c, h][:L, :L]), 0.0) for c in chunks]
    a_rb = [jnp.where(incl2, side(lambda h: aa[c, h][L:, :L]), 0.0).astype(BF16) for c in chunks]
    a_xk = [jnp.where(tri2, side(lambda h: aa[c, h][:, L:]), 0.0).astype(BF16) for c in chunks]
    avr = [_dot(a_xk[c], by_head_rows(v_bf[c])) for c in chunks]

    hl = L // 2
    blk_r = lax.broadcasted_iota(jnp.int32, (2 * L, 2 * L), 0) // hl
    blk_c = lax.broadcasted_iota(jnp.int32, (2 * L, 2 * L), 1) // hl
    diag4 = blk_r == blk_c
    below4 = jnp.logical_and(blk_r == blk_c + 1, blk_c % 2 == 0)
    eye4 = (lax.broadcasted_iota(jnp.int32, (hl, 2 * L), 0)
            == lax.broadcasted_iota(jnp.int32, (hl, 2 * L), 1) % hl).astype(F32)

    def tile4(x, keep):
        return jnp.where(keep, jnp.concatenate([x, x, x, x], axis=0), jnp.zeros((), x.dtype))

    n_lvl = int(math.log2(hl)) - 1
    p, t, a21 = [], [], []
    for c in chunks:
        pd = jnp.where(head0_2, a_kb[c][:hl], a_kb[c][hl:])
        a21.append(jnp.where(head0_2, a_kb[c][hl:], 0.0).astype(BF16))
        pd_bf = pd.astype(BF16)
        p.append(_dot(pd_bf, tile4(pd_bf, diag4)))
        t.append(eye4 - pd)
    for lvl in range(n_lvl):
        for c in chunks:
            p_bd = tile4(p[c].astype(BF16), diag4)
            if lvl + 1 < n_lvl:
                tp = _dot(jnp.concatenate([t[c], p[c]], axis=0).astype(BF16), p_bd)
                t[c] = t[c] + tp[:hl]
                p[c] = tp[hl:]
            else:
                t[c] = t[c] + _dot(t[c].astype(BF16), p_bd)
    for c in chunks:
        td_bf = t[c].astype(BF16)
        y = _dot(a21[c], tile4(jnp.where(head0_2, td_bf, jnp.zeros((), BF16)), diag4))
        z = _dot(td_bf, tile4(y.astype(BF16), below4))
        t[c] = jnp.concatenate([jnp.where(head0_2, t[c], 0.0), jnp.where(head0_2, -z, t[c])], axis=0)

    rq, y0, gmat, cmat = [], [], [], []
    for c in chunks:
        av = avr[c][:L]
        arkv = avr[c][L:]
        x = jnp.concatenate([kd_bf[c], av.astype(BF16)], axis=1)
        w_bf = _dot(t[c].astype(BF16), by_head_rows(x)).astype(BF16)
        aw = _dot(a_rb[c], by_head_rows(w_bf))
        bw = _dot_tn(b_end[c], w_bf)
        kv = _dot_tn(k_end[c], v_bf[c])
        rq.append(r_dec[c] - aw[:, :LANES])
        y0.append(arkv - aw[:, LANES:])
        gmat.append(eye_f * p_end[c] - jnp.where(same_head, bw[:, :LANES], 0.0))
        cmat.append(jnp.where(same_head, kv - bw[:, LANES:], 0.0))

    bd2 = ((lax.broadcasted_iota(jnp.int32, (2 * LANES, 2 * LANES), 0) // HEAD_DIM)
           == (lax.broadcasted_iota(jnp.int32, (2 * LANES, 2 * LANES), 1) // HEAD_DIM)).astype(BF16)

    def head_sum_all(x):
        n = x.shape[0] // 2
        s2 = _dot_exact01(jnp.concatenate([x[:n], x[n:]], axis=1), bd2)
        return jnp.concatenate([s2[:, :LANES], s2[:, LANES:]], axis=0)

    rkr = head_sum_all(jnp.concatenate(
        [r[u] * k[u] * rk_ref[:, lanes_of(u)] for u in chunks], axis=0))
    bonus = [rkr[u * L:(u + 1) * L] * v[u] for u in chunks]

    inv_n = 1.0 / HEAD_DIM
    hstate = [h_ref[q] for q in range(npair)]
    ys = [None] * len(chunks)
    for c in range(nc):
        for q in range(npair):
            u = q * nc + c
            res = _dot(jnp.concatenate([rq[u], gmat[u]], axis=0).astype(BF16),
                       hstate[q].astype(BF16))
            ys[u] = res[:L] + y0[u]
            hstate[q] = res[L:] + cmat[u]
    for q in range(npair):
        h_ref[q] = hstate[q]
    y = jnp.concatenate(ys, axis=0)
    mean = head_sum_all(y) * inv_n
    yc = y - mean
    var = head_sum_all(yc * yc) * inv_n
    yn = yc * lax.rsqrt(var + GN_EPS)
    for u in chunks:
        rows = pl.ds((u % nc) * L, L)
        o_ref[rows, lanes_of(u)] = (
            (yn[u * L:(u + 1) * L] * lnw_ref[:, lanes_of(u)] + lnb_ref[:, lanes_of(u)] + bonus[u])
            * g_ref[rows, lanes_of(u)]).astype(BF16)


def _scan(P, col_r, col_v, mu_r, mu_v, arrs, rk, lnw, lnb, batch, seq, nc=4, npair=4):
    k, kk, b, lw, g = [a.reshape(batch, seq, a.shape[1]) for a in arrs]
    P3 = P.reshape(batch, seq, P.shape[1])
    C = k.shape[2]
    width = LANES * npair
    rows = CHUNK * nc
    blk = pl.BlockSpec((None, rows, width), lambda bi, p, c: (bi, c, p))
    par = pl.BlockSpec((1, width), lambda bi, p, c: (0, p))

    def raw(col):
        return pl.BlockSpec((None, rows, width), lambda bi, p, c: (bi, c, col // width + p))

    def raw_prev(col):
        return pl.BlockSpec((None, 8, width),
                            lambda bi, p, c: (bi, jnp.maximum(c * (rows // 8) - 1, 0),
                                              col // width + p))

    out = pl.pallas_call(
        functools.partial(_scan_kernel, nc=nc, npair=npair),
        out_shape=jax.ShapeDtypeStruct((batch, seq, C), BF16),
        grid=(batch, C // width, seq // rows),
        in_specs=[raw(col_r), raw(col_v), raw_prev(col_r), raw_prev(col_v), par, par]
                 + [blk] * 5 + [par, par, par],
        out_specs=blk,
        scratch_shapes=[pltpu.VMEM((npair, LANES, LANES), F32)],
        compiler_params=_cparams(("parallel", "parallel", "arbitrary")),
        name="rwkv_scan",
    )(P3, P3, P3, P3, mu_r, mu_v, k, kk, b, lw, g, rk, lnw, lnb)
    return out.reshape(batch * seq, C)


def _sb_kernel(q_ref, k_ref, v_ref, o_ref, *, npair, nq):
    tq = tk = SB_BLOCK
    nu = SB_BLOCKS_PER_STEP
    i = pl.program_id(2)
    lane = lax.broadcasted_iota(jnp.int32, (1, LANES), 1)
    head0 = lane < HEAD_DIM
    scale = jnp.asarray(1.0 / math.sqrt(HEAD_DIM), BF16)
    streams = [(s, hp) for s in range(nq) for hp in range(npair)]
    ns = len(streams)
    q = [q_ref[s * tq:(s + 1) * tq, hp * LANES:(hp + 1) * LANES] * scale for (s, hp) in streams]
    r2 = lax.broadcasted_iota(jnp.int32, (2 * tk, 2 * tk), 0)
    c2 = lax.broadcasted_iota(jnp.int32, (2 * tk, 2 * tk), 1)
    after01 = jnp.logical_and(r2 > c2, (r2 < tk) == (c2 < tk)).astype(BF16)
    key_minus_query = (lax.broadcasted_iota(jnp.int32, (tq, 2 * tk), 1) % tk
                       - lax.broadcasted_iota(jnp.int32, (tq, 2 * tk), 0))
    units = [(p, u) for p in range(ns) for u in range(nu)]

    def by_head_rows(x):
        zero = jnp.zeros_like(x)
        return jnp.concatenate([jnp.where(head0, x, zero), jnp.where(head0, zero, x)], axis=0)

    def both(c0, c1):
        return jnp.concatenate([jnp.broadcast_to(c0, (tq, tk)), jnp.broadcast_to(c1, (tq, tk))],
                               axis=1)

    def body(carry):
        jt, _, offs, accs = carry
        z, sp, log_keep, mask, vb = {}, {}, {}, {}, {}
        for (p, u) in units:
            s, hp = streams[p]
            jb = jt - (nq - 1 - s) - u
            start = pl.multiple_of(jnp.maximum(jb, 0) * tk, tk)
            lanes = slice(hp * LANES, (hp + 1) * LANES)
            kb = by_head_rows(k_ref[pl.ds(start, tk), lanes])
            vb[p, u] = by_head_rows(v_ref[pl.ds(start, tk), lanes])
            zu = _dot_nt(q[p], kb)
            if u > 0:
                zu = zu + jnp.where(jb >= 0, 0.0, -1e30)
            spu = jnp.maximum(zu, 0.0) + jnp.log(1.0 + jnp.exp(-jnp.abs(zu)))
            z[p, u], sp[p, u] = zu, spu
            if u == 0:
                q_start = (i * nq + s) * tq
                mask[p, u] = key_minus_query < jnp.where(jb >= 0, q_start - jb * tk, -tq)
                log_keep[p, u] = jnp.where(mask[p, u], -spu, 0.0)
            else:
                log_keep[p, u] = -spu
        parts = []
        for key in units:
            parts.extend(_split2(log_keep[key]))
        loc = _dot(jnp.concatenate(parts, axis=0), after01)
        offs, accs = list(offs), list(accs)
        for n, (p, u) in enumerate(units):
            local = loc[2 * n * tq:(2 * n + 1) * tq] + loc[(2 * n + 1) * tq:(2 * n + 2) * tq]
            later = local + both(offs[2 * p], offs[2 * p + 1])
            att = jnp.exp(z[p, u] - sp[p, u] + later)
            if u == 0:
                att = jnp.where(mask[p, u], att, 0.0)
            accs[p] = accs[p] + _dot(att.astype(BF16), vb[p, u])
            offs[2 * p] = offs[2 * p] + (local[:, 0:1] + log_keep[p, u][:, 0:1])
            offs[2 * p + 1] = offs[2 * p + 1] + (local[:, tk:tk + 1] + log_keep[p, u][:, tk:tk + 1])
        top = offs[0]
        for o in offs[1:]:
            top = jnp.maximum(top, o)
        alive = (jnp.max(top) > -SB_UNDERFLOW).astype(jnp.int32)
        return jt - nu, alive, tuple(offs), tuple(accs)

    def cond(carry):
        jt, alive = carry[0], carry[1]
        return jnp.logical_and(jt >= 0, alive > 0)

    init = (i * nq + nq - 1, jnp.int32(1),
            tuple(jnp.zeros((tq, 1), F32) for _ in range(2 * ns)),
            tuple(jnp.zeros((tq, LANES), F32) for _ in range(ns)))
    accs = lax.while_loop(cond, body, init)[3]
    for p, (s, hp) in enumerate(streams):
        o_ref[s * tq:(s + 1) * tq, hp * LANES:(hp + 1) * LANES] = accs[p].astype(BF16)


def _stick_breaking(qkv, width, npair=2, nq=2):
    B, S, _ = qkv.shape
    w = LANES * npair
    nblk = width // w
    rows = SB_BLOCK * nq
    out = pl.pallas_call(
        functools.partial(_sb_kernel, npair=npair, nq=nq),
        out_shape=jax.ShapeDtypeStruct((B, S, width), BF16),
        grid=(B, nblk, S // rows),
        in_specs=[pl.BlockSpec((None, rows, w), lambda b, p, i: (b, i, p)),
                  pl.BlockSpec((None, S, w), lambda b, p, i: (b, 0, nblk + p)),
                  pl.BlockSpec((None, S, w), lambda b, p, i: (b, 0, 2 * nblk + p))],
        out_specs=pl.BlockSpec((None, rows, w), lambda b, p, i: (b, i, p)),
        compiler_params=_cparams(("parallel", "parallel", "arbitrary")),
        name="stick_breaking",
    )(qkv, qkv, qkv)
    return out.reshape(B * S, width)


def _merge_kernel(ya_ref, yb_ref, ga_ref, gb_ref, x_ref, mod_ref, g1_ref,
                  g2_ref, wua_ref, wub_ref, wo_ref, o_ref, hf_ref, *, d):
    ua = _dot(ya_ref[...], wua_ref[...])
    ub = _dot(yb_ref[...], wub_ref[...])
    merged = jax.nn.sigmoid(ga_ref[...]) * ua + jax.nn.sigmoid(gb_ref[...]) * ub
    mix = _dot(merged.astype(BF16), wo_ref[...])
    gate_m = mod_ref[:, 2 * d:3 * d]
    x1 = x_ref[...] + gate_m * _rms(mix, g1_ref[...])
    o_ref[...] = x1
    shift_f = mod_ref[:, 3 * d:4 * d]
    scale_f = mod_ref[:, 4 * d:5 * d]
    hf_ref[...] = (_rms(x1, g2_ref[...]) * (1.0 + scale_f) + shift_f).astype(BF16)


def _merge(ya, yb, P, x2, mod3, g1, g2, wua, wub, wo, seq, tm=256):
    T, D = x2.shape
    C = ya.shape[1]

    def const(shape):
        return pl.BlockSpec(shape, lambda i: (0, 0), pipeline_mode=pl.Buffered(1))

    return pl.pallas_call(
        functools.partial(_merge_kernel, d=D),
        out_shape=[jax.ShapeDtypeStruct((T, D), F32), jax.ShapeDtypeStruct((T, D), BF16)],
        grid=(T // tm,),
        in_specs=[pl.BlockSpec((tm, C), lambda i: (i, 0)),
                  pl.BlockSpec((tm, C), lambda i: (i, 0)),
                  pl.BlockSpec((tm, D), lambda i: (i, 0)),
                  pl.BlockSpec((tm, D), lambda i: (i, 1)),
                  pl.BlockSpec((tm, D), lambda i: (i, 0)),
                  pl.BlockSpec((None, 1, mod3.shape[2]), lambda i: ((i * tm) // seq, 0, 0)),
                  pl.BlockSpec((1, D), lambda i: (0, 0)),
                  pl.BlockSpec((1, D), lambda i: (0, 0)),
                  const((C, D)), const((C, D)), const((D, D))],
        out_specs=[pl.BlockSpec((tm, D), lambda i: (i, 0)), pl.BlockSpec((tm, D), lambda i: (i, 0))],
        compiler_params=_cparams(("parallel",)),
        name="merge_out",
    )(ya, yb, P, P, x2, mod3, g1, g2, wua, wub, wo)


def _mlp_kernel(xp_ref, hf_ref, modp_ref, g3_ref, w1_ref, w2_ref, o_ref, acc_ref, *, d, rc):
    i = pl.program_id(0)
    j = pl.program_id(1)
    n = pl.num_programs(0) - 1
    slot = i % 2

    @pl.when(jnp.logical_and(i == 0, j == 0))
    def _():
        acc_ref[1] = jnp.zeros(acc_ref.shape[1:], F32)

    def finish_prev_rows():
        rows = pl.ds(pl.multiple_of(j * rc, rc), rc)
        gate_f = modp_ref[:, 5 * d:6 * d]
        o_ref[rows, :] = xp_ref[rows, :] + gate_f * _rms(acc_ref[1 - slot, rows, :], g3_ref[...])

    def partial_out():
        u = jnp.maximum(_dot(hf_ref[...], w1_ref[...]), 0.0)
        return _dot((u * u).astype(BF16), w2_ref[...])

    @pl.when(jnp.logical_and(i < n, j == 0))
    def _():
        finish_prev_rows()
        acc_ref[slot] = partial_out()

    @pl.when(jnp.logical_and(i < n, j > 0))
    def _():
        finish_prev_rows()
        acc_ref[slot] += partial_out()

    @pl.when(i == n)
    def _():
        finish_prev_rows()


def _mlp(x2, hf, mod3, g3, w1, w2, seq, tm=512, tf=1024):
    T, D = x2.shape
    F = w1.shape[1]
    n, nj = T // tm, F // tf
    assert tm % nj == 0 and (tm // nj) % 8 == 0

    def cur(i):
        return jnp.minimum(i, n - 1)

    def prv(i):
        return jnp.maximum(i - 1, 0)

    def ff(i, j):
        return jnp.where(i < n, j, nj - 1)

    return pl.pallas_call(
        functools.partial(_mlp_kernel, d=D, rc=tm // nj),
        out_shape=jax.ShapeDtypeStruct((T, D), F32),
        grid=(n + 1, nj),
        in_specs=[pl.BlockSpec((tm, D), lambda i, j: (prv(i), 0)),
                  pl.BlockSpec((tm, D), lambda i, j: (cur(i), 0)),
                  pl.BlockSpec((None, 1, mod3.shape[2]), lambda i, j: ((prv(i) * tm) // seq, 0, 0)),
                  pl.BlockSpec((1, D), lambda i, j: (0, 0)),
                  pl.BlockSpec((D, tf), lambda i, j: (0, ff(i, j))),
                  pl.BlockSpec((tf, D), lambda i, j: (ff(i, j), 0))],
        out_specs=pl.BlockSpec((tm, D), lambda i, j: (prv(i), 0)),
        scratch_shapes=[pltpu.VMEM((2, tm, D), F32)],
        compiler_params=_cparams(("arbitrary", "arbitrary")),
        name="mlp",
    )(x2, hf, mod3, g3, w1, w2)


def _pad_rows(w, offset, total):
    return jnp.zeros((total, w.shape[1]), w.dtype).at[offset:offset + w.shape[0]].set(w)


def kernel(x, c, w_ada, b_ada, norm_g, w_in, mu_shift, w0, w2, a0, a2, g2, k_k, k_a,
           r_k, ln_x_w, ln_x_b, w_up_rwkv, w_up_sb, w_out, w_mlp_in, w_mlp_out):
    B, S, D = x.shape
    depth = w_in.shape[0]
    C = w0.shape[1]
    W = w_up_sb.shape[1]
    rwkv_cols = 3 * C + DECAY_LORA + ICLR_LORA + GATE_LORA
    sb_cols = 3 * W
    n_lora = DECAY_LORA + ICLR_LORA + GATE_LORA
    assert C == 1024 and W == 1024 and D == 2 * C and S % (2 * CHUNK) == 0

    hi = lax.broadcasted_iota(jnp.int32, (LANES, LANES), 0) // HEAD_DIM
    hj = lax.broadcasted_iota(jnp.int32, (LANES, LANES), 1) // HEAD_DIM
    bd = (hi == hj).astype(BF16)

    x2 = x.reshape(B * S, D)
    for l in range(depth):
        mod = _ada(c, w_ada[l], b_ada[l])
        mod3 = mod.reshape(B, 1, mod.shape[1])

        g_off = rwkv_cols + sb_cols
        n_main_cols = 2 * D + 3 * C + LORA_PAD
        w_perm = _regroup_rows(
            w_in[l].T,
            [(g_off, 2 * D), (0, 3 * C), (3 * C, LORA_PAD), (rwkv_cols, sb_cols)],
            zero_lo=2 * D + 3 * C + n_lora, zero_hi=n_main_cols)
        P, qkv = _inproj(x2, mod3, norm_g[l, 0].reshape(1, D), w_perm, n_main_cols, S)

        mu = mu_shift[l]
        consts = dict(
            col_k=2 * D // C + 1, col_l=(2 * D + 3 * C) // LORA_PAD,
            mu_k=mu[None, C:2 * C],
            mu_l=jnp.pad(mu[None, 3 * C:], ((0, 0), (0, LORA_PAD - n_lora))),
            w0=w0[l][None], a0=a0[l][None], k_k=k_k[l][None], k_a=k_a[l][None],
            w2p=_pad_rows(w2[l], 0, LORA_PAD).astype(BF16),
            a2p=_pad_rows(a2[l], DECAY_LORA, LORA_PAD).astype(BF16),
            g2p=_pad_rows(g2[l], DECAY_LORA + ICLR_LORA, LORA_PAD).astype(BF16),
            bd=bd)
        prep = _prep(P, S, consts)
        ya = _scan(P, 2 * D, 2 * D + 2 * C, mu[None, :C], mu[None, 2 * C:3 * C], prep,
                   r_k[l].reshape(1, C), ln_x_w[l][None], ln_x_b[l][None], B, S)

        yb = _stick_breaking(qkv.reshape(B, S, 3 * W), W)

        x2, hf = _merge(ya, yb, P, x2, mod3, norm_g[l, 1].reshape(1, D),
                        norm_g[l, 2].reshape(1, D), w_up_rwkv[l].astype(BF16),
                        w_up_sb[l].astype(BF16), w_out[l].astype(BF16), S)
        x2 = _mlp(x2, hf, mod3, norm_g[l, 3].reshape(1, D),
                  w_mlp_in[l].astype(BF16), w_mlp_out[l].astype(BF16), S)
    return x2.reshape(B, S, D)
```

```python
import functools
import math

import jax
import jax.numpy as jnp
from jax import lax
from jax.experimental import pallas as pl
from jax.experimental.pallas import tpu as pltpu

F32 = jnp.float32
BF16 = jnp.bfloat16

HEAD_DIM = 64
LANES = 128
DECAY_LORA = 64
ICLR_LORA = 64
GATE_LORA = 160
LORA_PAD = 512
NORM_EPS = 1e-6
GN_EPS = 64e-5
CHUNK = 128
SB_BLOCK = 128
SB_BLOCKS_PER_STEP = 3
SB_UNDERFLOW = 104.0
VMEM_LIMIT = 56 * 1024 * 1024


def _cparams(sem, vmem=VMEM_LIMIT):
    return pltpu.CompilerParams(dimension_semantics=sem, vmem_limit_bytes=vmem)


def _dot(a, b):
    return jnp.dot(a, b, preferred_element_type=F32)


def _dot_nt(a, b):
    return lax.dot_general(a, b, (((1,), (1,)), ((), ())), preferred_element_type=F32)


def _dot_tn(a, b):
    return lax.dot_general(a, b, (((0,), (0,)), ((), ())), preferred_element_type=F32)


def _split2(x):
    hi = x.astype(BF16)
    lo = (x - hi.astype(F32)).astype(BF16)
    return hi, lo


def _dot_exact01(x, m01):
    n = x.shape[0]
    res = _dot(jnp.concatenate(_split2(x), axis=0), m01)
    return res[:n] + res[n:]


def _exact01_dot(m01, x):
    n = x.shape[1]
    res = _dot(m01, jnp.concatenate(_split2(x), axis=1))
    return res[:, :n] + res[:, n:]


def _head_sum(x, bd):
    return _dot_exact01(x, bd)


def _rms(x, g):
    return x * lax.rsqrt(jnp.mean(x * x, axis=-1, keepdims=True) + NORM_EPS) * g


def _ada_kernel(c_ref, w_ref, b_ref, o_ref):
    c = c_ref[...]
    s = c * jax.nn.sigmoid(c)
    o_ref[...] = _dot(s.astype(BF16), w_ref[...].astype(BF16)) + b_ref[...]


def _ada(c, w_ada, b_ada, tn=1536):
    B, D = c.shape
    N = w_ada.shape[1]
    rows = 8
    c_pad = jnp.zeros((rows, D), F32).at[:B].set(c)
    out = pl.pallas_call(
        _ada_kernel,
        out_shape=jax.ShapeDtypeStruct((rows, N), F32),
        grid=(N // tn,),
        in_specs=[pl.BlockSpec((rows, D), lambda j: (0, 0)),
                  pl.BlockSpec((D, tn), lambda j: (0, j)),
                  pl.BlockSpec((1, tn), lambda j: (0, j))],
        out_specs=pl.BlockSpec((rows, tn), lambda j: (0, j)),
        compiler_params=_cparams(("parallel",)),
        name="ada",
    )(c_pad, w_ada, b_ada.reshape(1, N))
    return out[:B]


def _regroup_kernel(offs_ref, w_ref, o_ref, *, zero_lo, zero_hi, rows):
    del offs_ref
    out_row = pl.program_id(0) * rows + lax.broadcasted_iota(jnp.int32, (rows, 1), 0)
    pad = jnp.logical_and(out_row >= zero_lo, out_row < zero_hi)
    o_ref[...] = jnp.where(pad, 0.0, w_ref[...]).astype(BF16)


def _regroup_rows(wt, groups, zero_lo, zero_hi, rows=256):
    D = wt.shape[1]
    sub = 8
    assert all(s % sub == 0 and n % rows == 0 for s, n in groups)
    offs = jnp.asarray([(s + r) // sub for (s, n) in groups for r in range(0, n, rows)], jnp.int32)
    n_out = sum(n for _, n in groups)
    return pl.pallas_call(
        functools.partial(_regroup_kernel, zero_lo=zero_lo, zero_hi=zero_hi, rows=rows),
        out_shape=jax.ShapeDtypeStruct((n_out, D), BF16),
        grid_spec=pltpu.PrefetchScalarGridSpec(
            num_scalar_prefetch=1, grid=(n_out // rows,),
            in_specs=[pl.BlockSpec((pl.Element(rows), pl.Element(D)),
                                   lambda t, offs: (offs[t] * sub, 0))],
            out_specs=pl.BlockSpec((rows, D), lambda t, offs: (t, 0))),
        compiler_params=_cparams(("parallel",)),
        name="regroup_w_in",
    )(offs, wt)


def _inproj_kernel(x0_ref, mod0_ref, xn_ref, modn_ref, g_ref, w_ref, o_ref, sb_ref, h_ref,
                   *, d, n_main, chunk):
    i = pl.program_id(0)
    j = pl.program_id(1)
    slot = i % 2
    tm = xn_ref.shape[0]

    def modulated(x, mod_ref):
        shift = mod_ref[:, 0:d]
        scale = mod_ref[:, d:2 * d]
        return (_rms(x, g_ref[...]) * (1.0 + scale) + shift).astype(BF16)

    @pl.when(jnp.logical_and(i == 0, j == 0))
    def _():
        h_ref[0] = modulated(x0_ref[...], mod0_ref)

    def prepare_next_rows():
        r0 = pl.multiple_of(jnp.minimum(j * chunk, tm - chunk), 16)
        h_ref[1 - slot, pl.ds(r0, chunk), :] = modulated(xn_ref[pl.ds(r0, chunk), :], modn_ref)

    @pl.when(j < n_main)
    def _():
        o_ref[...] = _dot_nt(h_ref[slot], w_ref[...])
        prepare_next_rows()

    @pl.when(j >= n_main)
    def _():
        sb_ref[...] = _dot_nt(h_ref[slot], w_ref[...]).astype(BF16)
        prepare_next_rows()


def _inproj(x2, mod3, g0, w_t, n_main_cols, seq, tm=1024, tn=768):
    T, D = x2.shape
    N = w_t.shape[0]
    n_main = n_main_cols // tn
    n_i, n_j = T // tm, N // tn
    chunk = 16 * pl.cdiv(pl.cdiv(tm, n_j), 16)

    def nxt(i):
        return jnp.minimum(i + 1, n_i - 1)

    return pl.pallas_call(
        functools.partial(_inproj_kernel, d=D, n_main=n_main, chunk=chunk),
        out_shape=[jax.ShapeDtypeStruct((T, n_main_cols), F32),
                   jax.ShapeDtypeStruct((T, N - n_main_cols), BF16)],
        grid=(n_i, n_j),
        in_specs=[pl.BlockSpec((tm, D), lambda i, j: (0, 0), pipeline_mode=pl.Buffered(1)),
                  pl.BlockSpec((None, 1, mod3.shape[2]), lambda i, j: (0, 0, 0)),
                  pl.BlockSpec((tm, D), lambda i, j: (nxt(i), 0)),
                  pl.BlockSpec((None, 1, mod3.shape[2]), lambda i, j: ((nxt(i) * tm) // seq, 0, 0)),
                  pl.BlockSpec((1, D), lambda i, j: (0, 0)),
                  pl.BlockSpec((tn, D), lambda i, j: (j, 0))],
        out_specs=[pl.BlockSpec((tm, tn), lambda i, j: (i, jnp.minimum(j, n_main - 1))),
                   pl.BlockSpec((tm, tn), lambda i, j: (i, jnp.maximum(j - n_main, 0)))],
        scratch_shapes=[pltpu.VMEM((2, tm, D), BF16)],
        compiler_params=_cparams(("arbitrary", "arbitrary")),
        name="inproj",
    )(x2, mod3, x2, mod3, g0, w_t)


def _token_shift_mix(cur_ref, prev_ref, mu_ref, first):
    cur = cur_ref[...]
    row = lax.broadcasted_iota(jnp.int32, (cur.shape[0], 1), 0)
    prev_row = jnp.where(first, 0.0, prev_ref[7:8, :])
    shifted = jnp.where(row == 0, prev_row, pltpu.roll(cur, 1, 0))
    return cur + (shifted - cur) * mu_ref[...]


def _prep_kernel(k_ref, l_ref, kp_ref, lp_ref, muk_ref, mul_ref, w0_ref, a0_ref, kk_ref, ka_ref,
                 w2_ref, a2_ref, g2_ref, bd_ref,
                 ko_ref, kko_ref, bo_ref, lwo_ref, go_ref, *, tm, seq):
    first = (pl.program_id(0) * tm) % seq == 0
    k = _token_shift_mix(k_ref, kp_ref, muk_ref, first)
    lo = _token_shift_mix(l_ref, lp_ref, mul_ref, first)

    dw = _dot(jnp.tanh(lo).astype(BF16), w2_ref[...])
    da = _dot(lo.astype(BF16), a2_ref[...])
    g = _dot(jax.nn.sigmoid(lo).astype(BF16), g2_ref[...])

    w_log = -jax.nn.softplus(-(w0_ref[...] + dw)) - 0.5
    lw = -jnp.exp(w_log)
    a = jax.nn.sigmoid(a0_ref[...] + da)
    kx = k * kk_ref[...]
    bd = bd_ref[...]
    n_blk = kx.shape[1] // LANES
    sq = kx * kx
    ssum = jnp.concatenate(
        [_head_sum(sq[:, c * LANES:(c + 1) * LANES], bd) for c in range(n_blk)], axis=1)
    kk = kx * lax.rsqrt(jnp.maximum(ssum, 1e-24))
    k2 = k * (1.0 + (a - 1.0) * ka_ref[...])

    ko_ref[...] = k2
    kko_ref[...] = kk
    bo_ref[...] = kk * a
    lwo_ref[...] = lw
    go_ref[...] = g


def _prep(P, seq, consts, tm=512):
    T = P.shape[0]
    C = consts["w0"].shape[1]
    col_k, col_l = consts["col_k"], consts["col_l"]

    def cur(col, w):
        return pl.BlockSpec((tm, w), lambda i: (i, col))

    def prev(col, w):
        return pl.BlockSpec((8, w), lambda i: (jnp.maximum(i * (tm // 8) - 1, 0), col))

    def const(shape):
        return pl.BlockSpec(shape, lambda i: (0, 0))

    out_sd = jax.ShapeDtypeStruct((T, C), F32)
    outs = pl.pallas_call(
        functools.partial(_prep_kernel, tm=tm, seq=seq),
        out_shape=[out_sd] * 5,
        grid=(T // tm,),
        in_specs=[cur(col_k, C), cur(col_l, LORA_PAD), prev(col_k, C), prev(col_l, LORA_PAD),
                  const((1, C)), const((1, LORA_PAD)),
                  const((1, C)), const((1, C)), const((1, C)), const((1, C)),
                  const((LORA_PAD, C)), const((LORA_PAD, C)), const((LORA_PAD, C)),
                  const((LANES, LANES))],
        out_specs=[pl.BlockSpec((tm, C), lambda i: (i, 0))] * 5,
        compiler_params=_cparams(("parallel",)),
        name="rwkv_prep",
    )(P, P, P, P, consts["mu_k"], consts["mu_l"],
      consts["w0"], consts["a0"], consts["k_k"], consts["k_a"],
      consts["w2p"], consts["a2p"], consts["g2p"], consts["bd"])
    return outs


def _scan_kernel(r_ref, v_ref, rp_ref, vp_ref, mur_ref, muv_ref, k_ref, kk_ref, b_ref, lw_ref,
                 g_ref, rk_ref, lnw_ref, lnb_ref, o_ref, h_ref, *, nc, npair):
    L = CHUNK
    half = L // 2

    @pl.when(pl.program_id(2) == 0)
    def _():
        h_ref[...] = jnp.zeros_like(h_ref)

    ri = lax.broadcasted_iota(jnp.int32, (L, L), 0)
    ci = lax.broadcasted_iota(jnp.int32, (L, L), 1)
    lower_incl = ri >= ci
    lower_strict = ri > ci
    eye = ri == ci
    same_head = (ri < HEAD_DIM) == (ci < HEAD_DIM)
    tri01 = lower_incl.astype(BF16)
    eye_f = eye.astype(F32)
    lane = lax.broadcasted_iota(jnp.int32, (1, LANES), 1)
    head0 = lane < HEAD_DIM
    lane2 = lax.broadcasted_iota(jnp.int32, (1, 2 * LANES), 1)
    head0_2 = (lane2 % LANES) < HEAD_DIM
    tri_stack = jnp.concatenate([lower_strict, lower_incl], axis=0)
    chunks = range(nc * npair)
    pairs = [(c, h) for c in chunks for h in range(2)]

    def lanes_of(u):
        return slice((u // nc) * LANES, (u // nc + 1) * LANES)

    def ld(ref, u):
        return ref[pl.ds((u % nc) * L, L), lanes_of(u)]

    first = pl.program_id(2) == 0
    r_tile = _token_shift_mix(r_ref, rp_ref, mur_ref, first)
    v_tile = _token_shift_mix(v_ref, vp_ref, muv_ref, first)
    r = [r_tile[(u % nc) * L:(u % nc + 1) * L, lanes_of(u)] for u in chunks]
    v = [v_tile[(u % nc) * L:(u % nc + 1) * L, lanes_of(u)] for u in chunks]
    k = [ld(k_ref, c) for c in chunks]
    kk = [ld(kk_ref, c) for c in chunks]
    b = [ld(b_ref, c) for c in chunks]
    lw = [ld(lw_ref, c) for c in chunks]

    cum = [_exact01_dot(tri01, lw[c]) for c in chunks]
    lhs_mid, rhs_mid, kd_bf, r_dec, k_end, b_end, p_end, v_bf = [], [], [], [], [], [], [], []
    for c in chunks:
        mid = cum[c][half - 1:half, :]
        end = cum[c][L - 1:L, :]
        r_mid = r[c] * jnp.exp(cum[c] - mid)
        kk_mid = kk[c] * jnp.exp(cum[c] - lw[c] - mid)
        e_back = jnp.exp(mid - cum[c])
        lhs_mid.append(jnp.concatenate([kk_mid, r_mid], axis=0))
        rhs_mid.append(jnp.concatenate([b[c] * e_back, k[c] * e_back], axis=0).astype(BF16))
        kd_bf.append((kk[c] * jnp.exp(cum[c] - lw[c])).astype(BF16))
        r_dec.append(r[c] * jnp.exp(cum[c]))
        e_end = jnp.exp(end - cum[c])
        k_end.append((k[c] * e_end).astype(BF16))
        b_end.append((b[c] * e_end).astype(BF16))
        p_end.append(jnp.exp(end))
        v_bf.append(v[c].astype(BF16))

    aa = {}
    for (c, h) in pairs:
        hm = head0 if h == 0 else jnp.logical_not(head0)
        aa[c, h] = _dot_nt(jnp.where(hm, lhs_mid[c], 0.0).astype(BF16), rhs_mid[c])
    def side(f):
        return jnp.concatenate([f(0), f(1)], axis=1)

    tri2 = jnp.concatenate([tri_stack, tri_stack], axis=1)
    strict2 = jnp.concatenate([lower_strict, lower_strict], axis=1)
    incl2 = jnp.concatenate([lower_incl, lower_incl], axis=1)
    def by_head_rows(x):
        hm = head0 if x.shape[1] == LANES else head0_2
        zero = jnp.zeros_like(x)
        return jnp.concatenate([jnp.where(hm, x, zero), jnp.where(hm, zero, x)], axis=0)

    a_kb = [jnp.where(strict2, side(lambda h: aa[c, h][:L, :L]), 0.0) for c in chunks]
    a_rb = [jnp.where(incl2, side(lambda h: aa[c, h][L:, :L]), 0.0).astype(BF16) for c in chunks]
    a_xk = [jnp.where(tri2, side(lambda h: aa[c, h][:, L:]), 0.0).astype(BF16) for c in chunks]
    avr = [_dot(a_xk[c], by_head_rows(v_bf[c])) for c in chunks]

    hl = L // 2
    blk_r = lax.broadcasted_iota(jnp.int32, (2 * L, 2 * L), 0) // hl
    blk_c = lax.broadcasted_iota(jnp.int32, (2 * L, 2 * L), 1) // hl
    diag4 = blk_r == blk_c
    below4 = jnp.logical_and(blk_r == blk_c + 1, blk_c % 2 == 0)
    eye4 = (lax.broadcasted_iota(jnp.int32, (hl, 2 * L), 0)
            == lax.broadcasted_iota(jnp.int32, (hl, 2 * L), 1) % hl).astype(F32)

    def tile4(x, keep):
        return jnp.where(keep, jnp.concatenate([x, x, x, x], axis=0), jnp.zeros((), x.dtype))

    n_lvl = int(math.log2(hl)) - 1
    p, t, a21 = [], [], []
    for c in chunks:
        pd = jnp.where(head0_2, a_kb[c][:hl], a_kb[c][hl:])
        a21.append(jnp.where(head0_2, a_kb[c][hl:], 0.0).astype(BF16))
        pd_bf = pd.astype(BF16)
        p.append(_dot(pd_bf, tile4(pd_bf, diag4)))
        t.append(eye4 - pd)
    for lvl in range(n_lvl):
        for c in chunks:
            p_bd = tile4(p[c].astype(BF16), diag4)
            if lvl + 1 < n_lvl:
                tp = _dot(jnp.concatenate([t[c], p[c]], axis=0).astype(BF16), p_bd)
                t[c] = t[c] + tp[:hl]
                p[c] = tp[hl:]
            else:
                t[c] = t[c] + _dot(t[c].astype(BF16), p_bd)
    for c in chunks:
        td_bf = t[c].astype(BF16)
        y = _dot(a21[c], tile4(jnp.where(head0_2, td_bf, jnp.zeros((), BF16)), diag4))
        z = _dot(td_bf, tile4(y.astype(BF16), below4))
        t[c] = jnp.concatenate([jnp.where(head0_2, t[c], 0.0), jnp.where(head0_2, -z, t[c])], axis=0)

    rq, y0, gmat, cmat = [], [], [], []
    for c in chunks:
        av = avr[c][:L]
        arkv = avr[c][L:]
        x = jnp.concatenate([kd_bf[c], av.astype(BF16)], axis=1)
        w_bf = _dot(t[c].astype(BF16), by_head_rows(x)).astype(BF16)
        aw = _dot(a_rb[c], by_head_rows(w_bf))
        bw = _dot_tn(b_end[c], w_bf)
        kv = _dot_tn(k_end[c], v_bf[c])
        rq.append(r_dec[c] - aw[:, :LANES])
        y0.append(arkv - aw[:, LANES:])
        gmat.append(eye_f * p_end[c] - jnp.where(same_head, bw[:, :LANES], 0.0))
        cmat.append(jnp.where(same_head, kv - bw[:, LANES:], 0.0))

    bd2 = ((lax.broadcasted_iota(jnp.int32, (2 * LANES, 2 * LANES), 0) // HEAD_DIM)
           == (lax.broadcasted_iota(jnp.int32, (2 * LANES, 2 * LANES), 1) // HEAD_DIM)).astype(BF16)

    def head_sum_all(x):
        n = x.shape[0] // 2
        s2 = _dot_exact01(jnp.concatenate([x[:n], x[n:]], axis=1), bd2)
        return jnp.concatenate([s2[:, :LANES], s2[:, LANES:]], axis=0)

    rkr = head_sum_all(jnp.concatenate(
        [r[u] * k[u] * rk_ref[:, lanes_of(u)] for u in chunks], axis=0))
    bonus = [rkr[u * L:(u + 1) * L] * v[u] for u in chunks]

    inv_n = 1.0 / HEAD_DIM
    hstate = [h_ref[q] for q in range(npair)]
    ys = [None] * len(chunks)
    for c in range(nc):
        for q in range(npair):
            u = q * nc + c
            res = _dot(jnp.concatenate([rq[u], gmat[u]], axis=0).astype(BF16),
                       hstate[q].astype(BF16))
            ys[u] = res[:L] + y0[u]
            hstate[q] = res[L:] + cmat[u]
    for q in range(npair):
        h_ref[q] = hstate[q]
    y = jnp.concatenate(ys, axis=0)
    mean = head_sum_all(y) * inv_n
    yc = y - mean
    var = head_sum_all(yc * yc) * inv_n
    yn = yc * lax.rsqrt(var + GN_EPS)
    for u in chunks:
        rows = pl.ds((u % nc) * L, L)
        o_ref[rows, lanes_of(u)] = (
            (yn[u * L:(u + 1) * L] * lnw_ref[:, lanes_of(u)] + lnb_ref[:, lanes_of(u)] + bonus[u])
            * g_ref[rows, lanes_of(u)]).astype(BF16)


def _scan(P, col_r, col_v, mu_r, mu_v, arrs, rk, lnw, lnb, batch, seq, nc=4, npair=4):
    k, kk, b, lw, g = [a.reshape(batch, seq, a.shape[1]) for a in arrs]
    P3 = P.reshape(batch, seq, P.shape[1])
    C = k.shape[2]
    width = LANES * npair
    rows = CHUNK * nc
    blk = pl.BlockSpec((None, rows, width), lambda bi, p, c: (bi, c, p))
    par = pl.BlockSpec((1, width), lambda bi, p, c: (0, p))

    def raw(col):
        return pl.BlockSpec((None, rows, width), lambda bi, p, c: (bi, c, col // width + p))

    def raw_prev(col):
        return pl.BlockSpec((None, 8, width),
                            lambda bi, p, c: (bi, jnp.maximum(c * (rows // 8) - 1, 0),
                                              col // width + p))

    out = pl.pallas_call(
        functools.partial(_scan_kernel, nc=nc, npair=npair),
        out_shape=jax.ShapeDtypeStruct((batch, seq, C), BF16),
        grid=(batch, C // width, seq // rows),
        in_specs=[raw(col_r), raw(col_v), raw_prev(col_r), raw_prev(col_v), par, par]
                 + [blk] * 5 + [par, par, par],
        out_specs=blk,
        scratch_shapes=[pltpu.VMEM((npair, LANES, LANES), F32)],
        compiler_params=_cparams(("parallel", "parallel", "arbitrary")),
        name="rwkv_scan",
    )(P3, P3, P3, P3, mu_r, mu_v, k, kk, b, lw, g, rk, lnw, lnb)
    return out.reshape(batch * seq, C)


def _sb_kernel(q_ref, k_ref, v_ref, o_ref, *, npair, nq):
    tq = tk = SB_BLOCK
    nu = SB_BLOCKS_PER_STEP
    i = pl.program_id(2)
    lane = lax.broadcasted_iota(jnp.int32, (1, LANES), 1)
    head0 = lane < HEAD_DIM
    scale = jnp.asarray(1.0 / math.sqrt(HEAD_DIM), BF16)
    streams = [(s, hp) for s in range(nq) for hp in range(npair)]
    ns = len(streams)
    q = [q_ref[s * tq:(s + 1) * tq, hp * LANES:(hp + 1) * LANES] * scale for (s, hp) in streams]
    r2 = lax.broadcasted_iota(jnp.int32, (2 * tk, 2 * tk), 0)
    c2 = lax.broadcasted_iota(jnp.int32, (2 * tk, 2 * tk), 1)
    after01 = jnp.logical_and(r2 > c2, (r2 < tk) == (c2 < tk)).astype(BF16)
    key_minus_query = (lax.broadcasted_iota(jnp.int32, (tq, 2 * tk), 1) % tk
                       - lax.broadcasted_iota(jnp.int32, (tq, 2 * tk), 0))
    units = [(p, u) for p in range(ns) for u in range(nu)]

    def by_head_rows(x):
        zero = jnp.zeros_like(x)
        return jnp.concatenate([jnp.where(head0, x, zero), jnp.where(head0, zero, x)], axis=0)

    def both(c0, c1):
        return jnp.concatenate([jnp.broadcast_to(c0, (tq, tk)), jnp.broadcast_to(c1, (tq, tk))],
                               axis=1)

    def body(carry):
        jt, _, offs, accs = carry
        z, sp, log_keep, mask, vb = {}, {}, {}, {}, {}
        for (p, u) in units:
            s, hp = streams[p]
            jb = jt - (nq - 1 - s) - u
            start = pl.multiple_of(jnp.maximum(jb, 0) * tk, tk)
            lanes = slice(hp * LANES, (hp + 1) * LANES)
            kb = by_head_rows(k_ref[pl.ds(start, tk), lanes])
            vb[p, u] = by_head_rows(v_ref[pl.ds(start, tk), lanes])
            zu = _dot_nt(q[p], kb)
            if u > 0:
                zu = zu + jnp.where(jb >= 0, 0.0, -1e30)
            spu = jnp.maximum(zu, 0.0) + jnp.log(1.0 + jnp.exp(-jnp.abs(zu)))
            z[p, u], sp[p, u] = zu, spu
            if u == 0:
                q_start = (i * nq + s) * tq
                mask[p, u] = key_minus_query < jnp.where(jb >= 0, q_start - jb * tk, -tq)
                log_keep[p, u] = jnp.where(mask[p, u], -spu, 0.0)
            else:
                log_keep[p, u] = -spu
        parts = []
        for key in units:
            parts.extend(_split2(log_keep[key]))
        loc = _dot(jnp.concatenate(parts, axis=0), after01)
        offs, accs = list(offs), list(accs)
        for n, (p, u) in enumerate(units):
            local = loc[2 * n * tq:(2 * n + 1) * tq] + loc[(2 * n + 1) * tq:(2 * n + 2) * tq]
            later = local + both(offs[2 * p], offs[2 * p + 1])
            att = jnp.exp(z[p, u] - sp[p, u] + later)
            if u == 0:
                att = jnp.where(mask[p, u], att, 0.0)
            accs[p] = accs[p] + _dot(att.astype(BF16), vb[p, u])
            offs[2 * p] = offs[2 * p] + (local[:, 0:1] + log_keep[p, u][:, 0:1])
            offs[2 * p + 1] = offs[2 * p + 1] + (local[:, tk:tk + 1] + log_keep[p, u][:, tk:tk + 1])
        top = offs[0]
        for o in offs[1:]:
            top = jnp.maximum(top, o)
        alive = (jnp.max(top) > -SB_UNDERFLOW).astype(jnp.int32)
        return jt - nu, alive, tuple(offs), tuple(accs)

    def cond(carry):
        jt, alive = carry[0], carry[1]
        return jnp.logical_and(jt >= 0, alive > 0)

    init = (i * nq + nq - 1, jnp.int32(1),
            tuple(jnp.zeros((tq, 1), F32) for _ in range(2 * ns)),
            tuple(jnp.zeros((tq, LANES), F32) for _ in range(ns)))
    accs = lax.while_loop(cond, body, body(init))[3]
    for p, (s, hp) in enumerate(streams):
        o_ref[s * tq:(s + 1) * tq, hp * LANES:(hp + 1) * LANES] = accs[p].astype(BF16)


def _stick_breaking(qkv, width, npair=2, nq=2):
    B, S, _ = qkv.shape
    w = LANES * npair
    nblk = width // w
    rows = SB_BLOCK * nq
    out = pl.pallas_call(
        functools.partial(_sb_kernel, npair=npair, nq=nq),
        out_shape=jax.ShapeDtypeStruct((B, S, width), BF16),
        grid=(B, nblk, S // rows),
        in_specs=[pl.BlockSpec((None, rows, w), lambda b, p, i: (b, i, p)),
                  pl.BlockSpec((None, S, w), lambda b, p, i: (b, 0, nblk + p)),
                  pl.BlockSpec((None, S, w), lambda b, p, i: (b, 0, 2 * nblk + p))],
        out_specs=pl.BlockSpec((None, rows, w), lambda b, p, i: (b, i, p)),
        compiler_params=_cparams(("parallel", "parallel", "arbitrary")),
        name="stick_breaking",
    )(qkv, qkv, qkv)
    return out.reshape(B * S, width)


def _merge_kernel(ya_ref, yb_ref, ga_ref, gb_ref, x_ref, mod_ref, g1_ref,
                  g2_ref, wua_ref, wub_ref, wo_ref, o_ref, hf_ref, *, d):
    ua = _dot(ya_ref[...], wua_ref[...])
    ub = _dot(yb_ref[...], wub_ref[...])
    merged = jax.nn.sigmoid(ga_ref[...]) * ua + jax.nn.sigmoid(gb_ref[...]) * ub
    mix = _dot(merged.astype(BF16), wo_ref[...])
    gate_m = mod_ref[:, 2 * d:3 * d]
    x1 = x_ref[...] + gate_m * _rms(mix, g1_ref[...])
    o_ref[...] = x1
    shift_f = mod_ref[:, 3 * d:4 * d]
    scale_f = mod_ref[:, 4 * d:5 * d]
    hf_ref[...] = (_rms(x1, g2_ref[...]) * (1.0 + scale_f) + shift_f).astype(BF16)


def _merge(ya, yb, P, x2, mod3, g1, g2, wua, wub, wo, seq, tm=256):
    T, D = x2.shape
    C = ya.shape[1]

    def const(shape):
        return pl.BlockSpec(shape, lambda i: (0, 0), pipeline_mode=pl.Buffered(1))

    return pl.pallas_call(
        functools.partial(_merge_kernel, d=D),
        out_shape=[jax.ShapeDtypeStruct((T, D), F32), jax.ShapeDtypeStruct((T, D), BF16)],
        grid=(T // tm,),
        in_specs=[pl.BlockSpec((tm, C), lambda i: (i, 0)),
                  pl.BlockSpec((tm, C), lambda i: (i, 0)),
                  pl.BlockSpec((tm, D), lambda i: (i, 0)),
                  pl.BlockSpec((tm, D), lambda i: (i, 1)),
                  pl.BlockSpec((tm, D), lambda i: (i, 0)),
                  pl.BlockSpec((None, 1, mod3.shape[2]), lambda i: ((i * tm) // seq, 0, 0)),
                  pl.BlockSpec((1, D), lambda i: (0, 0)),
                  pl.BlockSpec((1, D), lambda i: (0, 0)),
                  const((C, D)), const((C, D)), const((D, D))],
        out_specs=[pl.BlockSpec((tm, D), lambda i: (i, 0)), pl.BlockSpec((tm, D), lambda i: (i, 0))],
        compiler_params=_cparams(("parallel",)),
        name="merge_out",
    )(ya, yb, P, P, x2, mod3, g1, g2, wua, wub, wo)


def _mlp_kernel(xp_ref, hf_ref, modp_ref, g3_ref, w1_ref, w2_ref, o_ref, acc_ref, *, d, rc):
    i = pl.program_id(0)
    j = pl.program_id(1)
    n = pl.num_programs(0) - 1
    slot = i % 2

    @pl.when(jnp.logical_and(i == 0, j == 0))
    def _():
        acc_ref[1] = jnp.zeros(acc_ref.shape[1:], F32)

    def finish_prev_rows():
        rows = pl.ds(pl.multiple_of(j * rc, rc), rc)
        gate_f = modp_ref[:, 5 * d:6 * d]
        o_ref[rows, :] = xp_ref[rows, :] + gate_f * _rms(acc_ref[1 - slot, rows, :], g3_ref[...])

    def partial_out():
        u = jnp.maximum(_dot(hf_ref[...], w1_ref[...]), 0.0)
        return _dot((u * u).astype(BF16), w2_ref[...])

    @pl.when(jnp.logical_and(i < n, j == 0))
    def _():
        finish_prev_rows()
        acc_ref[slot] = partial_out()

    @pl.when(jnp.logical_and(i < n, j > 0))
    def _():
        finish_prev_rows()
        acc_ref[slot] += partial_out()

    @pl.when(i == n)
    def _():
        finish_prev_rows()


def _mlp(x2, hf, mod3, g3, w1, w2, seq, tm=512, tf=1024):
    T, D = x2.shape
    F = w1.shape[1]
    n, nj = T // tm, F // tf
    assert tm % nj == 0 and (tm // nj) % 8 == 0

    def cur(i):
        return jnp.minimum(i, n - 1)

    def prv(i):
        return jnp.maximum(i - 1, 0)

    def ff(i, j):
        return jnp.where(i < n, j, nj - 1)

    return pl.pallas_call(
        functools.partial(_mlp_kernel, d=D, rc=tm // nj),
        out_shape=jax.ShapeDtypeStruct((T, D), F32),
        grid=(n + 1, nj),
        in_specs=[pl.BlockSpec((tm, D), lambda i, j: (prv(i), 0)),
                  pl.BlockSpec((tm, D), lambda i, j: (cur(i), 0)),
                  pl.BlockSpec((None, 1, mod3.shape[2]), lambda i, j: ((prv(i) * tm) // seq, 0, 0)),
                  pl.BlockSpec((1, D), lambda i, j: (0, 0)),
                  pl.BlockSpec((D, tf), lambda i, j: (0, ff(i, j))),
                  pl.BlockSpec((tf, D), lambda i, j: (ff(i, j), 0))],
        out_specs=pl.BlockSpec((tm, D), lambda i, j: (prv(i), 0)),
        scratch_shapes=[pltpu.VMEM((2, tm, D), F32)],
        compiler_params=_cparams(("arbitrary", "arbitrary")),
        name="mlp",
    )(x2, hf, mod3, g3, w1, w2)


def _pad_rows(w, offset, total):
    return jnp.zeros((total, w.shape[1]), w.dtype).at[offset:offset + w.shape[0]].set(w)


def kernel(x, c, w_ada, b_ada, norm_g, w_in, mu_shift, w0, w2, a0, a2, g2, k_k, k_a,
           r_k, ln_x_w, ln_x_b, w_up_rwkv, w_up_sb, w_out, w_mlp_in, w_mlp_out):
    B, S, D = x.shape
    depth = w_in.shape[0]
    C = w0.shape[1]
    W = w_up_sb.shape[1]
    rwkv_cols = 3 * C + DECAY_LORA + ICLR_LORA + GATE_LORA
    sb_cols = 3 * W
    n_lora = DECAY_LORA + ICLR_LORA + GATE_LORA
    assert C == 1024 and W == 1024 and D == 2 * C and S % (2 * CHUNK) == 0

    hi = lax.broadcasted_iota(jnp.int32, (LANES, LANES), 0) // HEAD_DIM
    hj = lax.broadcasted_iota(jnp.int32, (LANES, LANES), 1) // HEAD_DIM
    bd = (hi == hj).astype(BF16)

    x2 = x.reshape(B * S, D)
    for l in range(depth):
        mod = _ada(c, w_ada[l], b_ada[l])
        mod3 = mod.reshape(B, 1, mod.shape[1])

        g_off = rwkv_cols + sb_cols
        n_main_cols = 2 * D + 3 * C + LORA_PAD
        w_perm = _regroup_rows(
            w_in[l].T,
            [(g_off, 2 * D), (0, 3 * C), (3 * C, LORA_PAD), (rwkv_cols, sb_cols)],
            zero_lo=2 * D + 3 * C + n_lora, zero_hi=n_main_cols)
        P, qkv = _inproj(x2, mod3, norm_g[l, 0].reshape(1, D), w_perm, n_main_cols, S)

        mu = mu_shift[l]
        consts = dict(
            col_k=2 * D // C + 1, col_l=(2 * D + 3 * C) // LORA_PAD,
            mu_k=mu[None, C:2 * C],
            mu_l=jnp.pad(mu[None, 3 * C:], ((0, 0), (0, LORA_PAD - n_lora))),
            w0=w0[l][None], a0=a0[l][None], k_k=k_k[l][None], k_a=k_a[l][None],
            w2p=_pad_rows(w2[l], 0, LORA_PAD).astype(BF16),
            a2p=_pad_rows(a2[l], DECAY_LORA, LORA_PAD).astype(BF16),
            g2p=_pad_rows(g2[l], DECAY_LORA + ICLR_LORA, LORA_PAD).astype(BF16),
            bd=bd)
        prep = _prep(P, S, consts)
        ya = _scan(P, 2 * D, 2 * D + 2 * C, mu[None, :C], mu[None, 2 * C:3 * C], prep,
                   r_k[l].reshape(1, C), ln_x_w[l][None], ln_x_b[l][None], B, S)

        yb = _stick_breaking(qkv.reshape(B, S, 3 * W), W)

        x2, hf = _merge(ya, yb, P, x2, mod3, norm_g[l, 1].reshape(1, D),
                        norm_g[l, 2].reshape(1, D), w_up_rwkv[l].astype(BF16),
                        w_up_sb[l].astype(BF16), w_out[l].astype(BF16), S)
        x2 = _mlp(x2, hf, mod3, norm_g[l, 3].reshape(1, D),
                  w_mlp_in[l].astype(BF16), w_mlp_out[l].astype(BF16), S)
    return x2.reshape(B, S, D)
```

```python
import functools
import math

import jax
import jax.numpy as jnp
from jax import lax
from jax.experimental import pallas as pl
from jax.experimental.pallas import tpu as pltpu

F32 = jnp.float32
BF16 = jnp.bfloat16

HEAD_DIM = 64
LANES = 128
DECAY_LORA = 64
ICLR_LORA = 64
GATE_LORA = 160
LORA_PAD = 512
NORM_EPS = 1e-6
GN_EPS = 64e-5
CHUNK = 128
SB_BLOCK = 128
SB_BLOCKS_PER_STEP = 3
SB_UNDERFLOW = 104.0
VMEM_LIMIT = 56 * 1024 * 1024


def _cparams(sem, vmem=VMEM_LIMIT):
    return pltpu.CompilerParams(dimension_semantics=sem, vmem_limit_bytes=vmem)


def _dot(a, b):
    return jnp.dot(a, b, preferred_element_type=F32)


def _dot_nt(a, b):
    return lax.dot_general(a, b, (((1,), (1,)), ((), ())), preferred_element_type=F32)


def _dot_tn(a, b):
    return lax.dot_general(a, b, (((0,), (0,)), ((), ())), preferred_element_type=F32)


def _split2(x):
    hi = x.astype(BF16)
    lo = (x - hi.astype(F32)).astype(BF16)
    return hi, lo


def _dot_exact01(x, m01):
    n = x.shape[0]
    res = _dot(jnp.concatenate(_split2(x), axis=0), m01)
    return res[:n] + res[n:]


def _exact01_dot(m01, x):
    n = x.shape[1]
    res = _dot(m01, jnp.concatenate(_split2(x), axis=1))
    return res[:, :n] + res[:, n:]


def _head_sum(x, bd):
    return _dot_exact01(x, bd)


def _rms(x, g):
    return x * lax.rsqrt(jnp.mean(x * x, axis=-1, keepdims=True) + NORM_EPS) * g


def _ada_kernel(c_ref, w_ref, b_ref, o_ref):
    c = c_ref[...]
    s = c * jax.nn.sigmoid(c)
    o_ref[...] = _dot(s.astype(BF16), w_ref[...].astype(BF16)) + b_ref[...]


def _ada(c, w_ada, b_ada, tn=1536):
    B, D = c.shape
    N = w_ada.shape[1]
    rows = 8
    c_pad = jnp.zeros((rows, D), F32).at[:B].set(c)
    out = pl.pallas_call(
        _ada_kernel,
        out_shape=jax.ShapeDtypeStruct((rows, N), F32),
        grid=(N // tn,),
        in_specs=[pl.BlockSpec((rows, D), lambda j: (0, 0)),
                  pl.BlockSpec((D, tn), lambda j: (0, j)),
                  pl.BlockSpec((1, tn), lambda j: (0, j))],
        out_specs=pl.BlockSpec((rows, tn), lambda j: (0, j)),
        compiler_params=_cparams(("parallel",)),
        name="ada",
    )(c_pad, w_ada, b_ada.reshape(1, N))
    return out[:B]


def _regroup_kernel(offs_ref, w_ref, o_ref, *, zero_lo, zero_hi, rows):
    del offs_ref
    out_row = pl.program_id(0) * rows + lax.broadcasted_iota(jnp.int32, (rows, 1), 0)
    pad = jnp.logical_and(out_row >= zero_lo, out_row < zero_hi)
    o_ref[...] = jnp.where(pad, 0.0, w_ref[...]).astype(BF16)


def _regroup_rows(wt, groups, zero_lo, zero_hi, rows=256):
    D = wt.shape[1]
    sub = 8
    assert all(s % sub == 0 and n % rows == 0 for s, n in groups)
    offs = jnp.asarray([(s + r) // sub for (s, n) in groups for r in range(0, n, rows)], jnp.int32)
    n_out = sum(n for _, n in groups)
    return pl.pallas_call(
        functools.partial(_regroup_kernel, zero_lo=zero_lo, zero_hi=zero_hi, rows=rows),
        out_shape=jax.ShapeDtypeStruct((n_out, D), BF16),
        grid_spec=pltpu.PrefetchScalarGridSpec(
            num_scalar_prefetch=1, grid=(n_out // rows,),
            in_specs=[pl.BlockSpec((pl.Element(rows), pl.Element(D)),
                                   lambda t, offs: (offs[t] * sub, 0))],
            out_specs=pl.BlockSpec((rows, D), lambda t, offs: (t, 0))),
        compiler_params=_cparams(("parallel",)),
        name="regroup_w_in",
    )(offs, wt)


def _inproj_kernel(x0_ref, mod0_ref, xn_ref, modn_ref, g_ref, w_ref, o_ref, sb_ref, h_ref,
                   *, d, n_main, chunk):
    i = pl.program_id(0)
    j = pl.program_id(1)
    slot = i % 2
    tm = xn_ref.shape[0]

    def modulated(x, mod_ref):
        shift = mod_ref[:, 0:d]
        scale = mod_ref[:, d:2 * d]
        return (_rms(x, g_ref[...]) * (1.0 + scale) + shift).astype(BF16)

    @pl.when(jnp.logical_and(i == 0, j == 0))
    def _():
        h_ref[0] = modulated(x0_ref[...], mod0_ref)

    def prepare_next_rows():
        r0 = pl.multiple_of(jnp.minimum(j * chunk, tm - chunk), 16)
        h_ref[1 - slot, pl.ds(r0, chunk), :] = modulated(xn_ref[pl.ds(r0, chunk), :], modn_ref)

    @pl.when(j < n_main)
    def _():
        o_ref[...] = _dot_nt(h_ref[slot], w_ref[...])
        prepare_next_rows()

    @pl.when(j >= n_main)
    def _():
        sb_ref[...] = _dot_nt(h_ref[slot], w_ref[...]).astype(BF16)
        prepare_next_rows()


def _inproj(x2, mod3, g0, w_t, n_main_cols, seq, tm=1024, tn=768):
    T, D = x2.shape
    N = w_t.shape[0]
    n_main = n_main_cols // tn
    n_i, n_j = T // tm, N // tn
    chunk = 16 * pl.cdiv(pl.cdiv(tm, n_j), 16)

    def nxt(i):
        return jnp.minimum(i + 1, n_i - 1)

    return pl.pallas_call(
        functools.partial(_inproj_kernel, d=D, n_main=n_main, chunk=chunk),
        out_shape=[jax.ShapeDtypeStruct((T, n_main_cols), F32),
                   jax.ShapeDtypeStruct((T, N - n_main_cols), BF16)],
        grid=(n_i, n_j),
        in_specs=[pl.BlockSpec((tm, D), lambda i, j: (0, 0), pipeline_mode=pl.Buffered(1)),
                  pl.BlockSpec((None, 1, mod3.shape[2]), lambda i, j: (0, 0, 0)),
                  pl.BlockSpec((tm, D), lambda i, j: (nxt(i), 0)),
                  pl.BlockSpec((None, 1, mod3.shape[2]), lambda i, j: ((nxt(i) * tm) // seq, 0, 0)),
                  pl.BlockSpec((1, D), lambda i, j: (0, 0)),
                  pl.BlockSpec((tn, D), lambda i, j: (j, 0))],
        out_specs=[pl.BlockSpec((tm, tn), lambda i, j: (i, jnp.minimum(j, n_main - 1))),
                   pl.BlockSpec((tm, tn), lambda i, j: (i, jnp.maximum(j - n_main, 0)))],
        scratch_shapes=[pltpu.VMEM((2, tm, D), BF16)],
        compiler_params=_cparams(("arbitrary", "arbitrary")),
        name="inproj",
    )(x2, mod3, x2, mod3, g0, w_t)


def _token_shift_mix(cur_ref, prev_ref, mu_ref, first):
    cur = cur_ref[...]
    row = lax.broadcasted_iota(jnp.int32, (cur.shape[0], 1), 0)
    prev_row = jnp.where(first, 0.0, prev_ref[7:8, :])
    shifted = jnp.where(row == 0, prev_row, pltpu.roll(cur, 1, 0))
    return cur + (shifted - cur) * mu_ref[...]


def _prep_kernel(k_ref, l_ref, kp_ref, lp_ref, muk_ref, mul_ref, w0_ref, a0_ref, kk_ref, ka_ref,
                 w2_ref, a2_ref, g2_ref, bd_ref,
                 ko_ref, kko_ref, bo_ref, lwo_ref, go_ref, *, tm, seq):
    first = (pl.program_id(0) * tm) % seq == 0
    k = _token_shift_mix(k_ref, kp_ref, muk_ref, first)
    lo = _token_shift_mix(l_ref, lp_ref, mul_ref, first)

    dw = _dot(jnp.tanh(lo).astype(BF16), w2_ref[...])
    da = _dot(lo.astype(BF16), a2_ref[...])
    g = _dot(jax.nn.sigmoid(lo).astype(BF16), g2_ref[...])

    t = -(w0_ref[...] + dw)
    softplus_t = jnp.maximum(t, 0.0) + jnp.log(1.0 + jnp.exp(-jnp.abs(t)))
    w_log = -softplus_t - 0.5
    lw = -jnp.exp(w_log)
    a = jax.nn.sigmoid(a0_ref[...] + da)
    kx = k * kk_ref[...]
    bd = bd_ref[...]
    n_blk = kx.shape[1] // LANES
    sq = kx * kx
    ssum = jnp.concatenate(
        [_head_sum(sq[:, c * LANES:(c + 1) * LANES], bd) for c in range(n_blk)], axis=1)
    kk = kx * lax.rsqrt(jnp.maximum(ssum, 1e-24))
    k2 = k * (1.0 + (a - 1.0) * ka_ref[...])

    ko_ref[...] = k2
    kko_ref[...] = kk
    bo_ref[...] = kk * a
    lwo_ref[...] = lw
    go_ref[...] = g


def _prep(P, seq, consts, tm=512):
    T = P.shape[0]
    C = consts["w0"].shape[1]
    col_k, col_l = consts["col_k"], consts["col_l"]

    def cur(col, w):
        return pl.BlockSpec((tm, w), lambda i: (i, col))

    def prev(col, w):
        return pl.BlockSpec((8, w), lambda i: (jnp.maximum(i * (tm // 8) - 1, 0), col))

    def const(shape):
        return pl.BlockSpec(shape, lambda i: (0, 0))

    out_sd = jax.ShapeDtypeStruct((T, C), F32)
    outs = pl.pallas_call(
        functools.partial(_prep_kernel, tm=tm, seq=seq),
        out_shape=[out_sd] * 5,
        grid=(T // tm,),
        in_specs=[cur(col_k, C), cur(col_l, LORA_PAD), prev(col_k, C), prev(col_l, LORA_PAD),
                  const((1, C)), const((1, LORA_PAD)),
                  const((1, C)), const((1, C)), const((1, C)), const((1, C)),
                  const((LORA_PAD, C)), const((LORA_PAD, C)), const((LORA_PAD, C)),
                  const((LANES, LANES))],
        out_specs=[pl.BlockSpec((tm, C), lambda i: (i, 0))] * 5,
        compiler_params=_cparams(("parallel",)),
        name="rwkv_prep",
    )(P, P, P, P, consts["mu_k"], consts["mu_l"],
      consts["w0"], consts["a0"], consts["k_k"], consts["k_a"],
      consts["w2p"], consts["a2p"], consts["g2p"], consts["bd"])
    return outs


def _scan_kernel(r_ref, v_ref, rp_ref, vp_ref, mur_ref, muv_ref, k_ref, kk_ref, b_ref, lw_ref,
                 g_ref, rk_ref, lnw_ref, lnb_ref, o_ref, h_ref, *, nc, npair):
    L = CHUNK
    half = L // 2

    @pl.when(pl.program_id(2) == 0)
    def _():
        h_ref[...] = jnp.zeros_like(h_ref)

    ri = lax.broadcasted_iota(jnp.int32, (L, L), 0)
    ci = lax.broadcasted_iota(jnp.int32, (L, L), 1)
    lower_incl = ri >= ci
    lower_strict = ri > ci
    eye = ri == ci
    same_head = (ri < HEAD_DIM) == (ci < HEAD_DIM)
    tri01 = lower_incl.astype(BF16)
    eye_f = eye.astype(F32)
    lane = lax.broadcasted_iota(jnp.int32, (1, LANES), 1)
    head0 = lane < HEAD_DIM
    lane2 = lax.broadcasted_iota(jnp.int32, (1, 2 * LANES), 1)
    head0_2 = (lane2 % LANES) < HEAD_DIM
    tri_stack = jnp.concatenate([lower_strict, lower_incl], axis=0)
    chunks = range(nc * npair)
    pairs = [(c, h) for c in chunks for h in range(2)]

    def lanes_of(u):
        return slice((u // nc) * LANES, (u // nc + 1) * LANES)

    def ld(ref, u):
        return ref[pl.ds((u % nc) * L, L), lanes_of(u)]

    first = pl.program_id(2) == 0
    r_tile = _token_shift_mix(r_ref, rp_ref, mur_ref, first)
    v_tile = _token_shift_mix(v_ref, vp_ref, muv_ref, first)
    r = [r_tile[(u % nc) * L:(u % nc + 1) * L, lanes_of(u)] for u in chunks]
    v = [v_tile[(u % nc) * L:(u % nc + 1) * L, lanes_of(u)] for u in chunks]
    k = [ld(k_ref, c) for c in chunks]
    kk = [ld(kk_ref, c) for c in chunks]
    b = [ld(b_ref, c) for c in chunks]
    lw = [ld(lw_ref, c) for c in chunks]

    cum = [_exact01_dot(tri01, lw[c]) for c in chunks]
    lhs_mid, rhs_mid, kd_bf, r_dec, k_end, b_end, p_end, v_bf = [], [], [], [], [], [], [], []
    for c in chunks:
        mid = cum[c][half - 1:half, :]
        end = cum[c][L - 1:L, :]
        r_mid = r[c] * jnp.exp(cum[c] - mid)
        kk_mid = kk[c] * jnp.exp(cum[c] - lw[c] - mid)
        e_back = jnp.exp(mid - cum[c])
        lhs_mid.append(jnp.concatenate([kk_mid, r_mid], axis=0))
        rhs_mid.append(jnp.concatenate([b[c] * e_back, k[c] * e_back], axis=0).astype(BF16))
        kd_bf.append((kk[c] * jnp.exp(cum[c] - lw[c])).astype(BF16))
        r_dec.append(r[c] * jnp.exp(cum[c]))
        e_end = jnp.exp(end - cum[c])
        k_end.append((k[c] * e_end).astype(BF16))
        b_end.append((b[c] * e_end).astype(BF16))
        p_end.append(jnp.exp(end))
        v_bf.append(v[c].astype(BF16))

    aa = {}
    for (c, h) in pairs:
        hm = head0 if h == 0 else jnp.logical_not(head0)
        aa[c, h] = _dot_nt(jnp.where(hm, lhs_mid[c], 0.0).astype(BF16), rhs_mid[c])
    def side(f):
        return jnp.concatenate([f(0), f(1)], axis=1)

    tri2 = jnp.concatenate([tri_stack, tri_stack], axis=1)
    strict2 = jnp.concatenate([lower_strict, lower_strict], axis=1)
    incl2 = jnp.concatenate([lower_incl, lower_incl], axis=1)
    def by_head_rows(x):
        hm = head0 if x.shape[1] == LANES else head0_2
        zero = jnp.zeros_like(x)
        return jnp.concatenate([jnp.where(hm, x, zero), jnp.where(hm, zero, x)], axis=0)

    a_kb = [jnp.where(strict2, side(lambda h: aa[c, h][:L, :L]), 0.0) for c in chunks]
    a_rb = [jnp.where(incl2, side(lambda h: aa[c, h][L:, :L]), 0.0).astype(BF16) for c in chunks]
    a_xk = [jnp.where(tri2, side(lambda h: aa[c, h][:, L:]), 0.0).astype(BF16) for c in chunks]
    avr = [_dot(a_xk[c], by_head_rows(v_bf[c])) for c in chunks]

    hl = L // 2
    blk_r = lax.broadcasted_iota(jnp.int32, (2 * L, 2 * L), 0) // hl
    blk_c = lax.broadcasted_iota(jnp.int32, (2 * L, 2 * L), 1) // hl
    diag4 = blk_r == blk_c
    below4 = jnp.logical_and(blk_r == blk_c + 1, blk_c % 2 == 0)
    eye4 = (lax.broadcasted_iota(jnp.int32, (hl, 2 * L), 0)
            == lax.broadcasted_iota(jnp.int32, (hl, 2 * L), 1) % hl).astype(F32)

    def tile4(x, keep):
        return jnp.where(keep, jnp.concatenate([x, x, x, x], axis=0), jnp.zeros((), x.dtype))

    n_lvl = int(math.log2(hl)) - 1
    p, t, a21 = [], [], []
    for c in chunks:
        pd = jnp.where(head0_2, a_kb[c][:hl], a_kb[c][hl:])
        a21.append(jnp.where(head0_2, a_kb[c][hl:], 0.0).astype(BF16))
        pd_bf = pd.astype(BF16)
        p.append(_dot(pd_bf, tile4(pd_bf, diag4)))
        t.append(eye4 - pd)
    for lvl in range(n_lvl):
        for c in chunks:
            p_bd = tile4(p[c].astype(BF16), diag4)
            if lvl + 1 < n_lvl:
                tp = _dot(jnp.concatenate([t[c], p[c]], axis=0).astype(BF16), p_bd)
                t[c] = t[c] + tp[:hl]
                p[c] = tp[hl:]
            else:
                t[c] = t[c] + _dot(t[c].astype(BF16), p_bd)
    for c in chunks:
        td_bf = t[c].astype(BF16)
        y = _dot(a21[c], tile4(jnp.where(head0_2, td_bf, jnp.zeros((), BF16)), diag4))
        z = _dot(td_bf, tile4(y.astype(BF16), below4))
        t[c] = jnp.concatenate([jnp.where(head0_2, t[c], 0.0), jnp.where(head0_2, -z, t[c])], axis=0)

    rq, y0, gmat, cmat = [], [], [], []
    for c in chunks:
        av = avr[c][:L]
        arkv = avr[c][L:]
        x = jnp.concatenate([kd_bf[c], av.astype(BF16)], axis=1)
        w_bf = _dot(t[c].astype(BF16), by_head_rows(x)).astype(BF16)
        aw = _dot(a_rb[c], by_head_rows(w_bf))
        bw = _dot_tn(b_end[c], w_bf)
        kv = _dot_tn(k_end[c], v_bf[c])
        rq.append(r_dec[c] - aw[:, :LANES])
        y0.append(arkv - aw[:, LANES:])
        gmat.append(eye_f * p_end[c] - jnp.where(same_head, bw[:, :LANES], 0.0))
        cmat.append(jnp.where(same_head, kv - bw[:, LANES:], 0.0))

    bd2 = ((lax.broadcasted_iota(jnp.int32, (2 * LANES, 2 * LANES), 0) // HEAD_DIM)
           == (lax.broadcasted_iota(jnp.int32, (2 * LANES, 2 * LANES), 1) // HEAD_DIM)).astype(BF16)

    def head_sum_all(x):
        n = x.shape[0] // 2
        s2 = _dot_exact01(jnp.concatenate([x[:n], x[n:]], axis=1), bd2)
        return jnp.concatenate([s2[:, :LANES], s2[:, LANES:]], axis=0)

    rkr = head_sum_all(jnp.concatenate(
        [r[u] * k[u] * rk_ref[:, lanes_of(u)] for u in chunks], axis=0))
    bonus = [rkr[u * L:(u + 1) * L] * v[u] for u in chunks]

    inv_n = 1.0 / HEAD_DIM
    hstate = [h_ref[q] for q in range(npair)]
    ys = [None] * len(chunks)
    for c in range(nc):
        for q in range(npair):
            u = q * nc + c
            res = _dot(jnp.concatenate([rq[u], gmat[u]], axis=0).astype(BF16),
                       hstate[q].astype(BF16))
            ys[u] = res[:L] + y0[u]
            hstate[q] = res[L:] + cmat[u]
    for q in range(npair):
        h_ref[q] = hstate[q]
    y = jnp.concatenate(ys, axis=0)
    mean = head_sum_all(y) * inv_n
    yc = y - mean
    var = head_sum_all(yc * yc) * inv_n
    yn = yc * lax.rsqrt(var + GN_EPS)
    for u in chunks:
        rows = pl.ds((u % nc) * L, L)
        o_ref[rows, lanes_of(u)] = (
            (yn[u * L:(u + 1) * L] * lnw_ref[:, lanes_of(u)] + lnb_ref[:, lanes_of(u)] + bonus[u])
            * g_ref[rows, lanes_of(u)]).astype(BF16)


def _scan(P, col_r, col_v, mu_r, mu_v, arrs, rk, lnw, lnb, batch, seq, nc=4, npair=4):
    k, kk, b, lw, g = [a.reshape(batch, seq, a.shape[1]) for a in arrs]
    P3 = P.reshape(batch, seq, P.shape[1])
    C = k.shape[2]
    width = LANES * npair
    rows = CHUNK * nc
    blk = pl.BlockSpec((None, rows, width), lambda bi, p, c: (bi, c, p))
    par = pl.BlockSpec((1, width), lambda bi, p, c: (0, p))

    def raw(col):
        return pl.BlockSpec((None, rows, width), lambda bi, p, c: (bi, c, col // width + p))

    def raw_prev(col):
        return pl.BlockSpec((None, 8, width),
                            lambda bi, p, c: (bi, jnp.maximum(c * (rows // 8) - 1, 0),
                                              col // width + p))

    out = pl.pallas_call(
        functools.partial(_scan_kernel, nc=nc, npair=npair),
        out_shape=jax.ShapeDtypeStruct((batch, seq, C), BF16),
        grid=(batch, C // width, seq // rows),
        in_specs=[raw(col_r), raw(col_v), raw_prev(col_r), raw_prev(col_v), par, par]
                 + [blk] * 5 + [par, par, par],
        out_specs=blk,
        scratch_shapes=[pltpu.VMEM((npair, LANES, LANES), F32)],
        compiler_params=_cparams(("parallel", "parallel", "arbitrary")),
        name="rwkv_scan",
    )(P3, P3, P3, P3, mu_r, mu_v, k, kk, b, lw, g, rk, lnw, lnb)
    return out.reshape(batch * seq, C)


def _sb_kernel(q_ref, k_ref, v_ref, o_ref, *, npair, nq):
    tq = tk = SB_BLOCK
    nu = SB_BLOCKS_PER_STEP
    i = pl.program_id(2)
    lane = lax.broadcasted_iota(jnp.int32, (1, LANES), 1)
    head0 = lane < HEAD_DIM
    scale = jnp.asarray(1.0 / math.sqrt(HEAD_DIM), BF16)
    streams = [(s, hp) for s in range(nq) for hp in range(npair)]
    ns = len(streams)
    q = [q_ref[s * tq:(s + 1) * tq, hp * LANES:(hp + 1) * LANES] * scale for (s, hp) in streams]
    r2 = lax.broadcasted_iota(jnp.int32, (2 * tk, 2 * tk), 0)
    c2 = lax.broadcasted_iota(jnp.int32, (2 * tk, 2 * tk), 1)
    after01 = jnp.logical_and(r2 > c2, (r2 < tk) == (c2 < tk)).astype(BF16)
    key_minus_query = (lax.broadcasted_iota(jnp.int32, (tq, 2 * tk), 1) % tk
                       - lax.broadcasted_iota(jnp.int32, (tq, 2 * tk), 0))
    units = [(p, u) for p in range(ns) for u in range(nu)]

    def by_head_rows(x):
        zero = jnp.zeros_like(x)
        return jnp.concatenate([jnp.where(head0, x, zero), jnp.where(head0, zero, x)], axis=0)

    def both(c0, c1):
        return jnp.concatenate([jnp.broadcast_to(c0, (tq, tk)), jnp.broadcast_to(c1, (tq, tk))],
                               axis=1)

    def body(carry):
        jt, _, offs, accs = carry
        z, sp, log_keep, mask, vb = {}, {}, {}, {}, {}
        for (p, u) in units:
            s, hp = streams[p]
            jb = jt - (nq - 1 - s) - u
            start = pl.multiple_of(jnp.maximum(jb, 0) * tk, tk)
            lanes = slice(hp * LANES, (hp + 1) * LANES)
            kb = by_head_rows(k_ref[pl.ds(start, tk), lanes])
            vb[p, u] = by_head_rows(v_ref[pl.ds(start, tk), lanes])
            zu = _dot_nt(q[p], kb)
            if u > 0:
                zu = zu + jnp.where(jb >= 0, 0.0, -1e30)
            spu = jnp.maximum(zu, 0.0) + jnp.log(1.0 + jnp.exp(-jnp.abs(zu)))
            z[p, u], sp[p, u] = zu, spu
            if u == 0:
                q_start = (i * nq + s) * tq
                mask[p, u] = key_minus_query < jnp.where(jb >= 0, q_start - jb * tk, -tq)
                log_keep[p, u] = jnp.where(mask[p, u], -spu, 0.0)
            else:
                log_keep[p, u] = -spu
        parts = []
        for key in units:
            parts.extend(_split2(log_keep[key]))
        loc = _dot(jnp.concatenate(parts, axis=0), after01)
        offs, accs = list(offs), list(accs)
        for n, (p, u) in enumerate(units):
            local = loc[2 * n * tq:(2 * n + 1) * tq] + loc[(2 * n + 1) * tq:(2 * n + 2) * tq]
            later = local + both(offs[2 * p], offs[2 * p + 1])
            att = jnp.exp(z[p, u] - sp[p, u] + later)
            if u == 0:
                att = jnp.where(mask[p, u], att, 0.0)
            accs[p] = accs[p] + _dot(att.astype(BF16), vb[p, u])
            offs[2 * p] = offs[2 * p] + (local[:, 0:1] + log_keep[p, u][:, 0:1])
            offs[2 * p + 1] = offs[2 * p + 1] + (local[:, tk:tk + 1] + log_keep[p, u][:, tk:tk + 1])
        top = offs[0]
        for o in offs[1:]:
            top = jnp.maximum(top, o)
        alive = (jnp.max(top) > -SB_UNDERFLOW).astype(jnp.int32)
        return jt - nu, alive, tuple(offs), tuple(accs)

    def cond(carry):
        jt, alive = carry[0], carry[1]
        return jnp.logical_and(jt >= 0, alive > 0)

    init = (i * nq + nq - 1, jnp.int32(1),
            tuple(jnp.zeros((tq, 1), F32) for _ in range(2 * ns)),
            tuple(jnp.zeros((tq, LANES), F32) for _ in range(ns)))
    accs = lax.while_loop(cond, body, body(init))[3]
    for p, (s, hp) in enumerate(streams):
        o_ref[s * tq:(s + 1) * tq, hp * LANES:(hp + 1) * LANES] = accs[p].astype(BF16)


def _stick_breaking(qkv, width, npair=2, nq=2):
    B, S, _ = qkv.shape
    w = LANES * npair
    nblk = width // w
    rows = SB_BLOCK * nq
    out = pl.pallas_call(
        functools.partial(_sb_kernel, npair=npair, nq=nq),
        out_shape=jax.ShapeDtypeStruct((B, S, width), BF16),
        grid=(B, nblk, S // rows),
        in_specs=[pl.BlockSpec((None, rows, w), lambda b, p, i: (b, i, p)),
                  pl.BlockSpec((None, S, w), lambda b, p, i: (b, 0, nblk + p)),
                  pl.BlockSpec((None, S, w), lambda b, p, i: (b, 0, 2 * nblk + p))],
        out_specs=pl.BlockSpec((None, rows, w), lambda b, p, i: (b, i, p)),
        compiler_params=_cparams(("parallel", "parallel", "arbitrary")),
        name="stick_breaking",
    )(qkv, qkv, qkv)
    return out.reshape(B * S, width)


def _merge_kernel(ya_ref, yb_ref, ga_ref, gb_ref, x_ref, mod_ref, g1_ref,
                  g2_ref, wua_ref, wub_ref, wo_ref, o_ref, hf_ref, *, d):
    ua = _dot(ya_ref[...], wua_ref[...])
    ub = _dot(yb_ref[...], wub_ref[...])
    merged = jax.nn.sigmoid(ga_ref[...]) * ua + jax.nn.sigmoid(gb_ref[...]) * ub
    mix = _dot(merged.astype(BF16), wo_ref[...])
    gate_m = mod_ref[:, 2 * d:3 * d]
    x1 = x_ref[...] + gate_m * _rms(mix, g1_ref[...])
    o_ref[...] = x1
    shift_f = mod_ref[:, 3 * d:4 * d]
    scale_f = mod_ref[:, 4 * d:5 * d]
    hf_ref[...] = (_rms(x1, g2_ref[...]) * (1.0 + scale_f) + shift_f).astype(BF16)


def _merge(ya, yb, P, x2, mod3, g1, g2, wua, wub, wo, seq, tm=256):
    T, D = x2.shape
    C = ya.shape[1]

    def const(shape):
        return pl.BlockSpec(shape, lambda i: (0, 0), pipeline_mode=pl.Buffered(1))

    return pl.pallas_call(
        functools.partial(_merge_kernel, d=D),
        out_shape=[jax.ShapeDtypeStruct((T, D), F32), jax.ShapeDtypeStruct((T, D), BF16)],
        grid=(T // tm,),
        in_specs=[pl.BlockSpec((tm, C), lambda i: (i, 0)),
                  pl.BlockSpec((tm, C), lambda i: (i, 0)),
                  pl.BlockSpec((tm, D), lambda i: (i, 0)),
                  pl.BlockSpec((tm, D), lambda i: (i, 1)),
                  pl.BlockSpec((tm, D), lambda i: (i, 0)),
                  pl.BlockSpec((None, 1, mod3.shape[2]), lambda i: ((i * tm) // seq, 0, 0)),
                  pl.BlockSpec((1, D), lambda i: (0, 0)),
                  pl.BlockSpec((1, D), lambda i: (0, 0)),
                  const((C, D)), const((C, D)), const((D, D))],
        out_specs=[pl.BlockSpec((tm, D), lambda i: (i, 0)), pl.BlockSpec((tm, D), lambda i: (i, 0))],
        compiler_params=_cparams(("parallel",)),
        name="merge_out",
    )(ya, yb, P, P, x2, mod3, g1, g2, wua, wub, wo)


def _mlp_kernel(xp_ref, hf_ref, modp_ref, g3_ref, w1_ref, w2_ref, o_ref, acc_ref, *, d, rc):
    i = pl.program_id(0)
    j = pl.program_id(1)
    n = pl.num_programs(0) - 1
    slot = i % 2

    @pl.when(jnp.logical_and(i == 0, j == 0))
    def _():
        acc_ref[1] = jnp.zeros(acc_ref.shape[1:], F32)

    def finish_prev_rows():
        rows = pl.ds(pl.multiple_of(j * rc, rc), rc)
        gate_f = modp_ref[:, 5 * d:6 * d]
        o_ref[rows, :] = xp_ref[rows, :] + gate_f * _rms(acc_ref[1 - slot, rows, :], g3_ref[...])

    def partial_out():
        u = jnp.maximum(_dot(hf_ref[...], w1_ref[...]), 0.0)
        return _dot((u * u).astype(BF16), w2_ref[...])

    @pl.when(jnp.logical_and(i < n, j == 0))
    def _():
        finish_prev_rows()
        acc_ref[slot] = partial_out()

    @pl.when(jnp.logical_and(i < n, j > 0))
    def _():
        finish_prev_rows()
        acc_ref[slot] += partial_out()

    @pl.when(i == n)
    def _():
        finish_prev_rows()


def _mlp(x2, hf, mod3, g3, w1, w2, seq, tm=512, tf=1024):
    T, D = x2.shape
    F = w1.shape[1]
    n, nj = T // tm, F // tf
    assert tm % nj == 0 and (tm // nj) % 8 == 0

    def cur(i):
        return jnp.minimum(i, n - 1)

    def prv(i):
        return jnp.maximum(i - 1, 0)

    def ff(i, j):
        return jnp.where(i < n, j, nj - 1)

    return pl.pallas_call(
        functools.partial(_mlp_kernel, d=D, rc=tm // nj),
        out_shape=jax.ShapeDtypeStruct((T, D), F32),
        grid=(n + 1, nj),
        in_specs=[pl.BlockSpec((tm, D), lambda i, j: (prv(i), 0)),
                  pl.BlockSpec((tm, D), lambda i, j: (cur(i), 0)),
                  pl.BlockSpec((None, 1, mod3.shape[2]), lambda i, j: ((prv(i) * tm) // seq, 0, 0)),
                  pl.BlockSpec((1, D), lambda i, j: (0, 0)),
                  pl.BlockSpec((D, tf), lambda i, j: (0, ff(i, j))),
                  pl.BlockSpec((tf, D), lambda i, j: (ff(i, j), 0))],
        out_specs=pl.BlockSpec((tm, D), lambda i, j: (prv(i), 0)),
        scratch_shapes=[pltpu.VMEM((2, tm, D), F32)],
        compiler_params=_cparams(("arbitrary", "arbitrary")),
        name="mlp",
    )(x2, hf, mod3, g3, w1, w2)


def _pad_rows(w, offset, total):
    return jnp.zeros((total, w.shape[1]), w.dtype).at[offset:offset + w.shape[0]].set(w)


def kernel(x, c, w_ada, b_ada, norm_g, w_in, mu_shift, w0, w2, a0, a2, g2, k_k, k_a,
           r_k, ln_x_w, ln_x_b, w_up_rwkv, w_up_sb, w_out, w_mlp_in, w_mlp_out):
    B, S, D = x.shape
    depth = w_in.shape[0]
    C = w0.shape[1]
    W = w_up_sb.shape[1]
    rwkv_cols = 3 * C + DECAY_LORA + ICLR_LORA + GATE_LORA
    sb_cols = 3 * W
    n_lora = DECAY_LORA + ICLR_LORA + GATE_LORA
    assert C == 1024 and W == 1024 and D == 2 * C and S % (2 * CHUNK) == 0

    hi = lax.broadcasted_iota(jnp.int32, (LANES, LANES), 0) // HEAD_DIM
    hj = lax.broadcasted_iota(jnp.int32, (LANES, LANES), 1) // HEAD_DIM
    bd = (hi == hj).astype(BF16)

    x2 = x.reshape(B * S, D)
    for l in range(depth):
        mod = _ada(c, w_ada[l], b_ada[l])
        mod3 = mod.reshape(B, 1, mod.shape[1])

        g_off = rwkv_cols + sb_cols
        n_main_cols = 2 * D + 3 * C + LORA_PAD
        w_perm = _regroup_rows(
            w_in[l].T,
            [(g_off, 2 * D), (0, 3 * C), (3 * C, LORA_PAD), (rwkv_cols, sb_cols)],
            zero_lo=2 * D + 3 * C + n_lora, zero_hi=n_main_cols)
        P, qkv = _inproj(x2, mod3, norm_g[l, 0].reshape(1, D), w_perm, n_main_cols, S)

        mu = mu_shift[l]
        consts = dict(
            col_k=2 * D // C + 1, col_l=(2 * D + 3 * C) // LORA_PAD,
            mu_k=mu[None, C:2 * C],
            mu_l=jnp.pad(mu[None, 3 * C:], ((0, 0), (0, LORA_PAD - n_lora))),
            w0=w0[l][None], a0=a0[l][None], k_k=k_k[l][None], k_a=k_a[l][None],
            w2p=_pad_rows(w2[l], 0, LORA_PAD).astype(BF16),
            a2p=_pad_rows(a2[l], DECAY_LORA, LORA_PAD).astype(BF16),
            g2p=_pad_rows(g2[l], DECAY_LORA + ICLR_LORA, LORA_PAD).astype(BF16),
            bd=bd)
        prep = _prep(P, S, consts)
        ya = _scan(P, 2 * D, 2 * D + 2 * C, mu[None, :C], mu[None, 2 * C:3 * C], prep,
                   r_k[l].reshape(1, C), ln_x_w[l][None], ln_x_b[l][None], B, S)

        yb = _stick_breaking(qkv.reshape(B, S, 3 * W), W)

        x2, hf = _merge(ya, yb, P, x2, mod3, norm_g[l, 1].reshape(1, D),
                        norm_g[l, 2].reshape(1, D), w_up_rwkv[l].astype(BF16),
                        w_up_sb[l].astype(BF16), w_out[l].astype(BF16), S)
        x2 = _mlp(x2, hf, mod3, norm_g[l, 3].reshape(1, D),
                  w_mlp_in[l].astype(BF16), w_mlp_out[l].astype(BF16), S)
    return x2.reshape(B, S, D)
```

```python
import functools
import math

import jax
import jax.numpy as jnp
from jax import lax
from jax.experimental import pallas as pl
from jax.experimental.pallas import tpu as pltpu

F32 = jnp.float32
BF16 = jnp.bfloat16

HEAD_DIM = 64
LANES = 128
DECAY_LORA = 64
ICLR_LORA = 64
GATE_LORA = 160
LORA_PAD = 512
NORM_EPS = 1e-6
GN_EPS = 64e-5
CHUNK = 128
SB_BLOCK = 128
SB_BLOCKS_PER_STEP = 3
SB_UNDERFLOW = 104.0
VMEM_LIMIT = 56 * 1024 * 1024


def _cparams(sem, vmem=VMEM_LIMIT):
    return pltpu.CompilerParams(dimension_semantics=sem, vmem_limit_bytes=vmem)


def _dot(a, b):
    return jnp.dot(a, b, preferred_element_type=F32)


def _dot_nt(a, b):
    return lax.dot_general(a, b, (((1,), (1,)), ((), ())), preferred_element_type=F32)


def _dot_tn(a, b):
    return lax.dot_general(a, b, (((0,), (0,)), ((), ())), preferred_element_type=F32)


def _split2(x):
    hi = x.astype(BF16)
    lo = (x - hi.astype(F32)).astype(BF16)
    return hi, lo


def _dot_exact01(x, m01):
    n = x.shape[0]
    res = _dot(jnp.concatenate(_split2(x), axis=0), m01)
    return res[:n] + res[n:]


def _exact01_dot(m01, x):
    n = x.shape[1]
    res = _dot(m01, jnp.concatenate(_split2(x), axis=1))
    return res[:, :n] + res[:, n:]


def _head_sum(x, bd):
    return _dot_exact01(x, bd)


def _rms(x, g):
    return x * lax.rsqrt(jnp.mean(x * x, axis=-1, keepdims=True) + NORM_EPS) * g


def _ada_kernel(c_ref, w_ref, b_ref, o_ref):
    c = c_ref[...]
    s = c * jax.nn.sigmoid(c)
    o_ref[...] = _dot(s.astype(BF16), w_ref[...].astype(BF16)) + b_ref[...]


def _ada(c, w_ada, b_ada, tn=3072):
    B, D = c.shape
    N = w_ada.shape[1]
    rows = 8
    c_pad = jnp.zeros((rows, D), F32).at[:B].set(c)
    out = pl.pallas_call(
        _ada_kernel,
        out_shape=jax.ShapeDtypeStruct((rows, N), F32),
        grid=(N // tn,),
        in_specs=[pl.BlockSpec((rows, D), lambda j: (0, 0)),
                  pl.BlockSpec((D, tn), lambda j: (0, j)),
                  pl.BlockSpec((1, tn), lambda j: (0, j))],
        out_specs=pl.BlockSpec((rows, tn), lambda j: (0, j)),
        compiler_params=_cparams(("parallel",)),
        name="ada",
    )(c_pad, w_ada, b_ada.reshape(1, N))
    return out[:B]


def _regroup_kernel(offs_ref, w_ref, o_ref, *, zero_lo, zero_hi, rows):
    del offs_ref
    out_row = pl.program_id(0) * rows + lax.broadcasted_iota(jnp.int32, (rows, 1), 0)
    pad = jnp.logical_and(out_row >= zero_lo, out_row < zero_hi)
    o_ref[...] = jnp.where(pad, 0.0, w_ref[...]).astype(BF16)


def _regroup_rows(wt, groups, zero_lo, zero_hi, rows=512):
    D = wt.shape[1]
    sub = 8
    assert all(s % sub == 0 and n % rows == 0 for s, n in groups)
    offs = jnp.asarray([(s + r) // sub for (s, n) in groups for r in range(0, n, rows)], jnp.int32)
    n_out = sum(n for _, n in groups)
    return pl.pallas_call(
        functools.partial(_regroup_kernel, zero_lo=zero_lo, zero_hi=zero_hi, rows=rows),
        out_shape=jax.ShapeDtypeStruct((n_out, D), BF16),
        grid_spec=pltpu.PrefetchScalarGridSpec(
            num_scalar_prefetch=1, grid=(n_out // rows,),
            in_specs=[pl.BlockSpec((pl.Element(rows), pl.Element(D)),
                                   lambda t, offs: (offs[t] * sub, 0))],
            out_specs=pl.BlockSpec((rows, D), lambda t, offs: (t, 0))),
        compiler_params=_cparams(("parallel",)),
        name="regroup_w_in",
    )(offs, wt)


def _inproj_kernel(x0_ref, mod0_ref, xn_ref, modn_ref, g_ref, w_ref, o_ref, sb_ref, h_ref,
                   *, d, n_main, chunk):
    i = pl.program_id(0)
    j = pl.program_id(1)
    slot = i % 2
    tm = xn_ref.shape[0]

    def modulated(x, mod_ref):
        shift = mod_ref[:, 0:d]
        scale = mod_ref[:, d:2 * d]
        return (_rms(x, g_ref[...]) * (1.0 + scale) + shift).astype(BF16)

    @pl.when(jnp.logical_and(i == 0, j == 0))
    def _():
        h_ref[0] = modulated(x0_ref[...], mod0_ref)

    def prepare_next_rows():
        r0 = pl.multiple_of(jnp.minimum(j * chunk, tm - chunk), 16)
        h_ref[1 - slot, pl.ds(r0, chunk), :] = modulated(xn_ref[pl.ds(r0, chunk), :], modn_ref)

    @pl.when(j < n_main)
    def _():
        o_ref[...] = _dot_nt(h_ref[slot], w_ref[...])
        prepare_next_rows()

    @pl.when(j >= n_main)
    def _():
        sb_ref[...] = _dot_nt(h_ref[slot], w_ref[...]).astype(BF16)
        prepare_next_rows()


def _inproj(x2, mod3, g0, w_t, n_main_cols, seq, tm=1024, tn=768):
    T, D = x2.shape
    N = w_t.shape[0]
    n_main = n_main_cols // tn
    n_i, n_j = T // tm, N // tn
    chunk = 16 * pl.cdiv(pl.cdiv(tm, n_j), 16)

    def nxt(i):
        return jnp.minimum(i + 1, n_i - 1)

    return pl.pallas_call(
        functools.partial(_inproj_kernel, d=D, n_main=n_main, chunk=chunk),
        out_shape=[jax.ShapeDtypeStruct((T, n_main_cols), F32),
                   jax.ShapeDtypeStruct((T, N - n_main_cols), BF16)],
        grid=(n_i, n_j),
        in_specs=[pl.BlockSpec((tm, D), lambda i, j: (0, 0), pipeline_mode=pl.Buffered(1)),
                  pl.BlockSpec((None, 1, mod3.shape[2]), lambda i, j: (0, 0, 0)),
                  pl.BlockSpec((tm, D), lambda i, j: (nxt(i), 0)),
                  pl.BlockSpec((None, 1, mod3.shape[2]), lambda i, j: ((nxt(i) * tm) // seq, 0, 0)),
                  pl.BlockSpec((1, D), lambda i, j: (0, 0)),
                  pl.BlockSpec((tn, D), lambda i, j: (j, 0))],
        out_specs=[pl.BlockSpec((tm, tn), lambda i, j: (i, jnp.minimum(j, n_main - 1))),
                   pl.BlockSpec((tm, tn), lambda i, j: (i, jnp.maximum(j - n_main, 0)))],
        scratch_shapes=[pltpu.VMEM((2, tm, D), BF16)],
        compiler_params=_cparams(("arbitrary", "arbitrary")),
        name="inproj",
    )(x2, mod3, x2, mod3, g0, w_t)


def _token_shift_mix(cur_ref, prev_ref, mu_ref, first):
    cur = cur_ref[...]
    row = lax.broadcasted_iota(jnp.int32, (cur.shape[0], 1), 0)
    prev_row = jnp.where(first, 0.0, prev_ref[7:8, :])
    shifted = jnp.where(row == 0, prev_row, pltpu.roll(cur, 1, 0))
    return cur + (shifted - cur) * mu_ref[...]


def _prep_kernel(k_ref, l_ref, kp_ref, lp_ref, muk_ref, mul_ref, w0_ref, a0_ref, kk_ref, ka_ref,
                 w2_ref, a2_ref, g2_ref, bd_ref,
                 ko_ref, kko_ref, bo_ref, lwo_ref, go_ref, *, tm, seq):
    first = (pl.program_id(0) * tm) % seq == 0
    k = _token_shift_mix(k_ref, kp_ref, muk_ref, first)
    lo = _token_shift_mix(l_ref, lp_ref, mul_ref, first)

    dw = _dot(jnp.tanh(lo).astype(BF16), w2_ref[...])
    da = _dot(lo.astype(BF16), a2_ref[...])
    g = _dot(jax.nn.sigmoid(lo).astype(BF16), g2_ref[...])

    t = -(w0_ref[...] + dw)
    softplus_t = jnp.maximum(t, 0.0) + jnp.log(1.0 + jnp.exp(-jnp.abs(t)))
    w_log = -softplus_t - 0.5
    lw = -jnp.exp(w_log)
    a = jax.nn.sigmoid(a0_ref[...] + da)
    kx = k * kk_ref[...]
    bd = bd_ref[...]
    n_blk = kx.shape[1] // LANES
    sq = kx * kx
    ssum = jnp.concatenate(
        [_head_sum(sq[:, c * LANES:(c + 1) * LANES], bd) for c in range(n_blk)], axis=1)
    kk = kx * lax.rsqrt(jnp.maximum(ssum, 1e-24))
    k2 = k * (1.0 + (a - 1.0) * ka_ref[...])

    ko_ref[...] = k2
    kko_ref[...] = kk
    bo_ref[...] = kk * a
    lwo_ref[...] = lw
    go_ref[...] = g


def _prep(P, seq, consts, tm=512):
    T = P.shape[0]
    C = consts["w0"].shape[1]
    col_k, col_l = consts["col_k"], consts["col_l"]

    def cur(col, w):
        return pl.BlockSpec((tm, w), lambda i: (i, col))

    def prev(col, w):
        return pl.BlockSpec((8, w), lambda i: (jnp.maximum(i * (tm // 8) - 1, 0), col))

    def const(shape):
        return pl.BlockSpec(shape, lambda i: (0, 0))

    out_sd = jax.ShapeDtypeStruct((T, C), F32)
    outs = pl.pallas_call(
        functools.partial(_prep_kernel, tm=tm, seq=seq),
        out_shape=[out_sd] * 5,
        grid=(T // tm,),
        in_specs=[cur(col_k, C), cur(col_l, LORA_PAD), prev(col_k, C), prev(col_l, LORA_PAD),
                  const((1, C)), const((1, LORA_PAD)),
                  const((1, C)), const((1, C)), const((1, C)), const((1, C)),
                  const((LORA_PAD, C)), const((LORA_PAD, C)), const((LORA_PAD, C)),
                  const((LANES, LANES))],
        out_specs=[pl.BlockSpec((tm, C), lambda i: (i, 0))] * 5,
        compiler_params=_cparams(("parallel",)),
        name="rwkv_prep",
    )(P, P, P, P, consts["mu_k"], consts["mu_l"],
      consts["w0"], consts["a0"], consts["k_k"], consts["k_a"],
      consts["w2p"], consts["a2p"], consts["g2p"], consts["bd"])
    return outs


def _scan_kernel(r_ref, v_ref, rp_ref, vp_ref, mur_ref, muv_ref, k_ref, kk_ref, b_ref, lw_ref,
                 g_ref, rk_ref, lnw_ref, lnb_ref, o_ref, h_ref, *, nc, npair):
    L = CHUNK
    half = L // 2

    @pl.when(pl.program_id(2) == 0)
    def _():
        h_ref[...] = jnp.zeros_like(h_ref)

    ri = lax.broadcasted_iota(jnp.int32, (L, L), 0)
    ci = lax.broadcasted_iota(jnp.int32, (L, L), 1)
    lower_incl = ri >= ci
    lower_strict = ri > ci
    eye = ri == ci
    same_head = (ri < HEAD_DIM) == (ci < HEAD_DIM)
    tri01 = lower_incl.astype(BF16)
    eye_f = eye.astype(F32)
    lane = lax.broadcasted_iota(jnp.int32, (1, LANES), 1)
    head0 = lane < HEAD_DIM
    lane2 = lax.broadcasted_iota(jnp.int32, (1, 2 * LANES), 1)
    head0_2 = (lane2 % LANES) < HEAD_DIM
    tri_stack = jnp.concatenate([lower_strict, lower_incl], axis=0)
    chunks = range(nc * npair)
    pairs = [(c, h) for c in chunks for h in range(2)]

    def lanes_of(u):
        return slice((u // nc) * LANES, (u // nc + 1) * LANES)

    def ld(ref, u):
        return ref[pl.ds((u % nc) * L, L), lanes_of(u)]

    first = pl.program_id(2) == 0
    r_tile = _token_shift_mix(r_ref, rp_ref, mur_ref, first)
    v_tile = _token_shift_mix(v_ref, vp_ref, muv_ref, first)
    r = [r_tile[(u % nc) * L:(u % nc + 1) * L, lanes_of(u)] for u in chunks]
    v = [v_tile[(u % nc) * L:(u % nc + 1) * L, lanes_of(u)] for u in chunks]
    k = [ld(k_ref, c) for c in chunks]
    kk = [ld(kk_ref, c) for c in chunks]
    b = [ld(b_ref, c) for c in chunks]
    lw = [ld(lw_ref, c) for c in chunks]

    cum = [_exact01_dot(tri01, lw[c]) for c in chunks]
    lhs_mid, rhs_mid, kd_bf, r_dec, k_end, b_end, p_end, v_bf = [], [], [], [], [], [], [], []
    for c in chunks:
        mid = cum[c][half - 1:half, :]
        end = cum[c][L - 1:L, :]
        r_mid = r[c] * jnp.exp(cum[c] - mid)
        kk_mid = kk[c] * jnp.exp(cum[c] - lw[c] - mid)
        e_back = jnp.exp(mid - cum[c])
        lhs_mid.append(jnp.concatenate([kk_mid, r_mid], axis=0))
        rhs_mid.append(jnp.concatenate([b[c] * e_back, k[c] * e_back], axis=0).astype(BF16))
        kd_bf.append((kk[c] * jnp.exp(cum[c] - lw[c])).astype(BF16))
        r_dec.append(r[c] * jnp.exp(cum[c]))
        e_end = jnp.exp(end - cum[c])
        k_end.append((k[c] * e_end).astype(BF16))
        b_end.append((b[c] * e_end).astype(BF16))
        p_end.append(jnp.exp(end))
        v_bf.append(v[c].astype(BF16))

    aa = {}
    for (c, h) in pairs:
        hm = head0 if h == 0 else jnp.logical_not(head0)
        aa[c, h] = _dot_nt(jnp.where(hm, lhs_mid[c], 0.0).astype(BF16), rhs_mid[c])
    def side(f):
        return jnp.concatenate([f(0), f(1)], axis=1)

    tri2 = jnp.concatenate([tri_stack, tri_stack], axis=1)
    strict2 = jnp.concatenate([lower_strict, lower_strict], axis=1)
    incl2 = jnp.concatenate([lower_incl, lower_incl], axis=1)
    def by_head_rows(x):
        hm = head0 if x.shape[1] == LANES else head0_2
        zero = jnp.zeros_like(x)
        return jnp.concatenate([jnp.where(hm, x, zero), jnp.where(hm, zero, x)], axis=0)

    a_kb = [jnp.where(strict2, side(lambda h: aa[c, h][:L, :L]), 0.0) for c in chunks]
    a_rb = [jnp.where(incl2, side(lambda h: aa[c, h][L:, :L]), 0.0).astype(BF16) for c in chunks]
    a_xk = [jnp.where(tri2, side(lambda h: aa[c, h][:, L:]), 0.0).astype(BF16) for c in chunks]
    avr = [_dot(a_xk[c], by_head_rows(v_bf[c])) for c in chunks]

    hl = L // 2
    blk_r = lax.broadcasted_iota(jnp.int32, (2 * L, 2 * L), 0) // hl
    blk_c = lax.broadcasted_iota(jnp.int32, (2 * L, 2 * L), 1) // hl
    diag4 = blk_r == blk_c
    below4 = jnp.logical_and(blk_r == blk_c + 1, blk_c % 2 == 0)
    eye4 = (lax.broadcasted_iota(jnp.int32, (hl, 2 * L), 0)
            == lax.broadcasted_iota(jnp.int32, (hl, 2 * L), 1) % hl).astype(F32)

    def tile4(x, keep):
        return jnp.where(keep, jnp.concatenate([x, x, x, x], axis=0), jnp.zeros((), x.dtype))

    n_lvl = int(math.log2(hl)) - 1
    p, t, a21 = [], [], []
    for c in chunks:
        pd = jnp.where(head0_2, a_kb[c][:hl], a_kb[c][hl:])
        a21.append(jnp.where(head0_2, a_kb[c][hl:], 0.0).astype(BF16))
        pd_bf = pd.astype(BF16)
        p.append(_dot(pd_bf, tile4(pd_bf, diag4)))
        t.append(eye4 - pd)
    for lvl in range(n_lvl):
        for c in chunks:
            p_bd = tile4(p[c].astype(BF16), diag4)
            if lvl + 1 < n_lvl:
                tp = _dot(jnp.concatenate([t[c], p[c]], axis=0).astype(BF16), p_bd)
                t[c] = t[c] + tp[:hl]
                p[c] = tp[hl:]
            else:
                t[c] = t[c] + _dot(t[c].astype(BF16), p_bd)
    for c in chunks:
        td_bf = t[c].astype(BF16)
        y = _dot(a21[c], tile4(jnp.where(head0_2, td_bf, jnp.zeros((), BF16)), diag4))
        z = _dot(td_bf, tile4(y.astype(BF16), below4))
        t[c] = jnp.concatenate([jnp.where(head0_2, t[c], 0.0), jnp.where(head0_2, -z, t[c])], axis=0)

    rq, y0, gmat, cmat = [], [], [], []
    for c in chunks:
        av = avr[c][:L]
        arkv = avr[c][L:]
        x = jnp.concatenate([kd_bf[c], av.astype(BF16)], axis=1)
        w_bf = _dot(t[c].astype(BF16), by_head_rows(x)).astype(BF16)
        aw = _dot(a_rb[c], by_head_rows(w_bf))
        bw = _dot_tn(b_end[c], w_bf)
        kv = _dot_tn(k_end[c], v_bf[c])
        rq.append(r_dec[c] - aw[:, :LANES])
        y0.append(arkv - aw[:, LANES:])
        gmat.append(eye_f * p_end[c] - jnp.where(same_head, bw[:, :LANES], 0.0))
        cmat.append(jnp.where(same_head, kv - bw[:, LANES:], 0.0))

    bd2 = ((lax.broadcasted_iota(jnp.int32, (2 * LANES, 2 * LANES), 0) // HEAD_DIM)
           == (lax.broadcasted_iota(jnp.int32, (2 * LANES, 2 * LANES), 1) // HEAD_DIM)).astype(BF16)

    def head_sum_all(x):
        n = x.shape[0] // 2
        s2 = _dot_exact01(jnp.concatenate([x[:n], x[n:]], axis=1), bd2)
        return jnp.concatenate([s2[:, :LANES], s2[:, LANES:]], axis=0)

    rkr = head_sum_all(jnp.concatenate(
        [r[u] * k[u] * rk_ref[:, lanes_of(u)] for u in chunks], axis=0))
    bonus = [rkr[u * L:(u + 1) * L] * v[u] for u in chunks]

    inv_n = 1.0 / HEAD_DIM
    hstate = [h_ref[q] for q in range(npair)]
    ys = [None] * len(chunks)
    for c in range(nc):
        for q in range(npair):
            u = q * nc + c
            res = _dot(jnp.concatenate([rq[u], gmat[u]], axis=0).astype(BF16),
                       hstate[q].astype(BF16))
            ys[u] = res[:L] + y0[u]
            hstate[q] = res[L:] + cmat[u]
    for q in range(npair):
        h_ref[q] = hstate[q]
    y = jnp.concatenate(ys, axis=0)
    mean = head_sum_all(y) * inv_n
    yc = y - mean
    var = head_sum_all(yc * yc) * inv_n
    yn = yc * lax.rsqrt(var + GN_EPS)
    for u in chunks:
        rows = pl.ds((u % nc) * L, L)
        o_ref[rows, lanes_of(u)] = (
            (yn[u * L:(u + 1) * L] * lnw_ref[:, lanes_of(u)] + lnb_ref[:, lanes_of(u)] + bonus[u])
            * g_ref[rows, lanes_of(u)]).astype(BF16)


def _scan(P, col_r, col_v, mu_r, mu_v, arrs, rk, lnw, lnb, batch, seq, nc=4, npair=4):
    k, kk, b, lw, g = [a.reshape(batch, seq, a.shape[1]) for a in arrs]
    P3 = P.reshape(batch, seq, P.shape[1])
    C = k.shape[2]
    width = LANES * npair
    rows = CHUNK * nc
    blk = pl.BlockSpec((None, rows, width), lambda bi, p, c: (bi, c, p))
    par = pl.BlockSpec((1, width), lambda bi, p, c: (0, p))

    def raw(col):
        return pl.BlockSpec((None, rows, width), lambda bi, p, c: (bi, c, col // width + p))

    def raw_prev(col):
        return pl.BlockSpec((None, 8, width),
                            lambda bi, p, c: (bi, jnp.maximum(c * (rows // 8) - 1, 0),
                                              col // width + p))

    out = pl.pallas_call(
        functools.partial(_scan_kernel, nc=nc, npair=npair),
        out_shape=jax.ShapeDtypeStruct((batch, seq, C), BF16),
        grid=(batch, C // width, seq // rows),
        in_specs=[raw(col_r), raw(col_v), raw_prev(col_r), raw_prev(col_v), par, par]
                 + [blk] * 5 + [par, par, par],
        out_specs=blk,
        scratch_shapes=[pltpu.VMEM((npair, LANES, LANES), F32)],
        compiler_params=_cparams(("parallel", "parallel", "arbitrary")),
        name="rwkv_scan",
    )(P3, P3, P3, P3, mu_r, mu_v, k, kk, b, lw, g, rk, lnw, lnb)
    return out.reshape(batch * seq, C)


def _sb_kernel(q_ref, k_ref, v_ref, o_ref, *, npair, nq):
    tq = tk = SB_BLOCK
    nu = SB_BLOCKS_PER_STEP
    i = pl.program_id(2)
    lane = lax.broadcasted_iota(jnp.int32, (1, LANES), 1)
    head0 = lane < HEAD_DIM
    scale = jnp.asarray(1.0 / math.sqrt(HEAD_DIM), BF16)
    streams = [(s, hp) for s in range(nq) for hp in range(npair)]
    ns = len(streams)
    q = [q_ref[s * tq:(s + 1) * tq, hp * LANES:(hp + 1) * LANES] * scale for (s, hp) in streams]
    r2 = lax.broadcasted_iota(jnp.int32, (2 * tk, 2 * tk), 0)
    c2 = lax.broadcasted_iota(jnp.int32, (2 * tk, 2 * tk), 1)
    after01 = jnp.logical_and(r2 > c2, (r2 < tk) == (c2 < tk)).astype(BF16)
    key_minus_query = (lax.broadcasted_iota(jnp.int32, (tq, 2 * tk), 1) % tk
                       - lax.broadcasted_iota(jnp.int32, (tq, 2 * tk), 0))
    units = [(p, u) for p in range(ns) for u in range(nu)]

    def by_head_rows(x):
        zero = jnp.zeros_like(x)
        return jnp.concatenate([jnp.where(head0, x, zero), jnp.where(head0, zero, x)], axis=0)

    def both(c0, c1):
        return jnp.concatenate([jnp.broadcast_to(c0, (tq, tk)), jnp.broadcast_to(c1, (tq, tk))],
                               axis=1)

    def body(carry):
        jt, _, offs, accs = carry
        z, sp, log_keep, mask, vb = {}, {}, {}, {}, {}
        for (p, u) in units:
            s, hp = streams[p]
            jb = jt - (nq - 1 - s) - u
            start = pl.multiple_of(jnp.maximum(jb, 0) * tk, tk)
            lanes = slice(hp * LANES, (hp + 1) * LANES)
            kb = by_head_rows(k_ref[pl.ds(start, tk), lanes])
            vb[p, u] = by_head_rows(v_ref[pl.ds(start, tk), lanes])
            zu = _dot_nt(q[p], kb)
            if u > 0:
                zu = zu + jnp.where(jb >= 0, 0.0, -1e30)
            spu = jnp.maximum(zu, 0.0) + jnp.log(1.0 + jnp.exp(-jnp.abs(zu)))
            z[p, u], sp[p, u] = zu, spu
            if u == 0:
                q_start = (i * nq + s) * tq
                mask[p, u] = key_minus_query < jnp.where(jb >= 0, q_start - jb * tk, -tq)
                log_keep[p, u] = jnp.where(mask[p, u], -spu, 0.0)
            else:
                log_keep[p, u] = -spu
        parts = []
        for key in units:
            parts.extend(_split2(log_keep[key]))
        loc = _dot(jnp.concatenate(parts, axis=0), after01)
        offs, accs = list(offs), list(accs)
        for n, (p, u) in enumerate(units):
            local = loc[2 * n * tq:(2 * n + 1) * tq] + loc[(2 * n + 1) * tq:(2 * n + 2) * tq]
            later = local + both(offs[2 * p], offs[2 * p + 1])
            att = jnp.exp(z[p, u] - sp[p, u] + later)
            if u == 0:
                att = jnp.where(mask[p, u], att, 0.0)
            accs[p] = accs[p] + _dot(att.astype(BF16), vb[p, u])
            offs[2 * p] = offs[2 * p] + (local[:, 0:1] + log_keep[p, u][:, 0:1])
            offs[2 * p + 1] = offs[2 * p + 1] + (local[:, tk:tk + 1] + log_keep[p, u][:, tk:tk + 1])
        top = offs[0]
        for o in offs[1:]:
            top = jnp.maximum(top, o)
        alive = (jnp.max(top) > -SB_UNDERFLOW).astype(jnp.int32)
        return jt - nu, alive, tuple(offs), tuple(accs)

    def cond(carry):
        jt, alive = carry[0], carry[1]
        return jnp.logical_and(jt >= 0, alive > 0)

    init = (i * nq + nq - 1, jnp.int32(1),
            tuple(jnp.zeros((tq, 1), F32) for _ in range(2 * ns)),
            tuple(jnp.zeros((tq, LANES), F32) for _ in range(ns)))
    accs = lax.while_loop(cond, body, body(init))[3]
    for p, (s, hp) in enumerate(streams):
        o_ref[s * tq:(s + 1) * tq, hp * LANES:(hp + 1) * LANES] = accs[p].astype(BF16)


def _stick_breaking(qkv, width, npair=2, nq=2):
    B, S, _ = qkv.shape
    w = LANES * npair
    nblk = width // w
    rows = SB_BLOCK * nq
    out = pl.pallas_call(
        functools.partial(_sb_kernel, npair=npair, nq=nq),
        out_shape=jax.ShapeDtypeStruct((B, S, width), BF16),
        grid=(B, nblk, S // rows),
        in_specs=[pl.BlockSpec((None, rows, w), lambda b, p, i: (b, i, p)),
                  pl.BlockSpec((None, S, w), lambda b, p, i: (b, 0, nblk + p)),
                  pl.BlockSpec((None, S, w), lambda b, p, i: (b, 0, 2 * nblk + p))],
        out_specs=pl.BlockSpec((None, rows, w), lambda b, p, i: (b, i, p)),
        compiler_params=_cparams(("parallel", "parallel", "arbitrary")),
        name="stick_breaking",
    )(qkv, qkv, qkv)
    return out.reshape(B * S, width)


def _merge_kernel(ya_ref, yb_ref, ga_ref, gb_ref, x_ref, mod_ref, g1_ref,
                  g2_ref, wua_ref, wub_ref, wo_ref, o_ref, hf_ref, *, d):
    ua = _dot(ya_ref[...], wua_ref[...])
    ub = _dot(yb_ref[...], wub_ref[...])
    merged = jax.nn.sigmoid(ga_ref[...]) * ua + jax.nn.sigmoid(gb_ref[...]) * ub
    mix = _dot(merged.astype(BF16), wo_ref[...])
    gate_m = mod_ref[:, 2 * d:3 * d]
    x1 = x_ref[...] + gate_m * _rms(mix, g1_ref[...])
    o_ref[...] = x1
    shift_f = mod_ref[:, 3 * d:4 * d]
    scale_f = mod_ref[:, 4 * d:5 * d]
    hf_ref[...] = (_rms(x1, g2_ref[...]) * (1.0 + scale_f) + shift_f).astype(BF16)


def _merge(ya, yb, P, x2, mod3, g1, g2, wua, wub, wo, seq, tm=256):
    T, D = x2.shape
    C = ya.shape[1]

    def const(shape):
        return pl.BlockSpec(shape, lambda i: (0, 0), pipeline_mode=pl.Buffered(1))

    return pl.pallas_call(
        functools.partial(_merge_kernel, d=D),
        out_shape=[jax.ShapeDtypeStruct((T, D), F32), jax.ShapeDtypeStruct((T, D), BF16)],
        grid=(T // tm,),
        in_specs=[pl.BlockSpec((tm, C), lambda i: (i, 0)),
                  pl.BlockSpec((tm, C), lambda i: (i, 0)),
                  pl.BlockSpec((tm, D), lambda i: (i, 0)),
                  pl.BlockSpec((tm, D), lambda i: (i, 1)),
                  pl.BlockSpec((tm, D), lambda i: (i, 0)),
                  pl.BlockSpec((None, 1, mod3.shape[2]), lambda i: ((i * tm) // seq, 0, 0)),
                  pl.BlockSpec((1, D), lambda i: (0, 0)),
                  pl.BlockSpec((1, D), lambda i: (0, 0)),
                  const((C, D)), const((C, D)), const((D, D))],
        out_specs=[pl.BlockSpec((tm, D), lambda i: (i, 0)), pl.BlockSpec((tm, D), lambda i: (i, 0))],
        compiler_params=_cparams(("parallel",)),
        name="merge_out",
    )(ya, yb, P, P, x2, mod3, g1, g2, wua, wub, wo)


def _mlp_kernel(xp_ref, hf_ref, modp_ref, g3_ref, w1_ref, w2_ref, o_ref, acc_ref, *, d, rc):
    i = pl.program_id(0)
    j = pl.program_id(1)
    n = pl.num_programs(0) - 1
    slot = i % 2

    @pl.when(jnp.logical_and(i == 0, j == 0))
    def _():
        acc_ref[1] = jnp.zeros(acc_ref.shape[1:], F32)

    def finish_prev_rows():
        rows = pl.ds(pl.multiple_of(j * rc, rc), rc)
        gate_f = modp_ref[:, 5 * d:6 * d]
        o_ref[rows, :] = xp_ref[rows, :] + gate_f * _rms(acc_ref[1 - slot, rows, :], g3_ref[...])

    def partial_out():
        u = jnp.maximum(_dot(hf_ref[...], w1_ref[...]), 0.0)
        return _dot((u * u).astype(BF16), w2_ref[...])

    @pl.when(jnp.logical_and(i < n, j == 0))
    def _():
        finish_prev_rows()
        acc_ref[slot] = partial_out()

    @pl.when(jnp.logical_and(i < n, j > 0))
    def _():
        finish_prev_rows()
        acc_ref[slot] += partial_out()

    @pl.when(i == n)
    def _():
        finish_prev_rows()


def _mlp(x2, hf, mod3, g3, w1, w2, seq, tm=512, tf=1024):
    T, D = x2.shape
    F = w1.shape[1]
    n, nj = T // tm, F // tf
    assert tm % nj == 0 and (tm // nj) % 8 == 0

    def cur(i):
        return jnp.minimum(i, n - 1)

    def prv(i):
        return jnp.maximum(i - 1, 0)

    def ff(i, j):
        return jnp.where(i < n, j, nj - 1)

    return pl.pallas_call(
        functools.partial(_mlp_kernel, d=D, rc=tm // nj),
        out_shape=jax.ShapeDtypeStruct((T, D), F32),
        grid=(n + 1, nj),
        in_specs=[pl.BlockSpec((tm, D), lambda i, j: (prv(i), 0)),
                  pl.BlockSpec((tm, D), lambda i, j: (cur(i), 0)),
                  pl.BlockSpec((None, 1, mod3.shape[2]), lambda i, j: ((prv(i) * tm) // seq, 0, 0)),
                  pl.BlockSpec((1, D), lambda i, j: (0, 0)),
                  pl.BlockSpec((D, tf), lambda i, j: (0, ff(i, j))),
                  pl.BlockSpec((tf, D), lambda i, j: (ff(i, j), 0))],
        out_specs=pl.BlockSpec((tm, D), lambda i, j: (prv(i), 0)),
        scratch_shapes=[pltpu.VMEM((2, tm, D), F32)],
        compiler_params=_cparams(("arbitrary", "arbitrary")),
        name="mlp",
    )(x2, hf, mod3, g3, w1, w2)


def _pad_rows(w, offset, total):
    return jnp.zeros((total, w.shape[1]), w.dtype).at[offset:offset + w.shape[0]].set(w)


def kernel(x, c, w_ada, b_ada, norm_g, w_in, mu_shift, w0, w2, a0, a2, g2, k_k, k_a,
           r_k, ln_x_w, ln_x_b, w_up_rwkv, w_up_sb, w_out, w_mlp_in, w_mlp_out):
    B, S, D = x.shape
    depth = w_in.shape[0]
    C = w0.shape[1]
    W = w_up_sb.shape[1]
    rwkv_cols = 3 * C + DECAY_LORA + ICLR_LORA + GATE_LORA
    sb_cols = 3 * W
    n_lora = DECAY_LORA + ICLR_LORA + GATE_LORA
    assert C == 1024 and W == 1024 and D == 2 * C and S % (2 * CHUNK) == 0

    hi = lax.broadcasted_iota(jnp.int32, (LANES, LANES), 0) // HEAD_DIM
    hj = lax.broadcasted_iota(jnp.int32, (LANES, LANES), 1) // HEAD_DIM
    bd = (hi == hj).astype(BF16)

    x2 = x.reshape(B * S, D)
    for l in range(depth):
        mod = _ada(c, w_ada[l], b_ada[l])
        mod3 = mod.reshape(B, 1, mod.shape[1])

        g_off = rwkv_cols + sb_cols
        n_main_cols = 2 * D + 3 * C + LORA_PAD
        w_perm = _regroup_rows(
            w_in[l].T,
            [(g_off, 2 * D), (0, 3 * C), (3 * C, LORA_PAD), (rwkv_cols, sb_cols)],
            zero_lo=2 * D + 3 * C + n_lora, zero_hi=n_main_cols)
        P, qkv = _inproj(x2, mod3, norm_g[l, 0].reshape(1, D), w_perm, n_main_cols, S)

        mu = mu_shift[l]
        consts = dict(
            col_k=2 * D // C + 1, col_l=(2 * D + 3 * C) // LORA_PAD,
            mu_k=mu[None, C:2 * C],
            mu_l=jnp.pad(mu[None, 3 * C:], ((0, 0), (0, LORA_PAD - n_lora))),
            w0=w0[l][None], a0=a0[l][None], k_k=k_k[l][None], k_a=k_a[l][None],
            w2p=_pad_rows(w2[l], 0, LORA_PAD).astype(BF16),
            a2p=_pad_rows(a2[l], DECAY_LORA, LORA_PAD).astype(BF16),
            g2p=_pad_rows(g2[l], DECAY_LORA + ICLR_LORA, LORA_PAD).astype(BF16),
            bd=bd)
        prep = _prep(P, S, consts)
        ya = _scan(P, 2 * D, 2 * D + 2 * C, mu[None, :C], mu[None, 2 * C:3 * C], prep,
                   r_k[l].reshape(1, C), ln_x_w[l][None], ln_x_b[l][None], B, S)

        yb = _stick_breaking(qkv.reshape(B, S, 3 * W), W)

        x2, hf = _merge(ya, yb, P, x2, mod3, norm_g[l, 1].reshape(1, D),
                        norm_g[l, 2].reshape(1, D), w_up_rwkv[l].astype(BF16),
                        w_up_sb[l].astype(BF16), w_out[l].astype(BF16), S)
        x2 = _mlp(x2, hf, mod3, norm_g[l, 3].reshape(1, D),
                  w_mlp_in[l].astype(BF16), w_mlp_out[l].astype(BF16), S)
    return x2.reshape(B, S, D)
```

```python
import functools
import math

import jax
import jax.numpy as jnp
from jax import lax
from jax.experimental import pallas as pl
from jax.experimental.pallas import tpu as pltpu

F32 = jnp.float32
BF16 = jnp.bfloat16

HEAD_DIM = 64
LANES = 128
DECAY_LORA = 64
ICLR_LORA = 64
GATE_LORA = 160
LORA_PAD = 512
NORM_EPS = 1e-6
GN_EPS = 64e-5
CHUNK = 128
SB_BLOCK = 128
SB_BLOCKS_PER_STEP = 3
SB_UNDERFLOW = 104.0
VMEM_LIMIT = 56 * 1024 * 1024


def _cparams(sem, vmem=VMEM_LIMIT):
    return pltpu.CompilerParams(dimension_semantics=sem, vmem_limit_bytes=vmem)


def _dot(a, b):
    return jnp.dot(a, b, preferred_element_type=F32)


def _dot_nt(a, b):
    return lax.dot_general(a, b, (((1,), (1,)), ((), ())), preferred_element_type=F32)


def _dot_tn(a, b):
    return lax.dot_general(a, b, (((0,), (0,)), ((), ())), preferred_element_type=F32)


def _split2(x):
    hi = x.astype(BF16)
    lo = (x - hi.astype(F32)).astype(BF16)
    return hi, lo


def _dot_exact01(x, m01):
    n = x.shape[0]
    res = _dot(jnp.concatenate(_split2(x), axis=0), m01)
    return res[:n] + res[n:]


def _exact01_dot(m01, x):
    n = x.shape[1]
    res = _dot(m01, jnp.concatenate(_split2(x), axis=1))
    return res[:, :n] + res[:, n:]


def _head_sum(x, bd):
    return _dot_exact01(x, bd)


def _rms(x, g):
    return x * lax.rsqrt(jnp.mean(x * x, axis=-1, keepdims=True) + NORM_EPS) * g


def _ada_kernel(c_ref, w_ref, b_ref, o_ref):
    c = c_ref[...]
    s = c * jax.nn.sigmoid(c)
    o_ref[...] = _dot(s.astype(BF16), w_ref[...].astype(BF16)) + b_ref[...]


def _ada(c, w_ada, b_ada, tn=3072):
    B, D = c.shape
    N = w_ada.shape[1]
    rows = 8
    c_pad = jnp.zeros((rows, D), F32).at[:B].set(c)
    out = pl.pallas_call(
        _ada_kernel,
        out_shape=jax.ShapeDtypeStruct((rows, N), F32),
        grid=(N // tn,),
        in_specs=[pl.BlockSpec((rows, D), lambda j: (0, 0)),
                  pl.BlockSpec((D, tn), lambda j: (0, j)),
                  pl.BlockSpec((1, tn), lambda j: (0, j))],
        out_specs=pl.BlockSpec((rows, tn), lambda j: (0, j)),
        compiler_params=_cparams(("parallel",)),
        name="ada",
    )(c_pad, w_ada, b_ada.reshape(1, N))
    return out[:B]


def _regroup_kernel(offs_ref, w_ref, o_ref, *, zero_lo, zero_hi, rows):
    del offs_ref
    out_row = pl.program_id(0) * rows + lax.broadcasted_iota(jnp.int32, (rows, 1), 0)
    pad = jnp.logical_and(out_row >= zero_lo, out_row < zero_hi)
    o_ref[...] = jnp.where(pad, 0.0, w_ref[...]).astype(BF16)


def _regroup_rows(wt, groups, zero_lo, zero_hi, rows=512):
    D = wt.shape[1]
    sub = 8
    assert all(s % sub == 0 and n % rows == 0 for s, n in groups)
    offs = jnp.asarray([(s + r) // sub for (s, n) in groups for r in range(0, n, rows)], jnp.int32)
    n_out = sum(n for _, n in groups)
    return pl.pallas_call(
        functools.partial(_regroup_kernel, zero_lo=zero_lo, zero_hi=zero_hi, rows=rows),
        out_shape=jax.ShapeDtypeStruct((n_out, D), BF16),
        grid_spec=pltpu.PrefetchScalarGridSpec(
            num_scalar_prefetch=1, grid=(n_out // rows,),
            in_specs=[pl.BlockSpec((pl.Element(rows), pl.Element(D)),
                                   lambda t, offs: (offs[t] * sub, 0))],
            out_specs=pl.BlockSpec((rows, D), lambda t, offs: (t, 0))),
        compiler_params=_cparams(("parallel",)),
        name="regroup_w_in",
    )(offs, wt)


def _inproj_kernel(x0_ref, mod0_ref, xn_ref, modn_ref, g_ref, w_ref, o_ref, sb_ref, h_ref,
                   *, d, n_main, chunk):
    i = pl.program_id(0)
    j = pl.program_id(1)
    slot = i % 2
    tm = xn_ref.shape[0]

    def modulated(x, mod_ref):
        shift = mod_ref[:, 0:d]
        scale = mod_ref[:, d:2 * d]
        return (_rms(x, g_ref[...]) * (1.0 + scale) + shift).astype(BF16)

    @pl.when(jnp.logical_and(i == 0, j == 0))
    def _():
        h_ref[0] = modulated(x0_ref[...], mod0_ref)

    def prepare_next_rows():
        r0 = pl.multiple_of(jnp.minimum(j * chunk, tm - chunk), 16)
        h_ref[1 - slot, pl.ds(r0, chunk), :] = modulated(xn_ref[pl.ds(r0, chunk), :], modn_ref)

    @pl.when(j < n_main)
    def _():
        o_ref[...] = _dot_nt(h_ref[slot], w_ref[...])
        prepare_next_rows()

    @pl.when(j >= n_main)
    def _():
        sb_ref[...] = _dot_nt(h_ref[slot], w_ref[...]).astype(BF16)
        prepare_next_rows()


def _inproj(x2, mod3, g0, w_t, n_main_cols, seq, tm=1024, tn=768):
    T, D = x2.shape
    N = w_t.shape[0]
    n_main = n_main_cols // tn
    n_i, n_j = T // tm, N // tn
    chunk = 16 * pl.cdiv(pl.cdiv(tm, n_j), 16)

    def nxt(i):
        return jnp.minimum(i + 1, n_i - 1)

    return pl.pallas_call(
        functools.partial(_inproj_kernel, d=D, n_main=n_main, chunk=chunk),
        out_shape=[jax.ShapeDtypeStruct((T, n_main_cols), F32),
                   jax.ShapeDtypeStruct((T, N - n_main_cols), BF16)],
        grid=(n_i, n_j),
        in_specs=[pl.BlockSpec((tm, D), lambda i, j: (0, 0), pipeline_mode=pl.Buffered(1)),
                  pl.BlockSpec((None, 1, mod3.shape[2]), lambda i, j: (0, 0, 0)),
                  pl.BlockSpec((tm, D), lambda i, j: (nxt(i), 0)),
                  pl.BlockSpec((None, 1, mod3.shape[2]), lambda i, j: ((nxt(i) * tm) // seq, 0, 0)),
                  pl.BlockSpec((1, D), lambda i, j: (0, 0)),
                  pl.BlockSpec((tn, D), lambda i, j: (j, 0))],
        out_specs=[pl.BlockSpec((tm, tn), lambda i, j: (i, jnp.minimum(j, n_main - 1))),
                   pl.BlockSpec((tm, tn), lambda i, j: (i, jnp.maximum(j - n_main, 0)))],
        scratch_shapes=[pltpu.VMEM((2, tm, D), BF16)],
        compiler_params=_cparams(("arbitrary", "arbitrary")),
        name="inproj",
    )(x2, mod3, x2, mod3, g0, w_t)


def _token_shift_mix(cur_ref, prev_ref, mu_ref, first):
    cur = cur_ref[...]
    row = lax.broadcasted_iota(jnp.int32, (cur.shape[0], 1), 0)
    prev_row = jnp.where(first, 0.0, prev_ref[7:8, :])
    shifted = jnp.where(row == 0, prev_row, pltpu.roll(cur, 1, 0))
    return cur + (shifted - cur) * mu_ref[...]


def _prep_kernel(k_ref, l_ref, kp_ref, lp_ref, muk_ref, mul_ref, w0_ref, a0_ref, kk_ref, ka_ref,
                 w2_ref, a2_ref, g2_ref, bd_ref,
                 ko_ref, kko_ref, bo_ref, lwo_ref, go_ref, *, tm, seq):
    first = (pl.program_id(0) * tm) % seq == 0
    k = _token_shift_mix(k_ref, kp_ref, muk_ref, first)
    lo = _token_shift_mix(l_ref, lp_ref, mul_ref, first)

    dw = _dot(jnp.tanh(lo).astype(BF16), w2_ref[...])
    da = _dot(lo.astype(BF16), a2_ref[...])
    g = _dot(jax.nn.sigmoid(lo).astype(BF16), g2_ref[...])

    t = -(w0_ref[...] + dw)
    softplus_t = jnp.maximum(t, 0.0) + jnp.log(1.0 + jnp.exp(-jnp.abs(t)))
    w_log = -softplus_t - 0.5
    lw = -jnp.exp(w_log)
    a = jax.nn.sigmoid(a0_ref[...] + da)
    kx = k * kk_ref[...]
    bd = bd_ref[...]
    n_blk = kx.shape[1] // LANES
    sq = kx * kx
    ssum = jnp.concatenate(
        [_head_sum(sq[:, c * LANES:(c + 1) * LANES], bd) for c in range(n_blk)], axis=1)
    kk = kx * lax.rsqrt(jnp.maximum(ssum, 1e-24))
    k2 = k * (1.0 + (a - 1.0) * ka_ref[...])

    ko_ref[...] = k2
    kko_ref[...] = kk
    bo_ref[...] = kk * a
    lwo_ref[...] = lw
    go_ref[...] = g


def _prep(P, seq, consts, tm=512):
    T = P.shape[0]
    C = consts["w0"].shape[1]
    col_k, col_l = consts["col_k"], consts["col_l"]

    def cur(col, w):
        return pl.BlockSpec((tm, w), lambda i: (i, col))

    def prev(col, w):
        return pl.BlockSpec((8, w), lambda i: (jnp.maximum(i * (tm // 8) - 1, 0), col))

    def const(shape):
        return pl.BlockSpec(shape, lambda i: (0, 0))

    out_sd = jax.ShapeDtypeStruct((T, C), F32)
    outs = pl.pallas_call(
        functools.partial(_prep_kernel, tm=tm, seq=seq),
        out_shape=[out_sd] * 5,
        grid=(T // tm,),
        in_specs=[cur(col_k, C), cur(col_l, LORA_PAD), prev(col_k, C), prev(col_l, LORA_PAD),
                  const((1, C)), const((1, LORA_PAD)),
                  const((1, C)), const((1, C)), const((1, C)), const((1, C)),
                  const((LORA_PAD, C)), const((LORA_PAD, C)), const((LORA_PAD, C)),
                  const((LANES, LANES))],
        out_specs=[pl.BlockSpec((tm, C), lambda i: (i, 0))] * 5,
        compiler_params=_cparams(("parallel",)),
        name="rwkv_prep",
    )(P, P, P, P, consts["mu_k"], consts["mu_l"],
      consts["w0"], consts["a0"], consts["k_k"], consts["k_a"],
      consts["w2p"], consts["a2p"], consts["g2p"], consts["bd"])
    return outs


def _scan_kernel(r_ref, v_ref, rp_ref, vp_ref, mur_ref, muv_ref, k_ref, kk_ref, b_ref, lw_ref,
                 g_ref, rk_ref, lnw_ref, lnb_ref, o_ref, h_ref, *, nc, npair):
    L = CHUNK
    half = L // 2

    @pl.when(pl.program_id(2) == 0)
    def _():
        h_ref[...] = jnp.zeros_like(h_ref)

    ri = lax.broadcasted_iota(jnp.int32, (L, L), 0)
    ci = lax.broadcasted_iota(jnp.int32, (L, L), 1)
    lower_incl = ri >= ci
    lower_strict = ri > ci
    eye = ri == ci
    same_head = (ri < HEAD_DIM) == (ci < HEAD_DIM)
    tri01 = lower_incl.astype(BF16)
    eye_f = eye.astype(F32)
    lane = lax.broadcasted_iota(jnp.int32, (1, LANES), 1)
    head0 = lane < HEAD_DIM
    lane2 = lax.broadcasted_iota(jnp.int32, (1, 2 * LANES), 1)
    head0_2 = (lane2 % LANES) < HEAD_DIM
    tri_stack = jnp.concatenate([lower_strict, lower_incl], axis=0)
    chunks = range(nc * npair)
    pairs = [(c, h) for c in chunks for h in range(2)]

    def lanes_of(u):
        return slice((u // nc) * LANES, (u // nc + 1) * LANES)

    def ld(ref, u):
        return ref[pl.ds((u % nc) * L, L), lanes_of(u)]

    first = pl.program_id(2) == 0
    r_tile = _token_shift_mix(r_ref, rp_ref, mur_ref, first)
    v_tile = _token_shift_mix(v_ref, vp_ref, muv_ref, first)
    r = [r_tile[(u % nc) * L:(u % nc + 1) * L, lanes_of(u)] for u in chunks]
    v = [v_tile[(u % nc) * L:(u % nc + 1) * L, lanes_of(u)] for u in chunks]
    k = [ld(k_ref, c) for c in chunks]
    kk = [ld(kk_ref, c) for c in chunks]
    b = [ld(b_ref, c) for c in chunks]
    lw = [ld(lw_ref, c) for c in chunks]

    cum = [_exact01_dot(tri01, lw[c]) for c in chunks]
    lhs_mid, rhs_mid, kd_bf, r_dec, k_end, b_end, p_end, v_bf = [], [], [], [], [], [], [], []
    for c in chunks:
        mid = cum[c][half - 1:half, :]
        end = cum[c][L - 1:L, :]
        r_mid = r[c] * jnp.exp(cum[c] - mid)
        kk_mid = kk[c] * jnp.exp(cum[c] - lw[c] - mid)
        e_back = jnp.exp(mid - cum[c])
        lhs_mid.append(jnp.concatenate([kk_mid, r_mid], axis=0))
        rhs_mid.append(jnp.concatenate([b[c] * e_back, k[c] * e_back], axis=0).astype(BF16))
        kd_bf.append((kk[c] * jnp.exp(cum[c] - lw[c])).astype(BF16))
        r_dec.append(r[c] * jnp.exp(cum[c]))
        e_end = jnp.exp(end - cum[c])
        k_end.append((k[c] * e_end).astype(BF16))
        b_end.append((b[c] * e_end).astype(BF16))
        p_end.append(jnp.exp(end))
        v_bf.append(v[c].astype(BF16))

    aa = {}
    for (c, h) in pairs:
        hm = head0 if h == 0 else jnp.logical_not(head0)
        aa[c, h] = _dot_nt(jnp.where(hm, lhs_mid[c], 0.0).astype(BF16), rhs_mid[c])
    def side(f):
        return jnp.concatenate([f(0), f(1)], axis=1)

    tri2 = jnp.concatenate([tri_stack, tri_stack], axis=1)
    strict2 = jnp.concatenate([lower_strict, lower_strict], axis=1)
    incl2 = jnp.concatenate([lower_incl, lower_incl], axis=1)
    def by_head_rows(x):
        hm = head0 if x.shape[1] == LANES else head0_2
        zero = jnp.zeros_like(x)
        return jnp.concatenate([jnp.where(hm, x, zero), jnp.where(hm, zero, x)], axis=0)

    a_kb = [jnp.where(strict2, side(lambda h: aa[c, h][:L, :L]), 0.0) for c in chunks]
    a_rb = [jnp.where(incl2, side(lambda h: aa[c, h][L:, :L]), 0.0).astype(BF16) for c in chunks]
    a_xk = [jnp.where(tri2, side(lambda h: aa[c, h][:, L:]), 0.0).astype(BF16) for c in chunks]
    avr = [_dot(a_xk[c], by_head_rows(v_bf[c])) for c in chunks]

    hl = L // 2
    blk_r = lax.broadcasted_iota(jnp.int32, (2 * L, 2 * L), 0) // hl
    blk_c = lax.broadcasted_iota(jnp.int32, (2 * L, 2 * L), 1) // hl
    diag4 = blk_r == blk_c
    below4 = jnp.logical_and(blk_r == blk_c + 1, blk_c % 2 == 0)
    eye4 = (lax.broadcasted_iota(jnp.int32, (hl, 2 * L), 0)
            == lax.broadcasted_iota(jnp.int32, (hl, 2 * L), 1) % hl).astype(F32)

    def tile4(x, keep):
        return jnp.where(keep, jnp.concatenate([x, x, x, x], axis=0), jnp.zeros((), x.dtype))

    r_loc = lax.broadcasted_iota(jnp.int32, (hl, 2 * L), 0)
    c_loc = lax.broadcasted_iota(jnp.int32, (hl, 2 * L), 1) % hl
    t, a21, pd = [], [], []
    for c in chunks:
        pd.append(jnp.where(head0_2, a_kb[c][:hl], a_kb[c][hl:]))
        a21.append(jnp.where(head0_2, a_kb[c][hl:], 0.0).astype(BF16))
        t.append(eye4)
    m = 1
    while m < hl:
        siblings = jnp.logical_and((r_loc // m) % 2 == 1, c_loc // m == r_loc // m - 1)
        for c in chunks:
            l_m = jnp.where(siblings, pd[c], 0.0).astype(BF16)
            x = _dot(l_m, tile4(t[c].astype(BF16), diag4))
            t[c] = t[c] - _dot(t[c].astype(BF16), tile4(x.astype(BF16), diag4))
        m *= 2
    for c in chunks:
        td_bf = t[c].astype(BF16)
        y = _dot(a21[c], tile4(jnp.where(head0_2, td_bf, jnp.zeros((), BF16)), diag4))
        z = _dot(td_bf, tile4(y.astype(BF16), below4))
        t[c] = jnp.concatenate([jnp.where(head0_2, t[c], 0.0), jnp.where(head0_2, -z, t[c])], axis=0)

    rq, y0, gmat, cmat = [], [], [], []
    for c in chunks:
        av = avr[c][:L]
        arkv = avr[c][L:]
        x = jnp.concatenate([kd_bf[c], av.astype(BF16)], axis=1)
        w_bf = _dot(t[c].astype(BF16), by_head_rows(x)).astype(BF16)
        aw = _dot(a_rb[c], by_head_rows(w_bf))
        bw = _dot_tn(b_end[c], w_bf)
        kv = _dot_tn(k_end[c], v_bf[c])
        rq.append(r_dec[c] - aw[:, :LANES])
        y0.append(arkv - aw[:, LANES:])
        gmat.append(eye_f * p_end[c] - jnp.where(same_head, bw[:, :LANES], 0.0))
        cmat.append(jnp.where(same_head, kv - bw[:, LANES:], 0.0))

    bd2 = ((lax.broadcasted_iota(jnp.int32, (2 * LANES, 2 * LANES), 0) // HEAD_DIM)
           == (lax.broadcasted_iota(jnp.int32, (2 * LANES, 2 * LANES), 1) // HEAD_DIM)).astype(BF16)

    def head_sum_all(x):
        n = x.shape[0] // 2
        s2 = _dot_exact01(jnp.concatenate([x[:n], x[n:]], axis=1), bd2)
        return jnp.concatenate([s2[:, :LANES], s2[:, LANES:]], axis=0)

    rkr = head_sum_all(jnp.concatenate(
        [r[u] * k[u] * rk_ref[:, lanes_of(u)] for u in chunks], axis=0))
    bonus = [rkr[u * L:(u + 1) * L] * v[u] for u in chunks]

    inv_n = 1.0 / HEAD_DIM
    hstate = [h_ref[q] for q in range(npair)]
    ys = [None] * len(chunks)
    for c in range(nc):
        for q in range(npair):
            u = q * nc + c
            res = _dot(jnp.concatenate([rq[u], gmat[u]], axis=0).astype(BF16),
                       hstate[q].astype(BF16))
            ys[u] = res[:L] + y0[u]
            hstate[q] = res[L:] + cmat[u]
    for q in range(npair):
        h_ref[q] = hstate[q]
    y = jnp.concatenate(ys, axis=0)
    mean = head_sum_all(y) * inv_n
    yc = y - mean
    var = head_sum_all(yc * yc) * inv_n
    yn = yc * lax.rsqrt(var + GN_EPS)
    for u in chunks:
        rows = pl.ds((u % nc) * L, L)
        o_ref[rows, lanes_of(u)] = (
            (yn[u * L:(u + 1) * L] * lnw_ref[:, lanes_of(u)] + lnb_ref[:, lanes_of(u)] + bonus[u])
            * g_ref[rows, lanes_of(u)]).astype(BF16)


def _scan(P, col_r, col_v, mu_r, mu_v, arrs, rk, lnw, lnb, batch, seq, nc=4, npair=4):
    k, kk, b, lw, g = [a.reshape(batch, seq, a.shape[1]) for a in arrs]
    P3 = P.reshape(batch, seq, P.shape[1])
    C = k.shape[2]
    width = LANES * npair
    rows = CHUNK * nc
    blk = pl.BlockSpec((None, rows, width), lambda bi, p, c: (bi, c, p))
    par = pl.BlockSpec((1, width), lambda bi, p, c: (0, p))

    def raw(col):
        return pl.BlockSpec((None, rows, width), lambda bi, p, c: (bi, c, col // width + p))

    def raw_prev(col):
        return pl.BlockSpec((None, 8, width),
                            lambda bi, p, c: (bi, jnp.maximum(c * (rows // 8) - 1, 0),
                                              col // width + p))

    out = pl.pallas_call(
        functools.partial(_scan_kernel, nc=nc, npair=npair),
        out_shape=jax.ShapeDtypeStruct((batch, seq, C), BF16),
        grid=(batch, C // width, seq // rows),
        in_specs=[raw(col_r), raw(col_v), raw_prev(col_r), raw_prev(col_v), par, par]
                 + [blk] * 5 + [par, par, par],
        out_specs=blk,
        scratch_shapes=[pltpu.VMEM((npair, LANES, LANES), F32)],
        compiler_params=_cparams(("parallel", "parallel", "arbitrary")),
        name="rwkv_scan",
    )(P3, P3, P3, P3, mu_r, mu_v, k, kk, b, lw, g, rk, lnw, lnb)
    return out.reshape(batch * seq, C)


def _sb_kernel(q_ref, k_ref, v_ref, o_ref, *, npair, nq):
    tq = tk = SB_BLOCK
    nu = SB_BLOCKS_PER_STEP
    i = pl.program_id(2)
    lane = lax.broadcasted_iota(jnp.int32, (1, LANES), 1)
    head0 = lane < HEAD_DIM
    scale = jnp.asarray(1.0 / math.sqrt(HEAD_DIM), BF16)
    streams = [(s, hp) for s in range(nq) for hp in range(npair)]
    ns = len(streams)
    q = [q_ref[s * tq:(s + 1) * tq, hp * LANES:(hp + 1) * LANES] * scale for (s, hp) in streams]
    r2 = lax.broadcasted_iota(jnp.int32, (2 * tk, 2 * tk), 0)
    c2 = lax.broadcasted_iota(jnp.int32, (2 * tk, 2 * tk), 1)
    after01 = jnp.logical_and(r2 > c2, (r2 < tk) == (c2 < tk)).astype(BF16)
    key_minus_query = (lax.broadcasted_iota(jnp.int32, (tq, 2 * tk), 1) % tk
                       - lax.broadcasted_iota(jnp.int32, (tq, 2 * tk), 0))
    units = [(p, u) for p in range(ns) for u in range(nu)]

    def by_head_rows(x):
        zero = jnp.zeros_like(x)
        return jnp.concatenate([jnp.where(head0, x, zero), jnp.where(head0, zero, x)], axis=0)

    def both(c0, c1):
        return jnp.concatenate([jnp.broadcast_to(c0, (tq, tk)), jnp.broadcast_to(c1, (tq, tk))],
                               axis=1)

    def body(carry):
        jt, _, offs, accs = carry
        z, sp, log_keep, mask, vb = {}, {}, {}, {}, {}
        for (p, u) in units:
            s, hp = streams[p]
            jb = jt - (nq - 1 - s) - u
            start = pl.multiple_of(jnp.maximum(jb, 0) * tk, tk)
            lanes = slice(hp * LANES, (hp + 1) * LANES)
            kb = by_head_rows(k_ref[pl.ds(start, tk), lanes])
            vb[p, u] = by_head_rows(v_ref[pl.ds(start, tk), lanes])
            zu = _dot_nt(q[p], kb)
            if u > 0:
                zu = zu + jnp.where(jb >= 0, 0.0, -1e30)
            spu = jnp.maximum(zu, 0.0) + jnp.log(1.0 + jnp.exp(-jnp.abs(zu)))
            z[p, u], sp[p, u] = zu, spu
            if u == 0:
                q_start = (i * nq + s) * tq
                mask[p, u] = key_minus_query < jnp.where(jb >= 0, q_start - jb * tk, -tq)
                log_keep[p, u] = jnp.where(mask[p, u], -spu, 0.0)
            else:
                log_keep[p, u] = -spu
        parts = []
        for key in units:
            parts.extend(_split2(log_keep[key]))
        loc = _dot(jnp.concatenate(parts, axis=0), after01)
        offs, accs = list(offs), list(accs)
        for n, (p, u) in enumerate(units):
            local = loc[2 * n * tq:(2 * n + 1) * tq] + loc[(2 * n + 1) * tq:(2 * n + 2) * tq]
            later = local + both(offs[2 * p], offs[2 * p + 1])
            att = jnp.exp(z[p, u] - sp[p, u] + later)
            if u == 0:
                att = jnp.where(mask[p, u], att, 0.0)
            accs[p] = accs[p] + _dot(att.astype(BF16), vb[p, u])
            offs[2 * p] = offs[2 * p] + (local[:, 0:1] + log_keep[p, u][:, 0:1])
            offs[2 * p + 1] = offs[2 * p + 1] + (local[:, tk:tk + 1] + log_keep[p, u][:, tk:tk + 1])
        top = offs[0]
        for o in offs[1:]:
            top = jnp.maximum(top, o)
        alive = (jnp.max(top) > -SB_UNDERFLOW).astype(jnp.int32)
        return jt - nu, alive, tuple(offs), tuple(accs)

    def cond(carry):
        jt, alive = carry[0], carry[1]
        return jnp.logical_and(jt >= 0, alive > 0)

    init = (i * nq + nq - 1, jnp.int32(1),
            tuple(jnp.zeros((tq, 1), F32) for _ in range(2 * ns)),
            tuple(jnp.zeros((tq, LANES), F32) for _ in range(ns)))
    accs = lax.while_loop(cond, body, body(init))[3]
    for p, (s, hp) in enumerate(streams):
        o_ref[s * tq:(s + 1) * tq, hp * LANES:(hp + 1) * LANES] = accs[p].astype(BF16)


def _stick_breaking(qkv, width, npair=2, nq=2):
    B, S, _ = qkv.shape
    w = LANES * npair
    nblk = width // w
    rows = SB_BLOCK * nq
    out = pl.pallas_call(
        functools.partial(_sb_kernel, npair=npair, nq=nq),
        out_shape=jax.ShapeDtypeStruct((B, S, width), BF16),
        grid=(B, nblk, S // rows),
        in_specs=[pl.BlockSpec((None, rows, w), lambda b, p, i: (b, i, p)),
                  pl.BlockSpec((None, S, w), lambda b, p, i: (b, 0, nblk + p)),
                  pl.BlockSpec((None, S, w), lambda b, p, i: (b, 0, 2 * nblk + p))],
        out_specs=pl.BlockSpec((None, rows, w), lambda b, p, i: (b, i, p)),
        compiler_params=_cparams(("parallel", "parallel", "arbitrary")),
        name="stick_breaking",
    )(qkv, qkv, qkv)
    return out.reshape(B * S, width)


def _merge_kernel(ya_ref, yb_ref, ga_ref, gb_ref, x_ref, mod_ref, g1_ref,
                  g2_ref, wua_ref, wub_ref, wo_ref, o_ref, hf_ref, *, d):
    ua = _dot(ya_ref[...], wua_ref[...])
    ub = _dot(yb_ref[...], wub_ref[...])
    merged = jax.nn.sigmoid(ga_ref[...]) * ua + jax.nn.sigmoid(gb_ref[...]) * ub
    mix = _dot(merged.astype(BF16), wo_ref[...])
    gate_m = mod_ref[:, 2 * d:3 * d]
    x1 = x_ref[...] + gate_m * _rms(mix, g1_ref[...])
    o_ref[...] = x1
    shift_f = mod_ref[:, 3 * d:4 * d]
    scale_f = mod_ref[:, 4 * d:5 * d]
    hf_ref[...] = (_rms(x1, g2_ref[...]) * (1.0 + scale_f) + shift_f).astype(BF16)


def _merge(ya, yb, P, x2, mod3, g1, g2, wua, wub, wo, seq, tm=256):
    T, D = x2.shape
    C = ya.shape[1]

    def const(shape):
        return pl.BlockSpec(shape, lambda i: (0, 0), pipeline_mode=pl.Buffered(1))

    return pl.pallas_call(
        functools.partial(_merge_kernel, d=D),
        out_shape=[jax.ShapeDtypeStruct((T, D), F32), jax.ShapeDtypeStruct((T, D), BF16)],
        grid=(T // tm,),
        in_specs=[pl.BlockSpec((tm, C), lambda i: (i, 0)),
                  pl.BlockSpec((tm, C), lambda i: (i, 0)),
                  pl.BlockSpec((tm, D), lambda i: (i, 0)),
                  pl.BlockSpec((tm, D), lambda i: (i, 1)),
                  pl.BlockSpec((tm, D), lambda i: (i, 0)),
                  pl.BlockSpec((None, 1, mod3.shape[2]), lambda i: ((i * tm) // seq, 0, 0)),
                  pl.BlockSpec((1, D), lambda i: (0, 0)),
                  pl.BlockSpec((1, D), lambda i: (0, 0)),
                  const((C, D)), const((C, D)), const((D, D))],
        out_specs=[pl.BlockSpec((tm, D), lambda i: (i, 0)), pl.BlockSpec((tm, D), lambda i: (i, 0))],
        compiler_params=_cparams(("parallel",)),
        name="merge_out",
    )(ya, yb, P, P, x2, mod3, g1, g2, wua, wub, wo)


def _mlp_kernel(xp_ref, hf_ref, modp_ref, g3_ref, w1_ref, w2_ref, o_ref, acc_ref, *, d, rc):
    i = pl.program_id(0)
    j = pl.program_id(1)
    n = pl.num_programs(0) - 1
    slot = i % 2

    @pl.when(jnp.logical_and(i == 0, j == 0))
    def _():
        acc_ref[1] = jnp.zeros(acc_ref.shape[1:], F32)

    def finish_prev_rows():
        rows = pl.ds(pl.multiple_of(j * rc, rc), rc)
        gate_f = modp_ref[:, 5 * d:6 * d]
        o_ref[rows, :] = xp_ref[rows, :] + gate_f * _rms(acc_ref[1 - slot, rows, :], g3_ref[...])

    def partial_out():
        u = jnp.maximum(_dot(hf_ref[...], w1_ref[...]), 0.0)
        return _dot((u * u).astype(BF16), w2_ref[...])

    @pl.when(jnp.logical_and(i < n, j == 0))
    def _():
        finish_prev_rows()
        acc_ref[slot] = partial_out()

    @pl.when(jnp.logical_and(i < n, j > 0))
    def _():
        finish_prev_rows()
        acc_ref[slot] += partial_out()

    @pl.when(i == n)
    def _():
        finish_prev_rows()


def _mlp(x2, hf, mod3, g3, w1, w2, seq, tm=512, tf=1024):
    T, D = x2.shape
    F = w1.shape[1]
    n, nj = T // tm, F // tf
    assert tm % nj == 0 and (tm // nj) % 8 == 0

    def cur(i):
        return jnp.minimum(i, n - 1)

    def prv(i):
        return jnp.maximum(i - 1, 0)

    def ff(i, j):
        return jnp.where(i < n, j, nj - 1)

    return pl.pallas_call(
        functools.partial(_mlp_kernel, d=D, rc=tm // nj),
        out_shape=jax.ShapeDtypeStruct((T, D), F32),
        grid=(n + 1, nj),
        in_specs=[pl.BlockSpec((tm, D), lambda i, j: (prv(i), 0)),
                  pl.BlockSpec((tm, D), lambda i, j: (cur(i), 0)),
                  pl.BlockSpec((None, 1, mod3.shape[2]), lambda i, j: ((prv(i) * tm) // seq, 0, 0)),
                  pl.BlockSpec((1, D), lambda i, j: (0, 0)),
                  pl.BlockSpec((D, tf), lambda i, j: (0, ff(i, j))),
                  pl.BlockSpec((tf, D), lambda i, j: (ff(i, j), 0))],
        out_specs=pl.BlockSpec((tm, D), lambda i, j: (prv(i), 0)),
        scratch_shapes=[pltpu.VMEM((2, tm, D), F32)],
        compiler_params=_cparams(("arbitrary", "arbitrary")),
        name="mlp",
    )(x2, hf, mod3, g3, w1, w2)


def _pad_rows(w, offset, total):
    return jnp.zeros((total, w.shape[1]), w.dtype).at[offset:offset + w.shape[0]].set(w)


def kernel(x, c, w_ada, b_ada, norm_g, w_in, mu_shift, w0, w2, a0, a2, g2, k_k, k_a,
           r_k, ln_x_w, ln_x_b, w_up_rwkv, w_up_sb, w_out, w_mlp_in, w_mlp_out):
    B, S, D = x.shape
    depth = w_in.shape[0]
    C = w0.shape[1]
    W = w_up_sb.shape[1]
    rwkv_cols = 3 * C + DECAY_LORA + ICLR_LORA + GATE_LORA
    sb_cols = 3 * W
    n_lora = DECAY_LORA + ICLR_LORA + GATE_LORA
    assert C == 1024 and W == 1024 and D == 2 * C and S % (2 * CHUNK) == 0

    hi = lax.broadcasted_iota(jnp.int32, (LANES, LANES), 0) // HEAD_DIM
    hj = lax.broadcasted_iota(jnp.int32, (LANES, LANES), 1) // HEAD_DIM
    bd = (hi == hj).astype(BF16)

    x2 = x.reshape(B * S, D)
    for l in range(depth):
        mod = _ada(c, w_ada[l], b_ada[l])
        mod3 = mod.reshape(B, 1, mod.shape[1])

        g_off = rwkv_cols + sb_cols
        n_main_cols = 2 * D + 3 * C + LORA_PAD
        w_perm = _regroup_rows(
            w_in[l].T,
            [(g_off, 2 * D), (0, 3 * C), (3 * C, LORA_PAD), (rwkv_cols, sb_cols)],
            zero_lo=2 * D + 3 * C + n_lora, zero_hi=n_main_cols)
        P, qkv = _inproj(x2, mod3, norm_g[l, 0].reshape(1, D), w_perm, n_main_cols, S)

        mu = mu_shift[l]
        consts = dict(
            col_k=2 * D // C + 1, col_l=(2 * D + 3 * C) // LORA_PAD,
            mu_k=mu[None, C:2 * C],
            mu_l=jnp.pad(mu[None, 3 * C:], ((0, 0), (0, LORA_PAD - n_lora))),
            w0=w0[l][None], a0=a0[l][None], k_k=k_k[l][None], k_a=k_a[l][None],
            w2p=_pad_rows(w2[l], 0, LORA_PAD).astype(BF16),
            a2p=_pad_rows(a2[l], DECAY_LORA, LORA_PAD).astype(BF16),
            g2p=_pad_rows(g2[l], DECAY_LORA + ICLR_LORA, LORA_PAD).astype(BF16),
            bd=bd)
        prep = _prep(P, S, consts)
        ya = _scan(P, 2 * D, 2 * D + 2 * C, mu[None, :C], mu[None, 2 * C:3 * C], prep,
                   r_k[l].reshape(1, C), ln_x_w[l][None], ln_x_b[l][None], B, S)

        yb = _stick_breaking(qkv.reshape(B, S, 3 * W), W)

        x2, hf = _merge(ya, yb, P, x2, mod3, norm_g[l, 1].reshape(1, D),
                        norm_g[l, 2].reshape(1, D), w_up_rwkv[l].astype(BF16),
                        w_up_sb[l].astype(BF16), w_out[l].astype(BF16), S)
        x2 = _mlp(x2, hf, mod3, norm_g[l, 3].reshape(1, D),
                  w_mlp_in[l].astype(BF16), w_mlp_out[l].astype(BF16), S)
    return x2.reshape(B, S, D)
```

```python
import functools
import math

import jax
import jax.numpy as jnp
from jax import lax
from jax.experimental import pallas as pl
from jax.experimental.pallas import tpu as pltpu

F32 = jnp.float32
BF16 = jnp.bfloat16

HEAD_DIM = 64
LANES = 128
DECAY_LORA = 64
ICLR_LORA = 64
GATE_LORA = 160
LORA_PAD = 512
NORM_EPS = 1e-6
GN_EPS = 64e-5
CHUNK = 128
SB_BLOCK = 128
SB_BLOCKS_PER_STEP = 3
SB_UNDERFLOW = 104.0
VMEM_LIMIT = 56 * 1024 * 1024


def _cparams(sem, vmem=VMEM_LIMIT):
    return pltpu.CompilerParams(dimension_semantics=sem, vmem_limit_bytes=vmem)


def _dot(a, b):
    return jnp.dot(a, b, preferred_element_type=F32)


def _dot_nt(a, b):
    return lax.dot_general(a, b, (((1,), (1,)), ((), ())), preferred_element_type=F32)


def _dot_tn(a, b):
    return lax.dot_general(a, b, (((0,), (0,)), ((), ())), preferred_element_type=F32)


def _split2(x):
    hi = x.astype(BF16)
    lo = (x - hi.astype(F32)).astype(BF16)
    return hi, lo


def _dot_exact01(x, m01):
    n = x.shape[0]
    res = _dot(jnp.concatenate(_split2(x), axis=0), m01)
    return res[:n] + res[n:]


def _exact01_dot(m01, x):
    n = x.shape[1]
    res = _dot(m01, jnp.concatenate(_split2(x), axis=1))
    return res[:, :n] + res[:, n:]


def _head_sum(x, bd):
    return _dot_exact01(x, bd)


def _rms(x, g):
    return x * lax.rsqrt(jnp.mean(x * x, axis=-1, keepdims=True) + NORM_EPS) * g


def _ada_kernel(c_ref, w_ref, b_ref, o_ref):
    c = c_ref[...]
    s = c * jax.nn.sigmoid(c)
    o_ref[...] = _dot(s.astype(BF16), w_ref[...].astype(BF16)) + b_ref[...]


def _ada(c, w_ada, b_ada, tn=3072):
    B, D = c.shape
    N = w_ada.shape[1]
    rows = 8
    c_pad = jnp.zeros((rows, D), F32).at[:B].set(c)
    out = pl.pallas_call(
        _ada_kernel,
        out_shape=jax.ShapeDtypeStruct((rows, N), F32),
        grid=(N // tn,),
        in_specs=[pl.BlockSpec((rows, D), lambda j: (0, 0)),
                  pl.BlockSpec((D, tn), lambda j: (0, j)),
                  pl.BlockSpec((1, tn), lambda j: (0, j))],
        out_specs=pl.BlockSpec((rows, tn), lambda j: (0, j)),
        compiler_params=_cparams(("parallel",)),
        name="ada",
    )(c_pad, w_ada, b_ada.reshape(1, N))
    return out[:B]


def _regroup_kernel(offs_ref, w_ref, o_ref, *, zero_lo, zero_hi, rows):
    del offs_ref
    out_row = pl.program_id(0) * rows + lax.broadcasted_iota(jnp.int32, (rows, 1), 0)
    pad = jnp.logical_and(out_row >= zero_lo, out_row < zero_hi)
    o_ref[...] = jnp.where(pad, 0.0, w_ref[...]).astype(BF16)


def _regroup_rows(wt, groups, zero_lo, zero_hi, rows=512):
    D = wt.shape[1]
    sub = 8
    assert all(s % sub == 0 and n % rows == 0 for s, n in groups)
    offs = jnp.asarray([(s + r) // sub for (s, n) in groups for r in range(0, n, rows)], jnp.int32)
    n_out = sum(n for _, n in groups)
    return pl.pallas_call(
        functools.partial(_regroup_kernel, zero_lo=zero_lo, zero_hi=zero_hi, rows=rows),
        out_shape=jax.ShapeDtypeStruct((n_out, D), BF16),
        grid_spec=pltpu.PrefetchScalarGridSpec(
            num_scalar_prefetch=1, grid=(n_out // rows,),
            in_specs=[pl.BlockSpec((pl.Element(rows), pl.Element(D)),
                                   lambda t, offs: (offs[t] * sub, 0))],
            out_specs=pl.BlockSpec((rows, D), lambda t, offs: (t, 0))),
        compiler_params=_cparams(("parallel",)),
        name="regroup_w_in",
    )(offs, wt)


def _inproj_kernel(x0_ref, mod0_ref, xn_ref, modn_ref, g_ref, w_ref, o_ref, sb_ref, h_ref,
                   *, d, n_main, chunk):
    i = pl.program_id(0)
    j = pl.program_id(1)
    slot = i % 2
    tm = xn_ref.shape[0]

    def modulated(x, mod_ref):
        shift = mod_ref[:, 0:d]
        scale = mod_ref[:, d:2 * d]
        return (_rms(x, g_ref[...]) * (1.0 + scale) + shift).astype(BF16)

    @pl.when(jnp.logical_and(i == 0, j == 0))
    def _():
        h_ref[0] = modulated(x0_ref[...], mod0_ref)

    def prepare_next_rows():
        r0 = pl.multiple_of(jnp.minimum(j * chunk, tm - chunk), 16)
        h_ref[1 - slot, pl.ds(r0, chunk), :] = modulated(xn_ref[pl.ds(r0, chunk), :], modn_ref)

    @pl.when(j < n_main)
    def _():
        o_ref[...] = _dot_nt(h_ref[slot], w_ref[...])
        prepare_next_rows()

    @pl.when(j >= n_main)
    def _():
        sb_ref[...] = _dot_nt(h_ref[slot], w_ref[...]).astype(BF16)
        prepare_next_rows()


def _inproj(x2, mod3, g0, w_t, n_main_cols, seq, tm=1024, tn=768):
    T, D = x2.shape
    N = w_t.shape[0]
    n_main = n_main_cols // tn
    n_i, n_j = T // tm, N // tn
    chunk = 16 * pl.cdiv(pl.cdiv(tm, n_j), 16)

    def nxt(i):
        return jnp.minimum(i + 1, n_i - 1)

    return pl.pallas_call(
        functools.partial(_inproj_kernel, d=D, n_main=n_main, chunk=chunk),
        out_shape=[jax.ShapeDtypeStruct((T, n_main_cols), F32),
                   jax.ShapeDtypeStruct((T, N - n_main_cols), BF16)],
        grid=(n_i, n_j),
        in_specs=[pl.BlockSpec((tm, D), lambda i, j: (0, 0), pipeline_mode=pl.Buffered(1)),
                  pl.BlockSpec((None, 1, mod3.shape[2]), lambda i, j: (0, 0, 0)),
                  pl.BlockSpec((tm, D), lambda i, j: (nxt(i), 0)),
                  pl.BlockSpec((None, 1, mod3.shape[2]), lambda i, j: ((nxt(i) * tm) // seq, 0, 0)),
                  pl.BlockSpec((1, D), lambda i, j: (0, 0)),
                  pl.BlockSpec((tn, D), lambda i, j: (j, 0))],
        out_specs=[pl.BlockSpec((tm, tn), lambda i, j: (i, jnp.minimum(j, n_main - 1))),
                   pl.BlockSpec((tm, tn), lambda i, j: (i, jnp.maximum(j - n_main, 0)))],
        scratch_shapes=[pltpu.VMEM((2, tm, D), BF16)],
        compiler_params=_cparams(("arbitrary", "arbitrary")),
        name="inproj",
    )(x2, mod3, x2, mod3, g0, w_t)


def _token_shift_mix(cur_ref, prev_ref, mu_ref, first):
    cur = cur_ref[...]
    row = lax.broadcasted_iota(jnp.int32, (cur.shape[0], 1), 0)
    prev_row = jnp.where(first, 0.0, prev_ref[7:8, :])
    shifted = jnp.where(row == 0, prev_row, pltpu.roll(cur, 1, 0))
    return cur + (shifted - cur) * mu_ref[...]


def _prep_kernel(k_ref, l_ref, kp_ref, lp_ref, muk_ref, mul_ref, w0_ref, a0_ref, kk_ref, ka_ref,
                 w2_ref, a2_ref, g2_ref, bd_ref,
                 ko_ref, kko_ref, bo_ref, lwo_ref, go_ref, *, tm, seq):
    first = (pl.program_id(0) * tm) % seq == 0
    k = _token_shift_mix(k_ref, kp_ref, muk_ref, first)
    lo = _token_shift_mix(l_ref, lp_ref, mul_ref, first)

    dw = _dot(jnp.tanh(lo).astype(BF16), w2_ref[...])
    da = _dot(lo.astype(BF16), a2_ref[...])
    g = _dot(jax.nn.sigmoid(lo).astype(BF16), g2_ref[...])

    t = -(w0_ref[...] + dw)
    softplus_t = jnp.maximum(t, 0.0) + jnp.log(1.0 + jnp.exp(-jnp.abs(t)))
    w_log = -softplus_t - 0.5
    lw = -jnp.exp(w_log)
    a = jax.nn.sigmoid(a0_ref[...] + da)
    kx = k * kk_ref[...]
    bd = bd_ref[...]
    n_blk = kx.shape[1] // LANES
    sq = kx * kx
    ssum = jnp.concatenate(
        [_head_sum(sq[:, c * LANES:(c + 1) * LANES], bd) for c in range(n_blk)], axis=1)
    kk = kx * lax.rsqrt(jnp.maximum(ssum, 1e-24))
    k2 = k * (1.0 + (a - 1.0) * ka_ref[...])

    ko_ref[...] = k2
    kko_ref[...] = kk
    bo_ref[...] = kk * a
    lwo_ref[...] = lw
    go_ref[...] = g


def _prep(P, seq, consts, tm=512):
    T = P.shape[0]
    C = consts["w0"].shape[1]
    col_k, col_l = consts["col_k"], consts["col_l"]

    def cur(col, w):
        return pl.BlockSpec((tm, w), lambda i: (i, col))

    def prev(col, w):
        return pl.BlockSpec((8, w), lambda i: (jnp.maximum(i * (tm // 8) - 1, 0), col))

    def const(shape):
        return pl.BlockSpec(shape, lambda i: (0, 0))

    out_sd = jax.ShapeDtypeStruct((T, C), F32)
    outs = pl.pallas_call(
        functools.partial(_prep_kernel, tm=tm, seq=seq),
        out_shape=[out_sd] * 5,
        grid=(T // tm,),
        in_specs=[cur(col_k, C), cur(col_l, LORA_PAD), prev(col_k, C), prev(col_l, LORA_PAD),
                  const((1, C)), const((1, LORA_PAD)),
                  const((1, C)), const((1, C)), const((1, C)), const((1, C)),
                  const((LORA_PAD, C)), const((LORA_PAD, C)), const((LORA_PAD, C)),
                  const((LANES, LANES))],
        out_specs=[pl.BlockSpec((tm, C), lambda i: (i, 0))] * 5,
        compiler_params=_cparams(("parallel",)),
        name="rwkv_prep",
    )(P, P, P, P, consts["mu_k"], consts["mu_l"],
      consts["w0"], consts["a0"], consts["k_k"], consts["k_a"],
      consts["w2p"], consts["a2p"], consts["g2p"], consts["bd"])
    return outs


def _scan_kernel(r_ref, v_ref, rp_ref, vp_ref, mur_ref, muv_ref, k_ref, kk_ref, b_ref, lw_ref,
                 g_ref, rk_ref, lnw_ref, lnb_ref, o_ref, h_ref, *, nc, npair):
    L = CHUNK
    half = L // 2

    @pl.when(pl.program_id(2) == 0)
    def _():
        h_ref[...] = jnp.zeros_like(h_ref)

    ri = lax.broadcasted_iota(jnp.int32, (L, L), 0)
    ci = lax.broadcasted_iota(jnp.int32, (L, L), 1)
    lower_incl = ri >= ci
    lower_strict = ri > ci
    eye = ri == ci
    same_head = (ri < HEAD_DIM) == (ci < HEAD_DIM)
    tri01 = lower_incl.astype(BF16)
    eye_f = eye.astype(F32)
    lane = lax.broadcasted_iota(jnp.int32, (1, LANES), 1)
    head0 = lane < HEAD_DIM
    lane2 = lax.broadcasted_iota(jnp.int32, (1, 2 * LANES), 1)
    head0_2 = (lane2 % LANES) < HEAD_DIM
    tri_stack = jnp.concatenate([lower_strict, lower_incl], axis=0)
    chunks = range(nc * npair)
    pairs = [(c, h) for c in chunks for h in range(2)]

    def lanes_of(u):
        return slice((u // nc) * LANES, (u // nc + 1) * LANES)

    def ld(ref, u):
        return ref[pl.ds((u % nc) * L, L), lanes_of(u)]

    first = pl.program_id(2) == 0
    r_tile = _token_shift_mix(r_ref, rp_ref, mur_ref, first)
    v_tile = _token_shift_mix(v_ref, vp_ref, muv_ref, first)
    r = [r_tile[(u % nc) * L:(u % nc + 1) * L, lanes_of(u)] for u in chunks]
    v = [v_tile[(u % nc) * L:(u % nc + 1) * L, lanes_of(u)] for u in chunks]
    k = [ld(k_ref, c) for c in chunks]
    kk = [ld(kk_ref, c) for c in chunks]
    b = [ld(b_ref, c) for c in chunks]
    lw = [ld(lw_ref, c) for c in chunks]

    cum = [_exact01_dot(tri01, lw[c]) for c in chunks]
    lhs_mid, rhs_mid, kd_bf, r_dec, k_end, b_end, p_end, v_bf = [], [], [], [], [], [], [], []
    for c in chunks:
        mid = cum[c][half - 1:half, :]
        end = cum[c][L - 1:L, :]
        r_mid = r[c] * jnp.exp(cum[c] - mid)
        kk_mid = kk[c] * jnp.exp(cum[c] - lw[c] - mid)
        e_back = jnp.exp(mid - cum[c])
        lhs_mid.append(jnp.concatenate([kk_mid, r_mid], axis=0))
        rhs_mid.append(jnp.concatenate([b[c] * e_back, k[c] * e_back], axis=0).astype(BF16))
        kd_bf.append((kk[c] * jnp.exp(cum[c] - lw[c])).astype(BF16))
        r_dec.append(r[c] * jnp.exp(cum[c]))
        e_end = jnp.exp(end - cum[c])
        k_end.append((k[c] * e_end).astype(BF16))
        b_end.append((b[c] * e_end).astype(BF16))
        p_end.append(jnp.exp(end))
        v_bf.append(v[c].astype(BF16))

    aa = {}
    for (c, h) in pairs:
        hm = head0 if h == 0 else jnp.logical_not(head0)
        aa[c, h] = _dot_nt(jnp.where(hm, lhs_mid[c], 0.0).astype(BF16), rhs_mid[c])
    def side(f):
        return jnp.concatenate([f(0), f(1)], axis=1)

    tri2 = jnp.concatenate([tri_stack, tri_stack], axis=1)
    strict2 = jnp.concatenate([lower_strict, lower_strict], axis=1)
    incl2 = jnp.concatenate([lower_incl, lower_incl], axis=1)
    def by_head_rows(x):
        hm = head0 if x.shape[1] == LANES else head0_2
        zero = jnp.zeros_like(x)
        return jnp.concatenate([jnp.where(hm, x, zero), jnp.where(hm, zero, x)], axis=0)

    a_kb = [jnp.where(strict2, side(lambda h: aa[c, h][:L, :L]), 0.0) for c in chunks]
    a_rb = [jnp.where(incl2, side(lambda h: aa[c, h][L:, :L]), 0.0).astype(BF16) for c in chunks]
    a_xk = [jnp.where(tri2, side(lambda h: aa[c, h][:, L:]), 0.0).astype(BF16) for c in chunks]
    avr = [_dot(a_xk[c], by_head_rows(v_bf[c])) for c in chunks]

    hl = L // 2
    blk_r = lax.broadcasted_iota(jnp.int32, (2 * L, 2 * L), 0) // hl
    blk_c = lax.broadcasted_iota(jnp.int32, (2 * L, 2 * L), 1) // hl
    diag4 = blk_r == blk_c
    below4 = jnp.logical_and(blk_r == blk_c + 1, blk_c % 2 == 0)
    eye4 = (lax.broadcasted_iota(jnp.int32, (hl, 2 * L), 0)
            == lax.broadcasted_iota(jnp.int32, (hl, 2 * L), 1) % hl).astype(F32)

    def tile4(x, keep):
        return jnp.where(keep, jnp.concatenate([x, x, x, x], axis=0), jnp.zeros((), x.dtype))

    r_loc = lax.broadcasted_iota(jnp.int32, (hl, 2 * L), 0)
    c_loc = lax.broadcasted_iota(jnp.int32, (hl, 2 * L), 1) % hl
    def siblings_at(m):
        return jnp.logical_and((r_loc // m) % 2 == 1, c_loc // m == r_loc // m - 1)

    t, a21, pd = [], [], []
    for c in chunks:
        pd.append(jnp.where(head0_2, a_kb[c][:hl], a_kb[c][hl:]))
        a21.append(jnp.where(head0_2, a_kb[c][hl:], 0.0).astype(BF16))
        t.append(eye4 - jnp.where(siblings_at(1), pd[c], 0.0))
    m = 2
    while m < hl:
        siblings = siblings_at(m)
        for c in chunks:
            l_m = jnp.where(siblings, pd[c], 0.0).astype(BF16)
            x = _dot(l_m, tile4(t[c].astype(BF16), diag4))
            t[c] = t[c] - _dot(t[c].astype(BF16), tile4(x.astype(BF16), diag4))
        m *= 2
    for c in chunks:
        td_bf = t[c].astype(BF16)
        y = _dot(a21[c], tile4(jnp.where(head0_2, td_bf, jnp.zeros((), BF16)), diag4))
        z = _dot(td_bf, tile4(y.astype(BF16), below4))
        t[c] = jnp.concatenate([jnp.where(head0_2, t[c], 0.0), jnp.where(head0_2, -z, t[c])], axis=0)

    rq, y0, gmat, cmat = [], [], [], []
    for c in chunks:
        av = avr[c][:L]
        arkv = avr[c][L:]
        x = jnp.concatenate([kd_bf[c], av.astype(BF16)], axis=1)
        w_bf = _dot(t[c].astype(BF16), by_head_rows(x)).astype(BF16)
        aw = _dot(a_rb[c], by_head_rows(w_bf))
        bw = _dot_tn(b_end[c], w_bf)
        kv = _dot_tn(k_end[c], v_bf[c])
        rq.append(r_dec[c] - aw[:, :LANES])
        y0.append(arkv - aw[:, LANES:])
        gmat.append(eye_f * p_end[c] - jnp.where(same_head, bw[:, :LANES], 0.0))
        cmat.append(jnp.where(same_head, kv - bw[:, LANES:], 0.0))

    bd2 = ((lax.broadcasted_iota(jnp.int32, (2 * LANES, 2 * LANES), 0) // HEAD_DIM)
           == (lax.broadcasted_iota(jnp.int32, (2 * LANES, 2 * LANES), 1) // HEAD_DIM)).astype(BF16)

    def head_sum_all(x):
        n = x.shape[0] // 2
        s2 = _dot_exact01(jnp.concatenate([x[:n], x[n:]], axis=1), bd2)
        return jnp.concatenate([s2[:, :LANES], s2[:, LANES:]], axis=0)

    rkr = head_sum_all(jnp.concatenate(
        [r[u] * k[u] * rk_ref[:, lanes_of(u)] for u in chunks], axis=0))
    bonus = [rkr[u * L:(u + 1) * L] * v[u] for u in chunks]

    inv_n = 1.0 / HEAD_DIM
    hstate = [h_ref[q] for q in range(npair)]
    ys = [None] * len(chunks)
    for c in range(nc):
        for q in range(npair):
            u = q * nc + c
            res = _dot(jnp.concatenate([rq[u], gmat[u]], axis=0).astype(BF16),
                       hstate[q].astype(BF16))
            ys[u] = res[:L] + y0[u]
            hstate[q] = res[L:] + cmat[u]
    for q in range(npair):
        h_ref[q] = hstate[q]
    y = jnp.concatenate(ys, axis=0)
    mean = head_sum_all(y) * inv_n
    yc = y - mean
    var = head_sum_all(yc * yc) * inv_n
    yn = yc * lax.rsqrt(var + GN_EPS)
    for u in chunks:
        rows = pl.ds((u % nc) * L, L)
        o_ref[rows, lanes_of(u)] = (
            (yn[u * L:(u + 1) * L] * lnw_ref[:, lanes_of(u)] + lnb_ref[:, lanes_of(u)] + bonus[u])
            * g_ref[rows, lanes_of(u)]).astype(BF16)


def _scan(P, col_r, col_v, mu_r, mu_v, arrs, rk, lnw, lnb, batch, seq, nc=4, npair=4):
    k, kk, b, lw, g = [a.reshape(batch, seq, a.shape[1]) for a in arrs]
    P3 = P.reshape(batch, seq, P.shape[1])
    C = k.shape[2]
    width = LANES * npair
    rows = CHUNK * nc
    blk = pl.BlockSpec((None, rows, width), lambda bi, p, c: (bi, c, p))
    par = pl.BlockSpec((1, width), lambda bi, p, c: (0, p))

    def raw(col):
        return pl.BlockSpec((None, rows, width), lambda bi, p, c: (bi, c, col // width + p))

    def raw_prev(col):
        return pl.BlockSpec((None, 8, width),
                            lambda bi, p, c: (bi, jnp.maximum(c * (rows // 8) - 1, 0),
                                              col // width + p))

    out = pl.pallas_call(
        functools.partial(_scan_kernel, nc=nc, npair=npair),
        out_shape=jax.ShapeDtypeStruct((batch, seq, C), BF16),
        grid=(batch, C // width, seq // rows),
        in_specs=[raw(col_r), raw(col_v), raw_prev(col_r), raw_prev(col_v), par, par]
                 + [blk] * 5 + [par, par, par],
        out_specs=blk,
        scratch_shapes=[pltpu.VMEM((npair, LANES, LANES), F32)],
        compiler_params=_cparams(("parallel", "parallel", "arbitrary")),
        name="rwkv_scan",
    )(P3, P3, P3, P3, mu_r, mu_v, k, kk, b, lw, g, rk, lnw, lnb)
    return out.reshape(batch * seq, C)


def _sb_kernel(q_ref, k_ref, v_ref, o_ref, *, npair, nq):
    tq = tk = SB_BLOCK
    nu = SB_BLOCKS_PER_STEP
    i = pl.program_id(2)
    lane = lax.broadcasted_iota(jnp.int32, (1, LANES), 1)
    head0 = lane < HEAD_DIM
    scale = jnp.asarray(1.0 / math.sqrt(HEAD_DIM), BF16)
    streams = [(s, hp) for s in range(nq) for hp in range(npair)]
    ns = len(streams)
    q = [q_ref[s * tq:(s + 1) * tq, hp * LANES:(hp + 1) * LANES] * scale for (s, hp) in streams]
    r2 = lax.broadcasted_iota(jnp.int32, (2 * tk, 2 * tk), 0)
    c2 = lax.broadcasted_iota(jnp.int32, (2 * tk, 2 * tk), 1)
    after01 = jnp.logical_and(r2 > c2, (r2 < tk) == (c2 < tk)).astype(BF16)
    key_minus_query = (lax.broadcasted_iota(jnp.int32, (tq, 2 * tk), 1) % tk
                       - lax.broadcasted_iota(jnp.int32, (tq, 2 * tk), 0))
    units = [(p, u) for p in range(ns) for u in range(nu)]

    def by_head_rows(x):
        zero = jnp.zeros_like(x)
        return jnp.concatenate([jnp.where(head0, x, zero), jnp.where(head0, zero, x)], axis=0)

    def both(c0, c1):
        return jnp.concatenate([jnp.broadcast_to(c0, (tq, tk)), jnp.broadcast_to(c1, (tq, tk))],
                               axis=1)

    def body(carry):
        jt, _, offs, accs = carry
        z, sp, log_keep, mask, vb = {}, {}, {}, {}, {}
        for (p, u) in units:
            s, hp = streams[p]
            jb = jt - (nq - 1 - s) - u
            start = pl.multiple_of(jnp.maximum(jb, 0) * tk, tk)
            lanes = slice(hp * LANES, (hp + 1) * LANES)
            kb = by_head_rows(k_ref[pl.ds(start, tk), lanes])
            vb[p, u] = by_head_rows(v_ref[pl.ds(start, tk), lanes])
            zu = _dot_nt(q[p], kb)
            if u > 0:
                zu = zu + jnp.where(jb >= 0, 0.0, -1e30)
            spu = jnp.maximum(zu, 0.0) + jnp.log(1.0 + jnp.exp(-jnp.abs(zu)))
            z[p, u], sp[p, u] = zu, spu
            if u == 0:
                q_start = (i * nq + s) * tq
                mask[p, u] = key_minus_query < jnp.where(jb >= 0, q_start - jb * tk, -tq)
                log_keep[p, u] = jnp.where(mask[p, u], -spu, 0.0)
            else:
                log_keep[p, u] = -spu
        parts = []
        for key in units:
            parts.extend(_split2(log_keep[key]))
        loc = _dot(jnp.concatenate(parts, axis=0), after01)
        offs, accs = list(offs), list(accs)
        for n, (p, u) in enumerate(units):
            local = loc[2 * n * tq:(2 * n + 1) * tq] + loc[(2 * n + 1) * tq:(2 * n + 2) * tq]
            later = local + both(offs[2 * p], offs[2 * p + 1])
            att = jnp.exp(z[p, u] - sp[p, u] + later)
            if u == 0:
                att = jnp.where(mask[p, u], att, 0.0)
            accs[p] = accs[p] + _dot(att.astype(BF16), vb[p, u])
            offs[2 * p] = offs[2 * p] + (local[:, 0:1] + log_keep[p, u][:, 0:1])
            offs[2 * p + 1] = offs[2 * p + 1] + (local[:, tk:tk + 1] + log_keep[p, u][:, tk:tk + 1])
        top = offs[0]
        for o in offs[1:]:
            top = jnp.maximum(top, o)
        alive = (jnp.max(top) > -SB_UNDERFLOW).astype(jnp.int32)
        return jt - nu, alive, tuple(offs), tuple(accs)

    def cond(carry):
        jt, alive = carry[0], carry[1]
        return jnp.logical_and(jt >= 0, alive > 0)

    init = (i * nq + nq - 1, jnp.int32(1),
            tuple(jnp.zeros((tq, 1), F32) for _ in range(2 * ns)),
            tuple(jnp.zeros((tq, LANES), F32) for _ in range(ns)))
    accs = lax.while_loop(cond, body, body(init))[3]
    for p, (s, hp) in enumerate(streams):
        o_ref[s * tq:(s + 1) * tq, hp * LANES:(hp + 1) * LANES] = accs[p].astype(BF16)


def _stick_breaking(qkv, width, npair=2, nq=2):
    B, S, _ = qkv.shape
    w = LANES * npair
    nblk = width // w
    rows = SB_BLOCK * nq
    out = pl.pallas_call(
        functools.partial(_sb_kernel, npair=npair, nq=nq),
        out_shape=jax.ShapeDtypeStruct((B, S, width), BF16),
        grid=(B, nblk, S // rows),
        in_specs=[pl.BlockSpec((None, rows, w), lambda b, p, i: (b, i, p)),
                  pl.BlockSpec((None, S, w), lambda b, p, i: (b, 0, nblk + p)),
                  pl.BlockSpec((None, S, w), lambda b, p, i: (b, 0, 2 * nblk + p))],
        out_specs=pl.BlockSpec((None, rows, w), lambda b, p, i: (b, i, p)),
        compiler_params=_cparams(("parallel", "parallel", "arbitrary")),
        name="stick_breaking",
    )(qkv, qkv, qkv)
    return out.reshape(B * S, width)


def _merge_kernel(ya_ref, yb_ref, ga_ref, gb_ref, x_ref, mod_ref, g1_ref,
                  g2_ref, wua_ref, wub_ref, wo_ref, o_ref, hf_ref, *, d):
    ua = _dot(ya_ref[...], wua_ref[...])
    ub = _dot(yb_ref[...], wub_ref[...])
    merged = jax.nn.sigmoid(ga_ref[...]) * ua + jax.nn.sigmoid(gb_ref[...]) * ub
    mix = _dot(merged.astype(BF16), wo_ref[...])
    gate_m = mod_ref[:, 2 * d:3 * d]
    x1 = x_ref[...] + gate_m * _rms(mix, g1_ref[...])
    o_ref[...] = x1
    shift_f = mod_ref[:, 3 * d:4 * d]
    scale_f = mod_ref[:, 4 * d:5 * d]
    hf_ref[...] = (_rms(x1, g2_ref[...]) * (1.0 + scale_f) + shift_f).astype(BF16)


def _merge(ya, yb, P, x2, mod3, g1, g2, wua, wub, wo, seq, tm=256):
    T, D = x2.shape
    C = ya.shape[1]

    def const(shape):
        return pl.BlockSpec(shape, lambda i: (0, 0), pipeline_mode=pl.Buffered(1))

    return pl.pallas_call(
        functools.partial(_merge_kernel, d=D),
        out_shape=[jax.ShapeDtypeStruct((T, D), F32), jax.ShapeDtypeStruct((T, D), BF16)],
        grid=(T // tm,),
        in_specs=[pl.BlockSpec((tm, C), lambda i: (i, 0)),
                  pl.BlockSpec((tm, C), lambda i: (i, 0)),
                  pl.BlockSpec((tm, D), lambda i: (i, 0)),
                  pl.BlockSpec((tm, D), lambda i: (i, 1)),
                  pl.BlockSpec((tm, D), lambda i: (i, 0)),
                  pl.BlockSpec((None, 1, mod3.shape[2]), lambda i: ((i * tm) // seq, 0, 0)),
                  pl.BlockSpec((1, D), lambda i: (0, 0)),
                  pl.BlockSpec((1, D), lambda i: (0, 0)),
                  const((C, D)), const((C, D)), const((D, D))],
        out_specs=[pl.BlockSpec((tm, D), lambda i: (i, 0)), pl.BlockSpec((tm, D), lambda i: (i, 0))],
        compiler_params=_cparams(("parallel",)),
        name="merge_out",
    )(ya, yb, P, P, x2, mod3, g1, g2, wua, wub, wo)


def _mlp_kernel(xp_ref, hf_ref, modp_ref, g3_ref, w1_ref, w2_ref, o_ref, acc_ref, *, d, rc):
    i = pl.program_id(0)
    j = pl.program_id(1)
    n = pl.num_programs(0) - 1
    slot = i % 2

    @pl.when(jnp.logical_and(i == 0, j == 0))
    def _():
        acc_ref[1] = jnp.zeros(acc_ref.shape[1:], F32)

    def finish_prev_rows():
        rows = pl.ds(pl.multiple_of(j * rc, rc), rc)
        gate_f = modp_ref[:, 5 * d:6 * d]
        o_ref[rows, :] = xp_ref[rows, :] + gate_f * _rms(acc_ref[1 - slot, rows, :], g3_ref[...])

    def partial_out():
        u = jnp.maximum(_dot(hf_ref[...], w1_ref[...]), 0.0)
        return _dot((u * u).astype(BF16), w2_ref[...])

    @pl.when(jnp.logical_and(i < n, j == 0))
    def _():
        finish_prev_rows()
        acc_ref[slot] = partial_out()

    @pl.when(jnp.logical_and(i < n, j > 0))
    def _():
        finish_prev_rows()
        acc_ref[slot] += partial_out()

    @pl.when(i == n)
    def _():
        finish_prev_rows()


def _mlp(x2, hf, mod3, g3, w1, w2, seq, tm=512, tf=1024):
    T, D = x2.shape
    F = w1.shape[1]
    n, nj = T // tm, F // tf
    assert tm % nj == 0 and (tm // nj) % 8 == 0

    def cur(i):
        return jnp.minimum(i, n - 1)

    def prv(i):
        return jnp.maximum(i - 1, 0)

    def ff(i, j):
        return jnp.where(i < n, j, nj - 1)

    return pl.pallas_call(
        functools.partial(_mlp_kernel, d=D, rc=tm // nj),
        out_shape=jax.ShapeDtypeStruct((T, D), F32),
        grid=(n + 1, nj),
        in_specs=[pl.BlockSpec((tm, D), lambda i, j: (prv(i), 0)),
                  pl.BlockSpec((tm, D), lambda i, j: (cur(i), 0)),
                  pl.BlockSpec((None, 1, mod3.shape[2]), lambda i, j: ((prv(i) * tm) // seq, 0, 0)),
                  pl.BlockSpec((1, D), lambda i, j: (0, 0)),
                  pl.BlockSpec((D, tf), lambda i, j: (0, ff(i, j))),
                  pl.BlockSpec((tf, D), lambda i, j: (ff(i, j), 0))],
        out_specs=pl.BlockSpec((tm, D), lambda i, j: (prv(i), 0)),
        scratch_shapes=[pltpu.VMEM((2, tm, D), F32)],
        compiler_params=_cparams(("arbitrary", "arbitrary")),
        name="mlp",
    )(x2, hf, mod3, g3, w1, w2)


def _pad_rows(w, offset, total):
    return jnp.zeros((total, w.shape[1]), w.dtype).at[offset:offset + w.shape[0]].set(w)


def kernel(x, c, w_ada, b_ada, norm_g, w_in, mu_shift, w0, w2, a0, a2, g2, k_k, k_a,
           r_k, ln_x_w, ln_x_b, w_up_rwkv, w_up_sb, w_out, w_mlp_in, w_mlp_out):
    B, S, D = x.shape
    depth = w_in.shape[0]
    C = w0.shape[1]
    W = w_up_sb.shape[1]
    rwkv_cols = 3 * C + DECAY_LORA + ICLR_LORA + GATE_LORA
    sb_cols = 3 * W
    n_lora = DECAY_LORA + ICLR_LORA + GATE_LORA
    assert C == 1024 and W == 1024 and D == 2 * C and S % (2 * CHUNK) == 0

    hi = lax.broadcasted_iota(jnp.int32, (LANES, LANES), 0) // HEAD_DIM
    hj = lax.broadcasted_iota(jnp.int32, (LANES, LANES), 1) // HEAD_DIM
    bd = (hi == hj).astype(BF16)

    x2 = x.reshape(B * S, D)
    for l in range(depth):
        mod = _ada(c, w_ada[l], b_ada[l])
        mod3 = mod.reshape(B, 1, mod.shape[1])

        g_off = rwkv_cols + sb_cols
        n_main_cols = 2 * D + 3 * C + LORA_PAD
        w_perm = _regroup_rows(
            w_in[l].T,
            [(g_off, 2 * D), (0, 3 * C), (3 * C, LORA_PAD), (rwkv_cols, sb_cols)],
            zero_lo=2 * D + 3 * C + n_lora, zero_hi=n_main_cols)
        P, qkv = _inproj(x2, mod3, norm_g[l, 0].reshape(1, D), w_perm, n_main_cols, S)

        mu = mu_shift[l]
        consts = dict(
            col_k=2 * D // C + 1, col_l=(2 * D + 3 * C) // LORA_PAD,
            mu_k=mu[None, C:2 * C],
            mu_l=jnp.pad(mu[None, 3 * C:], ((0, 0), (0, LORA_PAD - n_lora))),
            w0=w0[l][None], a0=a0[l][None], k_k=k_k[l][None], k_a=k_a[l][None],
            w2p=_pad_rows(w2[l], 0, LORA_PAD).astype(BF16),
            a2p=_pad_rows(a2[l], DECAY_LORA, LORA_PAD).astype(BF16),
            g2p=_pad_rows(g2[l], DECAY_LORA + ICLR_LORA, LORA_PAD).astype(BF16),
            bd=bd)
        prep = _prep(P, S, consts)
        ya = _scan(P, 2 * D, 2 * D + 2 * C, mu[None, :C], mu[None, 2 * C:3 * C], prep,
                   r_k[l].reshape(1, C), ln_x_w[l][None], ln_x_b[l][None], B, S)

        yb = _stick_breaking(qkv.reshape(B, S, 3 * W), W)

        x2, hf = _merge(ya, yb, P, x2, mod3, norm_g[l, 1].reshape(1, D),
                        norm_g[l, 2].reshape(1, D), w_up_rwkv[l].astype(BF16),
                        w_up_sb[l].astype(BF16), w_out[l].astype(BF16), S)
        x2 = _mlp(x2, hf, mod3, norm_g[l, 3].reshape(1, D),
                  w_mlp_in[l].astype(BF16), w_mlp_out[l].astype(BF16), S)
    return x2.reshape(B, S, D)
```
